```python
import math
import jax
import jax.numpy as jnp
from jax import lax
import numpy as np

D_MODEL = 1024
BATCH = 4
SEQ = 4096
DEPTH = 4

N_MIXERS = 4
GRID_W = 64
EPS = 1e-6
NEG_BIG = -1e30

SSD_DI = 2 * D_MODEL
SSD_HEADDIM = 64
SSD_HEADS = SSD_DI // SSD_HEADDIM
SSD_GROUPS = 4
SSD_HPG = SSD_HEADS // SSD_GROUPS
SSD_STATE = 128
SSD_CONV = 7
SSD_CHUNK = 128
SSD_CONV_CH = SSD_DI + 2 * SSD_GROUPS * SSD_STATE
SSD_IN = SSD_DI + SSD_CONV_CH + 2 * SSD_HEADS

HG_EXPAND = 128
HG_HEADS = D_MODEL // HG_EXPAND
HG_W = HG_HEADS * HG_EXPAND
HG_DV = HG_W // HG_HEADS
HG_CHUNK = 32
HG_IN = 5 * HG_W

AT_HEADS = 16
AT_KV = 8
AT_GRP = AT_HEADS // AT_KV
AT_HD = 128
AT_QBLK = 128
ROPE_THETA = 10000.0
ROPE_AXIS = AT_HD // 2
AT_QW = AT_HEADS * AT_HD
AT_KW = AT_KV * AT_HD
AT_IN = 2 * AT_QW + 2 * AT_KW

DL_PAIRS = ((128, 1), (512, 4), (2048, 16))
DL_HEADS = 16
DL_HD = 64
DL_W = DL_HEADS * DL_HD
DL_IN = 3 * len(DL_PAIRS) * DL_W + DL_W

REL_BUCKETS = 32
REL_MAX_DIST = 1024

N_SSD_LAYERS = (DEPTH - 0 + N_MIXERS - 1) // N_MIXERS
N_HG_LAYERS = (DEPTH - 1 + N_MIXERS - 1) // N_MIXERS
N_AT_LAYERS = (DEPTH - 2 + N_MIXERS - 1) // N_MIXERS
N_DL_LAYERS = (DEPTH - 3 + N_MIXERS - 1) // N_MIXERS

kernel_name = "hybrid_bidir_ssd_hgrn2_axialgqa_dilated"

F32 = jnp.float32


def rms_norm(x, g):
    xf = x.astype(F32)
    y = xf * lax.rsqrt(jnp.mean(xf * xf, axis=-1, keepdims=True) + EPS)
    return (y * g.astype(F32)).astype(x.dtype)


def centred_depthwise_conv(x, w, b):
    C = x.shape[-1]
    pad = w.shape[0] // 2
    y = lax.conv_general_dilated(x, w[:, None, :].astype(x.dtype), window_strides=(1,),
                                 padding=((pad, pad),), dimension_numbers=('NWC', 'WIO', 'NWC'),
                                 feature_group_count=C)
    return y + b.astype(x.dtype)


def exp_segsum(a):
    T = a.shape[-1]
    cs = jnp.cumsum(a, axis=-1)
    diff = cs[..., :, None] - cs[..., None, :]
    mask = jnp.tril(jnp.ones((T, T), dtype=bool))
    return jnp.exp(jnp.where(mask, diff, -jnp.inf))


def ssd_chunked(xdt, dA, Bm, Cm):
    b, S, G, J, P = xdt.shape
    N = Bm.shape[-1]
    Q = SSD_CHUNK
    nc = S // Q
    x = xdt.reshape(b, nc, Q, G, J, P)
    a = dA.reshape(b, nc, Q, G, J).transpose(0, 3, 4, 1, 2)
    Bc = Bm.reshape(b, nc, Q, G, N)
    Cc = Cm.reshape(b, nc, Q, G, N)
    a_cum = jnp.cumsum(a, axis=-1)
    L = exp_segsum(a)
    CB = jnp.einsum('bclgn,bcsgn->bcgls', Cc, Bc)
    y_diag = jnp.einsum('bcgls,bgjcls,bcsgjp->bclgjp', CB, L, x)
    decay_states = jnp.exp(a_cum[..., -1:] - a_cum)
    states = jnp.einsum('bclgn,bgjcl,bclgjp->bcgjpn', Bc, decay_states, x)
    states = jnp.concatenate([jnp.zeros_like(states[:, :1]), states], axis=1)
    chunk_decay = exp_segsum(jnp.pad(a_cum[..., -1], ((0, 0), (0, 0), (0, 0), (1, 0))))
    start_states = jnp.einsum('bgjzc,bcgjpn->bzgjpn', chunk_decay, states)[:, :-1]
    y_off = jnp.einsum('bclgn,bcgjpn,bgjcl->bclgjp', Cc, start_states, jnp.exp(a_cum))
    return (y_diag + y_off).reshape(b, S, G, J, P)


def ssd_mixer(h, w_in, conv_w, conv_b, dt_bias, a_log, d_skip, norm_g, w_out):
    b, S, _ = h.shape
    GN = SSD_GROUPS * SSD_STATE
    u = h @ w_in
    z = u[..., :SSD_DI]
    xbc = u[..., SSD_DI:SSD_DI + SSD_CONV_CH]
    dt_raw = u[..., SSD_DI + SSD_CONV_CH:].reshape(b, S, 2, SSD_HEADS)
    xbc = jax.nn.silu(centred_depthwise_conv(xbc, conv_w, conv_b))
    xs = xbc[..., :SSD_DI].reshape(b, S, SSD_GROUPS, SSD_HPG, SSD_HEADDIM)
    Bm = xbc[..., SSD_DI:SSD_DI + GN].reshape(b, S, SSD_GROUPS, SSD_STATE)
    Cm = xbc[..., SSD_DI + GN:].reshape(b, S, SSD_GROUPS, SSD_STATE)
    dt = jax.nn.softplus(dt_raw.astype(F32) + dt_bias.astype(F32))
    A = -jnp.exp(a_log.astype(F32))
    dA = (dt * A).reshape(b, S, 2, SSD_GROUPS, SSD_HPG)
    dtg = dt.reshape(b, S, 2, SSD_GROUPS, SSD_HPG)
    x_f = xs * dtg[:, :, 0, :, :, None]
    x_b = xs * dtg[:, :, 1, :, :, None]
    y_fwd = ssd_chunked(x_f, dA[:, :, 0], Bm, Cm)
    y_bwd = jnp.flip(ssd_chunked(jnp.flip(x_b, 1), jnp.flip(dA[:, :, 1], 1),
                                 jnp.flip(Bm, 1), jnp.flip(Cm, 1)), 1)
    y = y_fwd + y_bwd + xs * d_skip.reshape(SSD_GROUPS, SSD_HPG)[..., None]
    y = y.reshape(b, S, SSD_DI) * jax.nn.silu(z)
    gs = SSD_DI // SSD_GROUPS
    y = rms_norm(y.reshape(b, S, SSD_GROUPS, gs), norm_g.reshape(SSD_GROUPS, gs))
    return y.reshape(b, S, SSD_DI).astype(h.dtype) @ w_out


def hgrn2_chunked(q, k, v, g):
    b, S, h, dk = q.shape
    dv = v.shape[-1]
    C = HG_CHUNK
    nc = S // C

    def to_chunks(t):
        return t.reshape(b, nc, C, h, t.shape[-1]).transpose(1, 0, 3, 2, 4)

    mask = jnp.tril(jnp.ones((C, C), dtype=bool))

    def step(state, inp):
        qi, ki, vi, gi = inp
        G = jnp.cumsum(gi, axis=-2)
        Gr = G[..., C // 2:C // 2 + 1, :]
        q_t = qi * jnp.exp(G - Gr)
        k_t = ki * jnp.exp(Gr - G)
        att = jnp.where(mask, jnp.einsum('bhid,bhjd->bhij', q_t, k_t), 0.0)
        o = (jnp.einsum('bhij,bhjv->bhiv', att, vi)
             + jnp.einsum('bhid,bhdv->bhiv', qi * jnp.exp(G), state))
        G_last = G[..., -1:, :]
        new_state = (jnp.exp(G_last)[..., 0, :, None] * state
                     + jnp.einsum('bhjd,bhjv->bhdv', ki * jnp.exp(G_last - G), vi))
        return new_state, o

    s0 = jnp.zeros((b, h, dk, dv), F32)
    _, o = lax.scan(step, s0, (to_chunks(q), to_chunks(k), to_chunks(v), to_chunks(g)))
    return o.transpose(1, 0, 3, 2, 4).reshape(b, S, h, dv)


def hgrn2_mixer(h, lb, w_in, norm_g, w_out):
    b, S, _ = h.shape
    u = h @ w_in
    q, f_fwd, f_bwd, inp, gate = jnp.split(u, 5, axis=-1)
    shp = (b, S, HG_HEADS, HG_EXPAND)
    q = jax.nn.silu(q).astype(F32).reshape(shp)
    v = inp.astype(F32).reshape(b, S, HG_HEADS, HG_DV)
    lbh = lb.reshape(HG_HEADS, HG_EXPAND)

    def direction(fpre, reverse):
        f = lbh + (1.0 - lbh) * jax.nn.sigmoid(fpre.astype(F32).reshape(shp))
        args = (q, 1.0 - f, v, jnp.log(f))
        if reverse:
            return jnp.flip(hgrn2_chunked(*[jnp.flip(a, 1) for a in args]), 1)
        return hgrn2_chunked(*args)

    o = direction(f_fwd, False) + direction(f_bwd, True)
    o = rms_norm(o, norm_g.reshape(HG_HEADS, HG_DV)).reshape(b, S, HG_W).astype(h.dtype)
    return (o * jax.nn.silu(gate)) @ w_out


def axial_rope_tables(S):
    rows = S // GRID_W
    row = jnp.repeat(jnp.arange(rows), GRID_W).astype(F32)
    col = (jnp.arange(S) % GRID_W).astype(F32)
    inv = ROPE_THETA ** (-jnp.arange(0, ROPE_AXIS, 2, dtype=F32) / ROPE_AXIS)
    ang = jnp.stack([row[:, None] * inv, col[:, None] * inv], axis=1)
    return jnp.cos(ang), jnp.sin(ang)


def apply_axial_rope(x, cos, sin):
    shp = x.shape
    xf = x.astype(F32).reshape(*shp[:-1], 2, 2, ROPE_AXIS // 2)
    x1, x2 = xf[..., 0, :], xf[..., 1, :]
    c, s = cos[:, None], sin[:, None]
    out = jnp.stack([x1 * c - x2 * s, x2 * c + x1 * s], axis=-2)
    return out.reshape(shp).astype(x.dtype)


def gqa_mixer(h, w_in, q_g, k_g, w_out):
    b, S, _ = h.shape
    u = h @ w_in
    q = u[..., :AT_QW].reshape(b, S, AT_HEADS, AT_HD)
    k = u[..., AT_QW:AT_QW + AT_KW].reshape(b, S, AT_KV, AT_HD)
    v = u[..., AT_QW + AT_KW:AT_QW + 2 * AT_KW].reshape(b, S, AT_KV, AT_HD)
    gate = u[..., AT_QW + 2 * AT_KW:]
    cos, sin = axial_rope_tables(S)
    q = apply_axial_rope(rms_norm(q, q_g), cos, sin) * (AT_HD ** -0.5)
    k = apply_axial_rope(rms_norm(k, k_g), cos, sin)
    nq = S // AT_QBLK
    qb = q.reshape(b, nq, AT_QBLK, AT_KV, AT_GRP, AT_HD).transpose(1, 0, 2, 3, 4, 5)

    def block(qi):
        s = jnp.einsum('bqkgd,bskd->bkgqs', qi, k).astype(F32)
        p = jax.nn.softmax(s, axis=-1).astype(v.dtype)
        return jnp.einsum('bkgqs,bskd->bqkgd', p, v)

    o = lax.map(block, qb)
    o = o.transpose(1, 0, 2, 3, 4, 5).reshape(b, S, AT_QW)
    return (o * jax.nn.silu(gate)) @ w_out


def t5_bucket(rel):
    half = REL_BUCKETS // 2
    exact = half // 2
    n = jnp.abs(rel)
    large = exact + (jnp.log(jnp.maximum(n, 1).astype(F32) / exact)
                     / math.log(REL_MAX_DIST / exact) * (half - exact)).astype(jnp.int32)
    large = jnp.minimum(large, half - 1)
    return jnp.where(rel > 0, half, 0) + jnp.where(n < exact, n, large)


def dilated_group(q, k, v, dil, steps, rel_bias):
    b, S, h, e = q.shape
    Ls = S // dil
    blk = steps
    nb = -(-Ls // blk)
    Lp = nb * blk

    def sub(t):
        return t.reshape(b, Ls, dil, h, e).transpose(0, 2, 1, 3, 4)

    qb = jnp.pad(sub(q), ((0, 0), (0, 0), (0, Lp - Ls), (0, 0), (0, 0))).reshape(b, dil, nb, blk, h, e)
    kpad = ((0, 0), (0, 0), (blk, Lp - Ls + blk), (0, 0), (0, 0))
    kp = jnp.pad(sub(k), kpad).reshape(b, dil, nb + 2, blk, h, e)
    vp = jnp.pad(sub(v), kpad).reshape(b, dil, nb + 2, blk, h, e)
    kb = jnp.concatenate([kp[:, :, 0:nb], kp[:, :, 1:nb + 1], kp[:, :, 2:nb + 2]], axis=3)
    vb = jnp.concatenate([vp[:, :, 0:nb], vp[:, :, 1:nb + 1], vp[:, :, 2:nb + 2]], axis=3)
    i = jnp.arange(blk)[:, None]
    j = jnp.arange(3 * blk)[None, :]
    dm = j - blk - i
    m_k = jnp.arange(nb)[:, None, None] * blk + j[None] - blk
    mask = (jnp.abs(dm) <= steps)[None] & (m_k >= 0) & (m_k < Ls)
    bias = rel_bias[t5_bucket(dm * dil)].transpose(2, 0, 1).astype(F32)
    s = jnp.einsum('bdnqhe,bdnkhe->bdnhqk', qb, kb).astype(F32) + bias[None, None, None]
    s = jnp.where(mask[None, None, :, None], s, NEG_BIG)
    lse = jax.nn.logsumexp(s, axis=-1)
    p = jnp.exp(s - lse[..., None]).astype(v.dtype)
    o = jnp.einsum('bdnhqk,bdnkhe->bdnqhe', p, vb)
    o = o.reshape(b, dil, Lp, h, e)[:, :, :Ls].transpose(0, 2, 1, 3, 4).reshape(b, S, h, e)
    lse = lse.transpose(0, 1, 2, 4, 3).reshape(b, dil, Lp, h)[:, :, :Ls]
    lse = lse.transpose(0, 2, 1, 3).reshape(b, S, h)
    return o, lse


def dilated_mixer(h, rel_bias, w_in, w_out):
    b, S, _ = h.shape
    u = h @ w_in
    outs, lses = [], []
    for gi, (window, dil) in enumerate(DL_PAIRS):
        base = gi * 3 * DL_W
        q, k, v = [u[..., base + c * DL_W: base + (c + 1) * DL_W].reshape(b, S, DL_HEADS, DL_HD)
                   for c in range(3)]
        o, l = dilated_group(q * (DL_HD ** -0.5), k, v, dil, (window // 2) // dil, rel_bias)
        outs.append(o)
        lses.append(l)
    w = jax.nn.softmax(jnp.stack(lses), axis=0)
    o = jnp.sum(w[..., None] * jnp.stack(outs).astype(F32), axis=0)
    o = o.astype(h.dtype).reshape(b, S, DL_W)
    gate = u[..., 3 * len(DL_PAIRS) * DL_W:]
    return (o * jax.nn.silu(gate)) @ w_out


def _dense(key, shape, fan_in):
    return jax.random.normal(key, shape, F32) * (fan_in ** -0.5)


def _gain(key, shape):
    return 1.0 + 0.05 * jax.random.normal(key, shape, F32)


def setup_inputs(seed: int = 0) -> dict:
    key = jax.random.key(seed)
    k = jax.random.split(key, 24)
    nA, nB, nC, nD = N_SSD_LAYERS, N_HG_LAYERS, N_AT_LAYERS, N_DL_LAYERS
    dt0 = jnp.exp(jax.random.uniform(k[8], (nA, 2, SSD_HEADS), F32,
                                     minval=math.log(1e-3), maxval=math.log(1e-1)))
    return {
        "x": jax.random.normal(k[0], (BATCH, SEQ, D_MODEL), F32),
        "norm_g": _gain(k[1], (DEPTH, D_MODEL)),
        "final_g": _gain(k[2], (D_MODEL,)),
        "rel_bias": 0.5 * jax.random.normal(k[3], (REL_BUCKETS, DL_HEADS), F32),
        "hgrn_lb": 0.5 * jax.random.normal(k[4], (DEPTH, HG_W), F32),
        "ssd_w_in": _dense(k[5], (nA, D_MODEL, SSD_IN), D_MODEL),
        "ssd_conv_w": _dense(k[6], (nA, SSD_CONV, SSD_CONV_CH), SSD_CONV),
        "ssd_conv_b": 0.02 * jax.random.normal(k[7], (nA, SSD_CONV_CH), F32),
        "ssd_dt_bias": dt0 + jnp.log(-jnp.expm1(-dt0)),
        "ssd_a_log": jnp.log(jax.random.uniform(k[9], (nA, 2, SSD_HEADS), F32, minval=1.0, maxval=16.0)),
        "ssd_d": 1.0 + 0.1 * jax.random.normal(k[10], (nA, SSD_HEADS), F32),
        "ssd_norm_g": _gain(k[11], (nA, SSD_DI)),
        "ssd_w_out": _dense(k[12], (nA, SSD_DI, D_MODEL), SSD_DI),
        "hg_w_in": _dense(k[13], (nB, D_MODEL, HG_IN), D_MODEL),
        "hg_norm_g": _gain(k[14], (nB, HG_W)),
        "hg_w_out": _dense(k[15], (nB, HG_W, D_MODEL), HG_W),
        "at_w_in": _dense(k[16], (nC, D_MODEL, AT_IN), D_MODEL),
        "at_q_norm_g": _gain(k[17], (nC, AT_HD)),
        "at_k_norm_g": _gain(k[18], (nC, AT_HD)),
        "at_w_out": _dense(k[19], (nC, AT_QW, D_MODEL), AT_QW),
        "dl_w_in": _dense(k[20], (nD, D_MODEL, DL_IN), D_MODEL),
        "dl_w_out": _dense(k[21], (nD, DL_W, D_MODEL), DL_W),
    }


def reference(x, norm_g, final_g, rel_bias, hgrn_lb,
              ssd_w_in, ssd_conv_w, ssd_conv_b, ssd_dt_bias, ssd_a_log, ssd_d, ssd_norm_g, ssd_w_out,
              hg_w_in, hg_norm_g, hg_w_out,
              at_w_in, at_q_norm_g, at_k_norm_g, at_w_out,
              dl_w_in, dl_w_out):
    lb_sm = jax.nn.softmax(hgrn_lb.astype(F32), axis=0)
    lb_all = jnp.cumsum(lb_sm, axis=0) - lb_sm[0:1]
    for layer in range(DEPTH):
        kind = layer % N_MIXERS
        slot = layer // N_MIXERS
        hn = rms_norm(x, norm_g[layer])
        if kind == 0:
            y = ssd_mixer(hn, ssd_w_in[slot], ssd_conv_w[slot], ssd_conv_b[slot], ssd_dt_bias[slot],
                          ssd_a_log[slot], ssd_d[slot], ssd_norm_g[slot], ssd_w_out[slot])
        elif kind == 1:
            y = hgrn2_mixer(hn, lb_all[layer], hg_w_in[slot], hg_norm_g[slot], hg_w_out[slot])
        elif kind == 2:
            y = gqa_mixer(hn, at_w_in[slot], at_q_norm_g[slot], at_k_norm_g[slot], at_w_out[slot])
        else:
            y = dilated_mixer(hn, rel_bias, dl_w_in[slot], dl_w_out[slot])
        x = x + y.astype(x.dtype)
    return rms_norm(x, final_g)
```

```python
import functools
import math

import jax
import jax.numpy as jnp
import numpy as np
from jax import lax
from jax.experimental import pallas as pl
from jax.experimental.pallas import tpu as pltpu

F32 = jnp.float32
BF16 = jnp.bfloat16

EPS = 1e-6
NEG_BIG = -1e30
GRID_W = 64
ROPE_THETA = 10000.0

SSD_HEADDIM = 64
SSD_HEADS = 32
SSD_GROUPS = 4
SSD_STATE = 128
SSD_CONV = 7
HG_HEADS = 8
HG_SUB = 32
AT_HEADS = 16
AT_KV = 8
AT_HD = 128
DL_PAIRS = ((128, 1), (512, 4), (2048, 16))
DL_HEADS = 16
DL_HD = 64
REL_BUCKETS = 32
REL_MAX_DIST = 1024

LANES = 128
SUBLANES = 8
CHUNK = 128
VMEM_LIMIT = 56 * 1024 * 1024


def _params(*sem):
    return pltpu.CompilerParams(dimension_semantics=sem, vmem_limit_bytes=VMEM_LIMIT)


def _silu(x):
    return x * (1.0 / (1.0 + jnp.exp(-x)))


def _softplus(x):
    return jnp.maximum(x, 0.0) + jnp.log(1.0 + jnp.exp(-jnp.abs(x)))


def _dot(a, b):
    return jnp.dot(a, b, preferred_element_type=F32)


def _dot_nt(a, b):
    return lax.dot_general(a, b, (((1,), (1,)), ((), ())), preferred_element_type=F32)


def _dot_tn(a, b):
    return lax.dot_general(a, b, (((0,), (0,)), ((), ())), preferred_element_type=F32)


def _dot_exact(a, b):
    return jnp.dot(a, b, preferred_element_type=F32, precision=lax.Precision.HIGHEST)


def _proj_in_kernel(x_ref, g_ref, w_ref, o_ref, xn_ref):
    @pl.when(pl.program_id(1) == 0)
    def _():
        x = x_ref[...]
        ms = jnp.mean(x * x, axis=-1, keepdims=True)
        xn_ref[...] = (x * lax.rsqrt(ms + EPS) * g_ref[...]).astype(BF16)

    o_ref[...] = _dot(xn_ref[...], w_ref[...]).astype(o_ref.dtype)


def proj_in(x, g, w, *, tm=1024, tn=1024, out_dtype=F32):
    n, d = x.shape
    dout = w.shape[1]
    tm = min(tm, n)
    tn = min(tn, dout)
    assert n % tm == 0 and dout % tn == 0
    return pl.pallas_call(
        _proj_in_kernel,
        grid=(n // tm, dout // tn),
        in_specs=[pl.BlockSpec((tm, d), lambda i, j: (i, 0)),
                  pl.BlockSpec((1, d), lambda i, j: (0, 0)),
                  pl.BlockSpec((d, tn), lambda i, j: (0, j))],
        out_specs=pl.BlockSpec((tm, tn), lambda i, j: (i, j)),
        out_shape=jax.ShapeDtypeStruct((n, dout), out_dtype),
        scratch_shapes=[pltpu.VMEM((tm, d), BF16)],
        compiler_params=_params("parallel", "arbitrary"),
        name="proj_in",
    )(x, g.reshape(1, d), w)


def _proj_out_kernel(a_ref, w_ref, r_ref, g_ref, o_ref, *, final):
    y = r_ref[...] + _dot(a_ref[...], w_ref[...])
    if final:
        ms = jnp.mean(y * y, axis=-1, keepdims=True)
        y = y * lax.rsqrt(ms + EPS) * g_ref[...]
    o_ref[...] = y


def proj_out(a, w, res, final_g=None, *, tm=512):
    n, k = a.shape
    d = w.shape[1]
    tm = min(tm, n)
    assert n % tm == 0
    g = jnp.ones((1, d), F32) if final_g is None else final_g.reshape(1, d)
    return pl.pallas_call(
        functools.partial(_proj_out_kernel, final=final_g is not None),
        grid=(n // tm,),
        in_specs=[pl.BlockSpec((tm, k), lambda i: (i, 0)),
                  pl.BlockSpec((k, d), lambda i: (0, 0)),
                  pl.BlockSpec((tm, d), lambda i: (i, 0)),
                  pl.BlockSpec((1, d), lambda i: (0, 0))],
        out_specs=pl.BlockSpec((tm, d), lambda i: (i, 0)),
        out_shape=jax.ShapeDtypeStruct((n, d), F32),
        compiler_params=_params("parallel"),
        name="proj_out",
    )(a, w, res, g)


def _ssd_conv_kernel(xp_ref, xc_ref, xn_ref, bp_ref, bc_ref, bn_ref, wx_ref, wb_ref, bx_ref, bb_ref,
                     ox_ref, ob_ref, *, nblk):
    i = pl.program_id(1)
    pad = SSD_CONV // 2

    def conv(prev_ref, cur_ref, next_ref, w_ref, b_ref, o_ref):
        rows = cur_ref.shape[0]
        prev = jnp.where(i > 0, prev_ref[...], 0.0)
        nxt = jnp.where(i < nblk - 1, next_ref[...], 0.0)
        ext = jnp.concatenate([prev, cur_ref[...], nxt], axis=0)
        acc = jnp.zeros(cur_ref.shape, F32) + b_ref[...]
        for t in range(SSD_CONV):
            off = SUBLANES - pad + t
            acc = acc + ext[off:off + rows, :] * w_ref[t:t + 1, :]
        o_ref[...] = _silu(acc)

    conv(xp_ref, xc_ref, xn_ref, wx_ref, bx_ref, ox_ref)
    conv(bp_ref, bc_ref, bn_ref, wb_ref, bb_ref, ob_ref)


def ssd_conv(u, conv_w, conv_b, batch, seq, *, tc=256):
    n = u.shape[0]
    di = SSD_HEADS * SSD_HEADDIM
    gn2 = 2 * SSD_GROUPS * SSD_STATE
    tc = min(tc, seq)
    nblk = seq // tc
    r8 = tc // SUBLANES
    last8 = n // SUBLANES - 1

    def cur(wblk):
        return lambda b, i: (b * nblk + i, wblk)

    def prev(wblk):
        return lambda b, i: (jnp.maximum((b * nblk + i) * r8 - 1, 0), wblk)

    def nxt(wblk):
        return lambda b, i: (jnp.minimum((b * nblk + i + 1) * r8, last8), wblk)

    wx, wb = conv_w[:, :di], conv_w[:, di:]
    bx, bb = conv_b[:di].reshape(1, di), conv_b[di:].reshape(1, gn2)
    const = lambda b, i: (0, 0)
    return pl.pallas_call(
        functools.partial(_ssd_conv_kernel, nblk=nblk),
        grid=(batch, nblk),
        in_specs=[pl.BlockSpec((SUBLANES, di), prev(1)), pl.BlockSpec((tc, di), cur(1)),
                  pl.BlockSpec((SUBLANES, di), nxt(1)),
                  pl.BlockSpec((SUBLANES, gn2), prev(4)), pl.BlockSpec((tc, gn2), cur(4)),
                  pl.BlockSpec((SUBLANES, gn2), nxt(4)),
                  pl.BlockSpec((SSD_CONV, di), const), pl.BlockSpec((SSD_CONV, gn2), const),
                  pl.BlockSpec((1, di), const), pl.BlockSpec((1, gn2), const)],
        out_specs=[pl.BlockSpec((tc, di), lambda b, i: (b * nblk + i, 0)),
                   pl.BlockSpec((tc, gn2), lambda b, i: (b * nblk + i, 0))],
        out_shape=[jax.ShapeDtypeStruct((n, di), F32), jax.ShapeDtypeStruct((n, gn2), F32)],
        compiler_params=_params("parallel", "parallel"),
        name="ssd_conv",
    )(u, u, u, u, u, u, wx, wb, bx, bb)


def _ssd_scan_kernel(x_ref, bc_ref, dtr_ref, dtb_ref, alog_ref, *rest, reverse, final):
    if final:
        z_ref, yo_ref, dskip_ref, ng_ref, o_ref, st_ref = rest
    else:
        o_ref, st_ref = rest
    c = pl.program_id(1)
    q = CHUNK
    gn = SSD_GROUPS * SSD_STATE
    hpg = SSD_HEADS // SSD_GROUPS
    gw = hpg * SSD_HEADDIM
    hoff = SSD_HEADS if reverse else 0

    @pl.when(c == 0)
    def _():
        st_ref[...] = jnp.zeros(st_ref.shape, F32)

    x = x_ref[...]
    bm = bc_ref[:, :gn].astype(BF16)
    cm = bc_ref[:, gn:].astype(BF16)
    dt = _softplus(dtr_ref[...] + dtb_ref[...])
    da = dt * (-jnp.exp(alog_ref[...]))
    row = lax.broadcasted_iota(jnp.int32, (q, q), 0)
    col = lax.broadcasted_iota(jnp.int32, (q, q), 1)
    valid = (col >= row) if reverse else (col <= row)
    tri = valid.astype(F32)
    cs_col = _dot_exact(tri, da)
    cs_row = cs_col.T
    dt_row = dt.T
    tot_row = cs_row[:, 0:1] if reverse else cs_row[:, q - 1:q]
    tot_col = cs_col[0:1, :] if reverse else cs_col[q - 1:q, :]
    e_col = jnp.exp(cs_col)
    w_col = jnp.exp(tot_col - cs_col) * dt
    lane = lax.broadcasted_iota(jnp.int32, (q, LANES), 1)
    lo = lane < SSD_HEADDIM

    def expand(v, h0):
        return jnp.where(lo, v[:, h0:h0 + 1], v[:, h0 + 1:h0 + 2])

    for g in range(SSD_GROUPS):
        b_g = bm[:, g * SSD_STATE:(g + 1) * SSD_STATE]
        c_g = cm[:, g * SSD_STATE:(g + 1) * SSD_STATE]
        cb = _dot_nt(c_g, b_g)
        s_old = st_ref[g]
        y_off = _dot_nt(c_g, s_old.astype(BF16))
        xw_parts, y_parts = [], []
        for pp in range(hpg // 2):
            h0 = g * hpg + 2 * pp
            cols = slice(h0 * SSD_HEADDIM, (h0 + 2) * SSD_HEADDIM)
            x2 = x[:, cols]
            x2b = x2.astype(BF16)
            ys = []
            for h in (h0, h0 + 1):
                hl = hoff + h
                diff = cs_col[:, hl:hl + 1] - cs_row[hl:hl + 1, :]
                lmat = jnp.exp(jnp.where(valid, diff, NEG_BIG))
                m = (cb * lmat * dt_row[hl:hl + 1, :]).astype(BF16)
                ys.append(_dot(m, x2b))
            y2 = jnp.where(lo, ys[0], ys[1])
            off2 = y_off[:, 2 * pp * SSD_HEADDIM:(2 * pp + 2) * SSD_HEADDIM]
            y2 = y2 + off2 * expand(e_col, hoff + h0)
            xw_parts.append((x2 * expand(w_col, hoff + h0)).astype(BF16))
            if final:
                y2 = y2 + yo_ref[:, cols] + x2 * dskip_ref[:, cols]
                y_parts.append(y2 * _silu(z_ref[:, cols]))
            else:
                o_ref[:, cols] = y2
        xw = jnp.concatenate(xw_parts, axis=1)
        states = _dot_tn(xw, b_g)
        for j in range(hpg):
            hl = hoff + g * hpg + j
            rows = slice(j * SSD_HEADDIM, (j + 1) * SSD_HEADDIM)
            st_ref[g, rows, :] = s_old[rows, :] * jnp.exp(tot_row[hl:hl + 1, :]) + states[rows, :]
        if final:
            gcols = slice(g * gw, (g + 1) * gw)
            yg = jnp.concatenate(y_parts, axis=1)
            ms = jnp.mean(yg * yg, axis=-1, keepdims=True)
            o_ref[:, gcols] = (yg * lax.rsqrt(ms + EPS) * ng_ref[:, gcols]).astype(o_ref.dtype)


def ssd_scan(xc, bc, dtr, dt_bias, a_log, batch, seq, *, reverse, final_args=None):
    n, di = xc.shape
    nc = seq // CHUNK
    gn2 = bc.shape[1]
    final = final_args is not None

    def rows(b, c):
        return b * nc + (nc - 1 - c if reverse else c)

    blk = lambda w: pl.BlockSpec((CHUNK, w), lambda b, c: (rows(b, c), 0))
    const = lambda w: pl.BlockSpec((1, w), lambda b, c: (0, 0))
    in_specs = [blk(di), blk(gn2), blk(LANES), const(LANES), const(LANES)]
    args = [xc, bc, dtr, dt_bias, a_log]
    if final:
        u, y_other, dskip, ng = final_args
        in_specs += [blk(di), blk(di), const(di), const(di)]
        args += [u, y_other, dskip, ng]
    return pl.pallas_call(
        functools.partial(_ssd_scan_kernel, reverse=reverse, final=final),
        grid=(batch, nc),
        in_specs=in_specs,
        out_specs=blk(di),
        out_shape=jax.ShapeDtypeStruct((n, di), BF16 if final else F32),
        scratch_shapes=[pltpu.VMEM((SSD_GROUPS, (SSD_HEADS // SSD_GROUPS) * SSD_HEADDIM, SSD_STATE), F32)],
        compiler_params=_params("parallel", "arbitrary"),
        name="ssd_scan_rev" if reverse else "ssd_scan_fwd",
    )(*args)


def ssd_layer(x, g, w_in, conv_w, conv_b, dt_bias, a_log, d_skip, norm_g, w_out, batch, seq, final_g=None):
    di = SSD_HEADS * SSD_HEADDIM
    main = 2 * di + 2 * SSD_GROUPS * SSD_STATE
    w_main = w_in[:, :main].astype(BF16)
    w_dt = jnp.pad(w_in[:, main:], ((0, 0), (0, LANES - 2 * SSD_HEADS))).astype(BF16)
    u = proj_in(x, g, w_main)
    dtr = proj_in(x, g, w_dt)
    xc, bc = ssd_conv(u, conv_w, conv_b, batch, seq)
    pad = lambda v: jnp.pad(v.reshape(1, 2 * SSD_HEADS), ((0, 0), (0, LANES - 2 * SSD_HEADS)))
    dtb, alog = pad(dt_bias), pad(a_log)
    y_rev = ssd_scan(xc, bc, dtr, dtb, alog, batch, seq, reverse=True)
    dskip = jnp.repeat(d_skip, SSD_HEADDIM).reshape(1, di)
    y = ssd_scan(xc, bc, dtr, dtb, alog, batch, seq, reverse=False,
                 final_args=(u, y_rev, dskip, norm_g.reshape(1, di)))
    return proj_out(y, w_out.astype(BF16), x, final_g)


def _hgrn_scan_kernel(q_ref, f_ref, v_ref, lb_ref, *rest, reverse, final):
    if final:
        gate_ref, oo_ref, ng_ref, o_ref, st_ref = rest
    else:
        o_ref, st_ref = rest
    c = pl.program_id(1)
    n = CHUNK
    nsub = n // HG_SUB
    dk = LANES

    @pl.when(c == 0)
    def _():
        st_ref[...] = jnp.zeros(st_ref.shape, F32)

    lb = lb_ref[...]
    qa = _silu(q_ref[...])
    f = lb + (1.0 - lb) * (1.0 / (1.0 + jnp.exp(-f_ref[...])))
    ka = 1.0 - f
    lg = jnp.log(f)
    row = lax.broadcasted_iota(jnp.int32, (n, n), 0)
    col = lax.broadcasted_iota(jnp.int32, (n, n), 1)
    valid = (col >= row) if reverse else (col <= row)
    gsum = _dot_exact(valid.astype(F32), lg)
    ref_row = HG_SUB // 2 - 1 if reverse else HG_SUB // 2
    far = 0 if reverse else n - 1

    for h in range(HG_HEADS):
        cols = slice(h * dk, (h + 1) * dk)
        gh, qh, kh = gsum[:, cols], qa[:, cols], ka[:, cols]
        vh = v_ref[:, cols].astype(BF16)
        tot = gh[far:far + 1, :]
        qp, kn, anchors = [], [], []
        for s in range(nsub):
            rs = slice(s * HG_SUB, (s + 1) * HG_SUB)
            a = gh[s * HG_SUB + ref_row:s * HG_SUB + ref_row + 1, :]
            anchors.append(a)
            qp.append(qh[rs] * jnp.exp(gh[rs] - a))
            kn.append(kh[rs] * jnp.exp(a - gh[rs]))
        att_cols = []
        for j in range(nsub):
            lhs = []
            for i in range(nsub):
                live = (i <= j) if reverse else (i >= j)
                if not live:
                    lhs.append(jnp.zeros((HG_SUB, dk), F32))
                elif i == j:
                    lhs.append(qp[i])
                else:
                    lhs.append(qp[i] * jnp.exp(anchors[i] - anchors[j]))
            att_cols.append(_dot_nt(jnp.concatenate(lhs, axis=0).astype(BF16), kn[j].astype(BF16)))
        att = jnp.where(valid, jnp.concatenate(att_cols, axis=1), 0.0)
        s_old = st_ref[h]
        qe = jnp.concatenate([qp[s] * jnp.exp(anchors[s]) for s in range(nsub)], axis=0)
        kd = jnp.concatenate([kn[s] * jnp.exp(tot - anchors[s]) for s in range(nsub)], axis=0)
        o = _dot(att.astype(BF16), vh) + _dot_nt(qe.astype(BF16), s_old.astype(BF16))
        st_ref[h] = s_old * jnp.exp(tot) + _dot_tn(vh, kd.astype(BF16))
        if final:
            o = o + oo_ref[:, cols]
            ms = jnp.mean(o * o, axis=-1, keepdims=True)
            o = o * lax.rsqrt(ms + EPS) * ng_ref[:, cols] * _silu(gate_ref[:, cols])
        o_ref[:, cols] = o.astype(o_ref.dtype)


def hgrn_scan(u, lb, batch, seq, *, reverse, final_args=None):
    n = u.shape[0]
    w = HG_HEADS * LANES
    nc = seq // CHUNK
    final = final_args is not None

    def rows(b, c):
        return b * nc + (nc - 1 - c if reverse else c)

    ublk = lambda j: pl.BlockSpec((CHUNK, w), lambda b, c: (rows(b, c), j))
    const = pl.BlockSpec((1, w), lambda b, c: (0, 0))
    in_specs = [ublk(0), ublk(2 if reverse else 1), ublk(3), const]
    args = [u, u, u, lb.reshape(1, w)]
    if final:
        o_other, ng = final_args
        in_specs += [ublk(4), ublk(0), const]
        args += [u, o_other, ng.reshape(1, w)]
    return pl.pallas_call(
        functools.partial(_hgrn_scan_kernel, reverse=reverse, final=final),
        grid=(batch, nc),
        in_specs=in_specs,
        out_specs=ublk(0),
        out_shape=jax.ShapeDtypeStruct((n, w), BF16 if final else F32),
        scratch_shapes=[pltpu.VMEM((HG_HEADS, LANES, LANES), F32)],
        compiler_params=_params("parallel", "arbitrary"),
        name="hgrn_scan_rev" if reverse else "hgrn_scan_fwd",
    )(*args)


def hgrn_layer(x, g, lb, w_in, norm_g, w_out, batch, seq, final_g=None):
    u = proj_in(x, g, w_in.astype(BF16))
    o_rev = hgrn_scan(u, lb, batch, seq, reverse=True)
    o = hgrn_scan(u, lb, batch, seq, reverse=False, final_args=(o_rev, norm_g))
    return proj_out(o, w_out.astype(BF16), x, final_g)


def _rope_tables(seq):
    pos = np.arange(seq)
    rowp = (pos // GRID_W).astype(np.float64)
    colp = (pos % GRID_W).astype(np.float64)
    half = AT_HD // 4
    inv = ROPE_THETA ** (-np.arange(0, 2 * half, 2, dtype=np.float64) / (2 * half))
    ar, ac = rowp[:, None] * inv, colp[:, None] * inv
    cos = np.concatenate([np.cos(ar), np.cos(ar), np.cos(ac), np.cos(ac)], axis=1)
    sin = np.concatenate([-np.sin(ar), np.sin(ar), -np.sin(ac), np.sin(ac)], axis=1)
    return jnp.asarray(cos, F32), jnp.asarray(sin, F32)


def _gqa_prep_kernel(q_ref, k_ref, v_ref, cos_ref, sin_ref, qg_ref, kg_ref, qo_ref, ko_ref, vo_ref):
    cos, sin = cos_ref[...], sin_ref[...]
    lane = lax.broadcasted_iota(jnp.int32, cos.shape, 1)
    first = (lane % (AT_HD // 2)) < (AT_HD // 4)

    def norm_rope(xh, gain, scale):
        ms = jnp.mean(xh * xh, axis=-1, keepdims=True)
        xn = xh * lax.rsqrt(ms + EPS) * gain
        swapped = jnp.where(first, pltpu.roll(xn, AT_HD - AT_HD // 4, axis=1), pltpu.roll(xn, AT_HD // 4, axis=1))
        return (xn * cos + swapped * sin) * scale

    for h in range(AT_HEADS):
        cols = slice(h * AT_HD, (h + 1) * AT_HD)
        qo_ref[:, cols] = norm_rope(q_ref[:, cols], qg_ref[...], AT_HD ** -0.5).astype(BF16)
    for h in range(AT_KV):
        cols = slice(h * AT_HD, (h + 1) * AT_HD)
        ko_ref[:, cols] = norm_rope(k_ref[:, cols], kg_ref[...], 1.0).astype(BF16)
    vo_ref[...] = v_ref[...].astype(BF16)


def gqa_prep(u, q_g, k_g, seq, *, tm=256):
    n = u.shape[0]
    qw, kw = AT_HEADS * AT_HD, AT_KV * AT_HD
    nb = seq // tm
    cos, sin = _rope_tables(seq)
    row = lambda j: (lambda i: (i, j))
    pos = lambda i: (i % nb, 0)
    const = lambda i: (0, 0)
    return pl.pallas_call(
        _gqa_prep_kernel,
        grid=(n // tm,),
        in_specs=[pl.BlockSpec((tm, qw), row(0)), pl.BlockSpec((tm, kw), row(2)), pl.BlockSpec((tm, kw), row(3)),
                  pl.BlockSpec((tm, AT_HD), pos), pl.BlockSpec((tm, AT_HD), pos),
                  pl.BlockSpec((1, AT_HD), const), pl.BlockSpec((1, AT_HD), const)],
        out_specs=[pl.BlockSpec((tm, qw), row(0)), pl.BlockSpec((tm, kw), row(0)), pl.BlockSpec((tm, kw), row(0))],
        out_shape=[jax.ShapeDtypeStruct((n, qw), BF16), jax.ShapeDtypeStruct((n, kw), BF16),
                   jax.ShapeDtypeStruct((n, kw), BF16)],
        compiler_params=_params("parallel"),
        name="gqa_prep",
    )(u, u, u, cos, sin, q_g.reshape(1, AT_HD), k_g.reshape(1, AT_HD))


def _gqa_flash_kernel(q_ref, k_ref, v_ref, gate_ref, o_ref, *, tk):
    tq = q_ref.shape[0]
    seq = k_ref.shape[0]
    grp = AT_HEADS // AT_KV
    qs = jnp.concatenate([q_ref[:, j * AT_HD:(j + 1) * AT_HD] for j in range(grp)], axis=0)

    def body(t, carry):
        m, l, acc = carry
        ks = k_ref[pl.ds(t * tk, tk), :]
        vs = v_ref[pl.ds(t * tk, tk), :]
        s = _dot_nt(qs, ks)
        m_new = jnp.maximum(m, jnp.max(s, axis=-1, keepdims=True))
        alpha = jnp.exp(m - m_new)
        p = jnp.exp(s - m_new)
        l = alpha * l + jnp.sum(p, axis=-1, keepdims=True)
        acc = alpha * acc + _dot(p.astype(BF16), vs)
        return m_new, l, acc

    init = (jnp.full((grp * tq, 1), -jnp.inf, F32), jnp.zeros((grp * tq, 1), F32),
            jnp.zeros((grp * tq, AT_HD), F32))
    m, l, acc = lax.fori_loop(0, seq // tk, body, init)
    o = acc / l
    for j in range(grp):
        cols = slice(j * AT_HD, (j + 1) * AT_HD)
        o_ref[:, cols] = (o[j * tq:(j + 1) * tq, :] * _silu(gate_ref[:, cols])).astype(o_ref.dtype)


def gqa_flash(q, k, v, u, batch, seq, *, tq=256, tk=512):
    n = q.shape[0]
    grp = AT_HEADS // AT_KV
    gw = grp * AT_HD
    nq = seq // tq
    gate_blk0 = (AT_HEADS + 2 * AT_KV) * AT_HD // gw
    return pl.pallas_call(
        functools.partial(_gqa_flash_kernel, tk=min(tk, seq)),
        grid=(batch, AT_KV, nq),
        in_specs=[pl.BlockSpec((tq, gw), lambda b, h, i: (b * nq + i, h)),
                  pl.BlockSpec((seq, AT_HD), lambda b, h, i: (b, h)),
                  pl.BlockSpec((seq, AT_HD), lambda b, h, i: (b, h)),
                  pl.BlockSpec((tq, gw), lambda b, h, i: (b * nq + i, gate_blk0 + h))],
        out_specs=pl.BlockSpec((tq, gw), lambda b, h, i: (b * nq + i, h)),
        out_shape=jax.ShapeDtypeStruct((n, AT_HEADS * AT_HD), BF16),
        compiler_params=_params("parallel", "parallel", "arbitrary"),
        name="gqa_flash",
    )(q, k, v, u)


def gqa_layer(x, g, w_in, q_g, k_g, w_out, batch, seq, final_g=None):
    u = proj_in(x, g, w_in.astype(BF16))
    q, k, v = gqa_prep(u, q_g, k_g, seq)
    o = gqa_flash(q, k, v, u, batch, seq)
    return proj_out(o, w_out.astype(BF16), x, final_g)


def _t5_bucket_np(rel):
    half = REL_BUCKETS // 2
    exact = half // 2
    nabs = np.abs(rel)
    large = exact + (np.log(np.maximum(nabs, 1).astype(np.float32) / exact)
                     / math.log(REL_MAX_DIST / exact) * (half - exact)).astype(np.int32)
    large = np.minimum(large, half - 1)
    return np.where(rel > 0, half, 0) + np.where(nabs < exact, nabs, large)


def _dilated_kernel(q_ref, kp_ref, kc_ref, kn_ref, vp_ref, vc_ref, vn_ref, bias_ref, o_ref, lse_ref, *, steps, ls):
    i = pl.program_id(2)
    tq = q_ref.shape[0]
    nk = tq + 2 * steps
    qi = lax.broadcasted_iota(jnp.int32, (tq, nk), 0)
    kj = lax.broadcasted_iota(jnp.int32, (tq, nk), 1)
    dm = kj - steps - qi
    mk = i * tq - steps + kj
    mask = (jnp.abs(dm) <= steps) & (mk >= 0) & (mk < ls)
    lane = lax.broadcasted_iota(jnp.int32, (tq, LANES), 1)
    lo = lane < DL_HD
    kcat = jnp.concatenate([kp_ref[...], kc_ref[...], kn_ref[...]], axis=0).astype(BF16)
    vcat = jnp.concatenate([vp_ref[...], vc_ref[...], vn_ref[...]], axis=0).astype(BF16)
    scale = DL_HD ** -0.5
    for p in range(DL_HEADS // 2):
        cols = slice(p * LANES, (p + 1) * LANES)
        q2 = q_ref[:, cols] * scale
        k2, v2 = kcat[:, cols], vcat[:, cols]
        outs, lses = [], []
        for half in range(2):
            h = 2 * p + half
            qh = jnp.where(lo if half == 0 else ~lo, q2, 0.0).astype(BF16)
            s = _dot_nt(qh, k2) + bias_ref[h]
            s = jnp.where(mask, s, NEG_BIG)
            m = jnp.max(s, axis=-1, keepdims=True)
            pexp = jnp.exp(s - m)
            l = jnp.sum(pexp, axis=-1, keepdims=True)
            outs.append(_dot(pexp.astype(BF16), v2) / l)
            lses.append(m + jnp.log(l))
        o_ref[:, cols] = jnp.where(lo, outs[0], outs[1])
        lse_ref[:, cols] = jnp.where(lo, lses[0], lses[1])


def dilated_group(u, rel_bias, gi, window, dil, batch, seq, *, tq=128):
    n, win = u.shape
    hw = DL_HEADS * DL_HD
    ls = seq // dil
    steps = (window // 2) // dil
    tq = min(tq, ls)
    nq = ls // tq
    hb = tq // steps
    nblk_k = ls // steps
    u3 = u.reshape(batch, ls, dil * win)
    cpb = win // hw
    qi = np.arange(tq)[:, None]
    kj = np.arange(tq + 2 * steps)[None, :]
    bucket = _t5_bucket_np((kj - steps - qi) * dil)
    bias = jnp.transpose(rel_bias[bucket], (2, 0, 1)).astype(F32)

    def colblk(c):
        return lambda b, r, i: r * cpb + gi * 3 + c

    def cur(c):
        cb = colblk(c)
        return lambda b, r, i: (b, i, cb(b, r, i))

    def prev(c):
        cb = colblk(c)
        return lambda b, r, i: (b, jnp.maximum(i * hb - 1, 0), cb(b, r, i))

    def nxt(c):
        cb = colblk(c)
        return lambda b, r, i: (b, jnp.minimum((i + 1) * hb, nblk_k - 1), cb(b, r, i))

    main = lambda f: pl.BlockSpec((None, tq, hw), f)
    halo = lambda f: pl.BlockSpec((None, steps, hw), f)
    out_spec = pl.BlockSpec((None, tq, hw), lambda b, r, i: (b, i, r))
    o, lse = pl.pallas_call(
        functools.partial(_dilated_kernel, steps=steps, ls=ls),
        grid=(batch, dil, nq),
        in_specs=[main(cur(0)), halo(prev(1)), main(cur(1)), halo(nxt(1)),
                  halo(prev(2)), main(cur(2)), halo(nxt(2)),
                  pl.BlockSpec((DL_HEADS, tq, tq + 2 * steps), lambda b, r, i: (0, 0, 0))],
        out_specs=[out_spec, out_spec],
        out_shape=[jax.ShapeDtypeStruct((batch, ls, dil * hw), F32)] * 2,
        compiler_params=_params("parallel", "parallel", "parallel"),
        name=f"dilated_g{gi}",
    )(u3, u3, u3, u3, u3, u3, u3, bias)
    return o.reshape(n, hw), lse.reshape(n, hw)


def _dilated_merge_kernel(o0, o1, o2, l0, l1, l2, gate_ref, out_ref):
    a, b, c = l0[...], l1[...], l2[...]
    m = jnp.maximum(jnp.maximum(a, b), c)
    ea, eb, ec = jnp.exp(a - m), jnp.exp(b - m), jnp.exp(c - m)
    o = (ea * o0[...] + eb * o1[...] + ec * o2[...]) / (ea + eb + ec)
    out_ref[...] = (o * _silu(gate_ref[...])).astype(out_ref.dtype)


def dilated_merge(outs, lses, u, *, tm=512):
    n, hw = outs[0].shape
    gate_blk = u.shape[1] // hw - 1
    blk = pl.BlockSpec((tm, hw), lambda i: (i, 0))
    return pl.pallas_call(
        _dilated_merge_kernel,
        grid=(n // tm,),
        in_specs=[blk] * 6 + [pl.BlockSpec((tm, hw), lambda i: (i, gate_blk))],
        out_specs=blk,
        out_shape=jax.ShapeDtypeStruct((n, hw), BF16),
        compiler_params=_params("parallel"),
        name="dilated_merge",
    )(*outs, *lses, u)


def dilated_layer(x, g, rel_bias, w_in, w_out, batch, seq, final_g=None):
    u = proj_in(x, g, w_in.astype(BF16))
    outs, lses = [], []
    for gi, (window, dil) in enumerate(DL_PAIRS):
        o, l = dilated_group(u, rel_bias, gi, window, dil, batch, seq)
        outs.append(o)
        lses.append(l)
    o = dilated_merge(outs, lses, u)
    return proj_out(o, w_out.astype(BF16), x, final_g)


def kernel(x, norm_g, final_g, rel_bias, hgrn_lb, ssd_w_in, ssd_conv_w, ssd_conv_b, ssd_dt_bias, ssd_a_log, ssd_d,
           ssd_norm_g, ssd_w_out, hg_w_in, hg_norm_g, hg_w_out, at_w_in, at_q_norm_g, at_k_norm_g, at_w_out,
           dl_w_in, dl_w_out):
    batch, seq, d = x.shape
    depth = norm_g.shape[0]
    n_mixers = 4
    lb_sm = jax.nn.softmax(hgrn_lb.astype(F32), axis=0)
    lb_all = jnp.cumsum(lb_sm, axis=0) - lb_sm[0:1]
    h = x.reshape(batch * seq, d)
    for layer in range(depth):
        kind, slot = layer % n_mixers, layer // n_mixers
        fg = final_g if layer == depth - 1 else None
        if kind == 0:
            h = ssd_layer(h, norm_g[layer], ssd_w_in[slot], ssd_conv_w[slot], ssd_conv_b[slot], ssd_dt_bias[slot],
                          ssd_a_log[slot], ssd_d[slot], ssd_norm_g[slot], ssd_w_out[slot], batch, seq, fg)
        elif kind == 1:
            h = hgrn_layer(h, norm_g[layer], lb_all[layer], hg_w_in[slot], hg_norm_g[slot], hg_w_out[slot],
                           batch, seq, fg)
        elif kind == 2:
            h = gqa_layer(h, norm_g[layer], at_w_in[slot], at_q_norm_g[slot], at_k_norm_g[slot], at_w_out[slot],
                          batch, seq, fg)
        else:
            h = dilated_layer(h, norm_g[layer], rel_bias, dl_w_in[slot], dl_w_out[slot], batch, seq, fg)
    return h.reshape(batch, seq, d)
```

```python
import functools
import math

import jax
import jax.numpy as jnp
import numpy as np
from jax import lax
from jax.experimental import pallas as pl
from jax.experimental.pallas import tpu as pltpu

F32 = jnp.float32
BF16 = jnp.bfloat16

EPS = 1e-6
NEG_BIG = -1e30
GRID_W = 64
ROPE_THETA = 10000.0

SSD_HEADDIM = 64
SSD_HEADS = 32
SSD_GROUPS = 4
SSD_STATE = 128
SSD_CONV = 7
HG_HEADS = 8
HG_SUB = 32
AT_HEADS = 16
AT_KV = 8
AT_HD = 128
DL_PAIRS = ((128, 1), (512, 4), (2048, 16))
DL_HEADS = 16
DL_HD = 64
REL_BUCKETS = 32
REL_MAX_DIST = 1024

LANES = 128
SUBLANES = 8
CHUNK = 128
VMEM_LIMIT = 56 * 1024 * 1024


def _params(*sem):
    return pltpu.CompilerParams(dimension_semantics=sem, vmem_limit_bytes=VMEM_LIMIT)


def _silu(x):
    return x * (1.0 / (1.0 + jnp.exp(-x)))


def _softplus(x):
    return jnp.maximum(x, 0.0) + jnp.log(1.0 + jnp.exp(-jnp.abs(x)))


def _dot(a, b):
    return jnp.dot(a, b, preferred_element_type=F32)


def _dot_nt(a, b):
    return lax.dot_general(a, b, (((1,), (1,)), ((), ())), preferred_element_type=F32)


def _dot_tn(a, b):
    return lax.dot_general(a, b, (((0,), (0,)), ((), ())), preferred_element_type=F32)


def _dot_exact(a, b):
    return jnp.dot(a, b, preferred_element_type=F32, precision=lax.Precision.HIGHEST)


def _proj_in_kernel(x_ref, g_ref, w_ref, o_ref, xn_ref):
    @pl.when(pl.program_id(1) == 0)
    def _():
        x = x_ref[...]
        ms = jnp.mean(x * x, axis=-1, keepdims=True)
        xn_ref[...] = (x * lax.rsqrt(ms + EPS) * g_ref[...]).astype(BF16)

    o_ref[...] = _dot(xn_ref[...], w_ref[...]).astype(o_ref.dtype)


def proj_in(x, g, w, *, tm=1024, tn=1024, out_dtype=F32):
    n, d = x.shape
    dout = w.shape[1]
    tm = min(tm, n)
    tn = min(tn, dout)
    assert n % tm == 0 and dout % tn == 0
    return pl.pallas_call(
        _proj_in_kernel,
        grid=(n // tm, dout // tn),
        in_specs=[pl.BlockSpec((tm, d), lambda i, j: (i, 0)),
                  pl.BlockSpec((1, d), lambda i, j: (0, 0)),
                  pl.BlockSpec((d, tn), lambda i, j: (0, j))],
        out_specs=pl.BlockSpec((tm, tn), lambda i, j: (i, j)),
        out_shape=jax.ShapeDtypeStruct((n, dout), out_dtype),
        scratch_shapes=[pltpu.VMEM((tm, d), BF16)],
        compiler_params=_params("parallel", "arbitrary"),
        name="proj_in",
    )(x, g.reshape(1, d), w)


def _proj_out_kernel(a_ref, w_ref, r_ref, g_ref, o_ref, *, final):
    y = r_ref[...] + _dot(a_ref[...], w_ref[...])
    if final:
        ms = jnp.mean(y * y, axis=-1, keepdims=True)
        y = y * lax.rsqrt(ms + EPS) * g_ref[...]
    o_ref[...] = y


def proj_out(a, w, res, final_g=None, *, tm=512):
    n, k = a.shape
    d = w.shape[1]
    tm = min(tm, n)
    assert n % tm == 0
    g = jnp.ones((1, d), F32) if final_g is None else final_g.reshape(1, d)
    return pl.pallas_call(
        functools.partial(_proj_out_kernel, final=final_g is not None),
        grid=(n // tm,),
        in_specs=[pl.BlockSpec((tm, k), lambda i: (i, 0)),
                  pl.BlockSpec((k, d), lambda i: (0, 0)),
                  pl.BlockSpec((tm, d), lambda i: (i, 0)),
                  pl.BlockSpec((1, d), lambda i: (0, 0))],
        out_specs=pl.BlockSpec((tm, d), lambda i: (i, 0)),
        out_shape=jax.ShapeDtypeStruct((n, d), F32),
        compiler_params=_params("parallel"),
        name="proj_out",
    )(a, w, res, g)


def _ssd_conv_kernel(xp_ref, xc_ref, xn_ref, bp_ref, bc_ref, bn_ref, wx_ref, wb_ref, bx_ref, bb_ref,
                     ox_ref, ob_ref, *, nblk):
    i = pl.program_id(1)
    pad = SSD_CONV // 2

    def conv(prev_ref, cur_ref, next_ref, w_ref, b_ref, o_ref):
        rows = cur_ref.shape[0]
        prev = jnp.where(i > 0, prev_ref[...], 0.0)
        nxt = jnp.where(i < nblk - 1, next_ref[...], 0.0)
        ext = jnp.concatenate([prev, cur_ref[...], nxt], axis=0)
        acc = jnp.zeros(cur_ref.shape, F32) + b_ref[...]
        for t in range(SSD_CONV):
            off = SUBLANES - pad + t
            acc = acc + ext[off:off + rows, :] * w_ref[t:t + 1, :]
        o_ref[...] = _silu(acc)

    conv(xp_ref, xc_ref, xn_ref, wx_ref, bx_ref, ox_ref)
    conv(bp_ref, bc_ref, bn_ref, wb_ref, bb_ref, ob_ref)


def ssd_conv(u, conv_w, conv_b, batch, seq, *, tc=256):
    n = u.shape[0]
    di = SSD_HEADS * SSD_HEADDIM
    gn2 = 2 * SSD_GROUPS * SSD_STATE
    tc = min(tc, seq)
    nblk = seq // tc
    r8 = tc // SUBLANES
    last8 = n // SUBLANES - 1

    def cur(wblk):
        return lambda b, i: (b * nblk + i, wblk)

    def prev(wblk):
        return lambda b, i: (jnp.maximum((b * nblk + i) * r8 - 1, 0), wblk)

    def nxt(wblk):
        return lambda b, i: (jnp.minimum((b * nblk + i + 1) * r8, last8), wblk)

    wx, wb = conv_w[:, :di], conv_w[:, di:]
    bx, bb = conv_b[:di].reshape(1, di), conv_b[di:].reshape(1, gn2)
    const = lambda b, i: (0, 0)
    return pl.pallas_call(
        functools.partial(_ssd_conv_kernel, nblk=nblk),
        grid=(batch, nblk),
        in_specs=[pl.BlockSpec((SUBLANES, di), prev(1)), pl.BlockSpec((tc, di), cur(1)),
                  pl.BlockSpec((SUBLANES, di), nxt(1)),
                  pl.BlockSpec((SUBLANES, gn2), prev(4)), pl.BlockSpec((tc, gn2), cur(4)),
                  pl.BlockSpec((SUBLANES, gn2), nxt(4)),
                  pl.BlockSpec((SSD_CONV, di), const), pl.BlockSpec((SSD_CONV, gn2), const),
                  pl.BlockSpec((1, di), const), pl.BlockSpec((1, gn2), const)],
        out_specs=[pl.BlockSpec((tc, di), lambda b, i: (b * nblk + i, 0)),
                   pl.BlockSpec((tc, gn2), lambda b, i: (b * nblk + i, 0))],
        out_shape=[jax.ShapeDtypeStruct((n, di), F32), jax.ShapeDtypeStruct((n, gn2), F32)],
        compiler_params=_params("parallel", "parallel"),
        name="ssd_conv",
    )(u, u, u, u, u, u, wx, wb, bx, bb)


def _ssd_scan_kernel(x_ref, bc_ref, dtr_ref, dtb_ref, alog_ref, *rest, reverse, final):
    if final:
        z_ref, yo_ref, dskip_ref, ng_ref, o_ref, st_ref = rest
    else:
        o_ref, st_ref = rest
    c = pl.program_id(1)
    q = CHUNK
    gn = SSD_GROUPS * SSD_STATE
    hpg = SSD_HEADS // SSD_GROUPS
    gw = hpg * SSD_HEADDIM
    hoff = SSD_HEADS if reverse else 0

    @pl.when(c == 0)
    def _():
        st_ref[...] = jnp.zeros(st_ref.shape, F32)

    x = x_ref[...]
    bm = bc_ref[:, :gn].astype(BF16)
    cm = bc_ref[:, gn:].astype(BF16)
    dt = _softplus(dtr_ref[...] + dtb_ref[...])
    da = dt * (-jnp.exp(alog_ref[...]))
    row = lax.broadcasted_iota(jnp.int32, (q, q), 0)
    col = lax.broadcasted_iota(jnp.int32, (q, q), 1)
    valid = (col >= row) if reverse else (col <= row)
    tri = valid.astype(F32)
    cs_col = _dot_exact(tri, da)
    cs_row = cs_col.T
    dt_row = dt.T
    tot_row = cs_row[:, 0:1] if reverse else cs_row[:, q - 1:q]
    tot_col = cs_col[0:1, :] if reverse else cs_col[q - 1:q, :]
    e_col = jnp.exp(cs_col)
    w_col = jnp.exp(tot_col - cs_col) * dt
    lane = lax.broadcasted_iota(jnp.int32, (q, LANES), 1)
    lo = lane < SSD_HEADDIM

    def expand(v, h0):
        return jnp.where(lo, v[:, h0:h0 + 1], v[:, h0 + 1:h0 + 2])

    for g in range(SSD_GROUPS):
        b_g = bm[:, g * SSD_STATE:(g + 1) * SSD_STATE]
        c_g = cm[:, g * SSD_STATE:(g + 1) * SSD_STATE]
        cb = _dot_nt(c_g, b_g)
        s_old = st_ref[g]
        y_off = _dot_nt(c_g, s_old.astype(BF16))
        xw_parts, y_parts = [], []
        for pp in range(hpg // 2):
            h0 = g * hpg + 2 * pp
            cols = slice(h0 * SSD_HEADDIM, (h0 + 2) * SSD_HEADDIM)
            x2 = x[:, cols]
            x2b = x2.astype(BF16)
            ys = []
            for h in (h0, h0 + 1):
                hl = hoff + h
                diff = cs_col[:, hl:hl + 1] - cs_row[hl:hl + 1, :]
                lmat = jnp.exp(jnp.where(valid, diff, NEG_BIG))
                m = (cb * lmat * dt_row[hl:hl + 1, :]).astype(BF16)
                ys.append(_dot(m, x2b))
            y2 = jnp.where(lo, ys[0], ys[1])
            off2 = y_off[:, 2 * pp * SSD_HEADDIM:(2 * pp + 2) * SSD_HEADDIM]
            y2 = y2 + off2 * expand(e_col, hoff + h0)
            xw_parts.append((x2 * expand(w_col, hoff + h0)).astype(BF16))
            if final:
                y2 = y2 + yo_ref[:, cols] + x2 * dskip_ref[:, cols]
                y_parts.append(y2 * _silu(z_ref[:, cols]))
            else:
                o_ref[:, cols] = y2
        xw = jnp.concatenate(xw_parts, axis=1)
        states = _dot_tn(xw, b_g)
        for j in range(hpg):
            hl = hoff + g * hpg + j
            rows = slice(j * SSD_HEADDIM, (j + 1) * SSD_HEADDIM)
            st_ref[g, rows, :] = s_old[rows, :] * jnp.exp(tot_row[hl:hl + 1, :]) + states[rows, :]
        if final:
            gcols = slice(g * gw, (g + 1) * gw)
            yg = jnp.concatenate(y_parts, axis=1)
            ms = jnp.mean(yg * yg, axis=-1, keepdims=True)
            o_ref[:, gcols] = (yg * lax.rsqrt(ms + EPS) * ng_ref[:, gcols]).astype(o_ref.dtype)


def ssd_scan(xc, bc, dtr, dt_bias, a_log, batch, seq, *, reverse, final_args=None):
    n, di = xc.shape
    nc = seq // CHUNK
    gn2 = bc.shape[1]
    final = final_args is not None

    def rows(b, c):
        return b * nc + (nc - 1 - c if reverse else c)

    blk = lambda w: pl.BlockSpec((CHUNK, w), lambda b, c: (rows(b, c), 0))
    const = lambda w: pl.BlockSpec((1, w), lambda b, c: (0, 0))
    in_specs = [blk(di), blk(gn2), blk(LANES), const(LANES), const(LANES)]
    args = [xc, bc, dtr, dt_bias, a_log]
    if final:
        u, y_other, dskip, ng = final_args
        in_specs += [blk(di), blk(di), const(di), const(di)]
        args += [u, y_other, dskip, ng]
    return pl.pallas_call(
        functools.partial(_ssd_scan_kernel, reverse=reverse, final=final),
        grid=(batch, nc),
        in_specs=in_specs,
        out_specs=blk(di),
        out_shape=jax.ShapeDtypeStruct((n, di), BF16 if final else F32),
        scratch_shapes=[pltpu.VMEM((SSD_GROUPS, (SSD_HEADS // SSD_GROUPS) * SSD_HEADDIM, SSD_STATE), F32)],
        compiler_params=_params("parallel", "arbitrary"),
        name="ssd_scan_rev" if reverse else "ssd_scan_fwd",
    )(*args)


def ssd_layer(x, g, w_in, conv_w, conv_b, dt_bias, a_log, d_skip, norm_g, w_out, batch, seq, final_g=None):
    di = SSD_HEADS * SSD_HEADDIM
    main = 2 * di + 2 * SSD_GROUPS * SSD_STATE
    w_main = w_in[:, :main].astype(BF16)
    w_dt = jnp.pad(w_in[:, main:], ((0, 0), (0, LANES - 2 * SSD_HEADS))).astype(BF16)
    u = proj_in(x, g, w_main)
    dtr = proj_in(x, g, w_dt)
    xc, bc = ssd_conv(u, conv_w, conv_b, batch, seq)
    pad = lambda v: jnp.pad(v.reshape(1, 2 * SSD_HEADS), ((0, 0), (0, LANES - 2 * SSD_HEADS)))
    dtb, alog = pad(dt_bias), pad(a_log)
    y_rev = ssd_scan(xc, bc, dtr, dtb, alog, batch, seq, reverse=True)
    dskip = jnp.repeat(d_skip, SSD_HEADDIM).reshape(1, di)
    y = ssd_scan(xc, bc, dtr, dtb, alog, batch, seq, reverse=False,
                 final_args=(u, y_rev, dskip, norm_g.reshape(1, di)))
    return proj_out(y, w_out.astype(BF16), x, final_g)


def _hgrn_scan_kernel(q_ref, f_ref, v_ref, lb_ref, *rest, reverse, final):
    if final:
        gate_ref, oo_ref, ng_ref, o_ref, st_ref = rest
    else:
        o_ref, st_ref = rest
    c = pl.program_id(1)
    n = CHUNK
    nsub = n // HG_SUB
    dk = LANES

    @pl.when(c == 0)
    def _():
        st_ref[...] = jnp.zeros(st_ref.shape, F32)

    lb = lb_ref[...]
    qa = _silu(q_ref[...])
    f = lb + (1.0 - lb) * (1.0 / (1.0 + jnp.exp(-f_ref[...])))
    ka = 1.0 - f
    lg = jnp.log(f)
    row = lax.broadcasted_iota(jnp.int32, (n, n), 0)
    col = lax.broadcasted_iota(jnp.int32, (n, n), 1)
    valid = (col >= row) if reverse else (col <= row)
    gsum = _dot_exact(valid.astype(F32), lg)
    ref_row = HG_SUB // 2 - 1 if reverse else HG_SUB // 2
    far = 0 if reverse else n - 1

    for h in range(HG_HEADS):
        cols = slice(h * dk, (h + 1) * dk)
        gh, qh, kh = gsum[:, cols], qa[:, cols], ka[:, cols]
        vh = v_ref[:, cols].astype(BF16)
        tot = gh[far:far + 1, :]
        qp, kn, anchors = [], [], []
        for s in range(nsub):
            rs = slice(s * HG_SUB, (s + 1) * HG_SUB)
            a = gh[s * HG_SUB + ref_row:s * HG_SUB + ref_row + 1, :]
            anchors.append(a)
            qp.append(qh[rs] * jnp.exp(gh[rs] - a))
            kn.append(kh[rs] * jnp.exp(a - gh[rs]))
        att_cols = []
        for j in range(nsub):
            lhs = []
            for i in range(nsub):
                live = (i <= j) if reverse else (i >= j)
                if not live:
                    lhs.append(jnp.zeros((HG_SUB, dk), F32))
                elif i == j:
                    lhs.append(qp[i])
                else:
                    lhs.append(qp[i] * jnp.exp(anchors[i] - anchors[j]))
            att_cols.append(_dot_nt(jnp.concatenate(lhs, axis=0).astype(BF16), kn[j].astype(BF16)))
        att = jnp.where(valid, jnp.concatenate(att_cols, axis=1), 0.0)
        s_old = st_ref[h]
        qe = jnp.concatenate([qp[s] * jnp.exp(anchors[s]) for s in range(nsub)], axis=0)
        kd = jnp.concatenate([kn[s] * jnp.exp(tot - anchors[s]) for s in range(nsub)], axis=0)
        o = _dot(att.astype(BF16), vh) + _dot_nt(qe.astype(BF16), s_old.astype(BF16))
        st_ref[h] = s_old * jnp.exp(tot) + _dot_tn(vh, kd.astype(BF16))
        if final:
            o = o + oo_ref[:, cols]
            ms = jnp.mean(o * o, axis=-1, keepdims=True)
            o = o * lax.rsqrt(ms + EPS) * ng_ref[:, cols] * _silu(gate_ref[:, cols])
        o_ref[:, cols] = o.astype(o_ref.dtype)


def hgrn_scan(u, lb, batch, seq, *, reverse, final_args=None):
    n = u.shape[0]
    w = HG_HEADS * LANES
    nc = seq // CHUNK
    final = final_args is not None

    def rows(b, c):
        return b * nc + (nc - 1 - c if reverse else c)

    ublk = lambda j: pl.BlockSpec((CHUNK, w), lambda b, c: (rows(b, c), j))
    const = pl.BlockSpec((1, w), lambda b, c: (0, 0))
    in_specs = [ublk(0), ublk(2 if reverse else 1), ublk(3), const]
    args = [u, u, u, lb.reshape(1, w)]
    if final:
        o_other, ng = final_args
        in_specs += [ublk(4), ublk(0), const]
        args += [u, o_other, ng.reshape(1, w)]
    return pl.pallas_call(
        functools.partial(_hgrn_scan_kernel, reverse=reverse, final=final),
        grid=(batch, nc),
        in_specs=in_specs,
        out_specs=ublk(0),
        out_shape=jax.ShapeDtypeStruct((n, w), BF16 if final else F32),
        scratch_shapes=[pltpu.VMEM((HG_HEADS, LANES, LANES), F32)],
        compiler_params=_params("parallel", "arbitrary"),
        name="hgrn_scan_rev" if reverse else "hgrn_scan_fwd",
    )(*args)


def hgrn_layer(x, g, lb, w_in, norm_g, w_out, batch, seq, final_g=None):
    u = proj_in(x, g, w_in.astype(BF16))
    o_rev = hgrn_scan(u, lb, batch, seq, reverse=True)
    o = hgrn_scan(u, lb, batch, seq, reverse=False, final_args=(o_rev, norm_g))
    return proj_out(o, w_out.astype(BF16), x, final_g)


def _rope_tables(seq):
    pos = np.arange(seq)
    rowp = (pos // GRID_W).astype(np.float64)
    colp = (pos % GRID_W).astype(np.float64)
    half = AT_HD // 4
    inv = ROPE_THETA ** (-np.arange(0, 2 * half, 2, dtype=np.float64) / (2 * half))
    ar, ac = rowp[:, None] * inv, colp[:, None] * inv
    cos = np.concatenate([np.cos(ar), np.cos(ar), np.cos(ac), np.cos(ac)], axis=1)
    sin = np.concatenate([-np.sin(ar), np.sin(ar), -np.sin(ac), np.sin(ac)], axis=1)
    return jnp.asarray(cos, F32), jnp.asarray(sin, F32)


def _gqa_prep_kernel(q_ref, k_ref, v_ref, cos_ref, sin_ref, qg_ref, kg_ref, qo_ref, ko_ref, vo_ref):
    cos, sin = cos_ref[...], sin_ref[...]
    lane = lax.broadcasted_iota(jnp.int32, cos.shape, 1)
    first = (lane % (AT_HD // 2)) < (AT_HD // 4)

    def norm_rope(xh, gain, scale):
        ms = jnp.mean(xh * xh, axis=-1, keepdims=True)
        xn = xh * lax.rsqrt(ms + EPS) * gain
        swapped = jnp.where(first, pltpu.roll(xn, AT_HD - AT_HD // 4, axis=1), pltpu.roll(xn, AT_HD // 4, axis=1))
        return (xn * cos + swapped * sin) * scale

    for h in range(AT_HEADS):
        cols = slice(h * AT_HD, (h + 1) * AT_HD)
        qo_ref[:, cols] = norm_rope(q_ref[:, cols], qg_ref[...], AT_HD ** -0.5).astype(BF16)
    for h in range(AT_KV):
        cols = slice(h * AT_HD, (h + 1) * AT_HD)
        ko_ref[:, cols] = norm_rope(k_ref[:, cols], kg_ref[...], 1.0).astype(BF16)
    vo_ref[...] = v_ref[...].astype(BF16)


def gqa_prep(u, q_g, k_g, seq, *, tm=256):
    n = u.shape[0]
    qw, kw = AT_HEADS * AT_HD, AT_KV * AT_HD
    nb = seq // tm
    cos, sin = _rope_tables(seq)
    row = lambda j: (lambda i: (i, j))
    pos = lambda i: (i % nb, 0)
    const = lambda i: (0, 0)
    return pl.pallas_call(
        _gqa_prep_kernel,
        grid=(n // tm,),
        in_specs=[pl.BlockSpec((tm, qw), row(0)), pl.BlockSpec((tm, kw), row(2)), pl.BlockSpec((tm, kw), row(3)),
                  pl.BlockSpec((tm, AT_HD), pos), pl.BlockSpec((tm, AT_HD), pos),
                  pl.BlockSpec((1, AT_HD), const), pl.BlockSpec((1, AT_HD), const)],
        out_specs=[pl.BlockSpec((tm, qw), row(0)), pl.BlockSpec((tm, kw), row(0)), pl.BlockSpec((tm, kw), row(0))],
        out_shape=[jax.ShapeDtypeStruct((n, qw), BF16), jax.ShapeDtypeStruct((n, kw), BF16),
                   jax.ShapeDtypeStruct((n, kw), BF16)],
        compiler_params=_params("parallel"),
        name="gqa_prep",
    )(u, u, u, cos, sin, q_g.reshape(1, AT_HD), k_g.reshape(1, AT_HD))


def _gqa_flash_kernel(q_ref, k_ref, v_ref, gate_ref, o_ref, *, tk):
    tq = q_ref.shape[0]
    seq = k_ref.shape[0]
    grp = AT_HEADS // AT_KV
    qs = jnp.concatenate([q_ref[:, j * AT_HD:(j + 1) * AT_HD] for j in range(grp)], axis=0)

    def body(t, carry):
        m, l, acc = carry
        ks = k_ref[pl.ds(t * tk, tk), :]
        vs = v_ref[pl.ds(t * tk, tk), :]
        s = _dot_nt(qs, ks)
        m_new = jnp.maximum(m, jnp.max(s, axis=-1, keepdims=True))
        alpha = jnp.exp(m - m_new)
        p = jnp.exp(s - m_new)
        l = alpha * l + jnp.sum(p, axis=-1, keepdims=True)
        acc = alpha * acc + _dot(p.astype(BF16), vs)
        return m_new, l, acc

    init = (jnp.full((grp * tq, 1), -jnp.inf, F32), jnp.zeros((grp * tq, 1), F32),
            jnp.zeros((grp * tq, AT_HD), F32))
    m, l, acc = lax.fori_loop(0, seq // tk, body, init)
    o = acc / l
    for j in range(grp):
        cols = slice(j * AT_HD, (j + 1) * AT_HD)
        o_ref[:, cols] = (o[j * tq:(j + 1) * tq, :] * _silu(gate_ref[:, cols])).astype(o_ref.dtype)


def gqa_flash(q, k, v, u, batch, seq, *, tq=256, tk=512):
    n = q.shape[0]
    grp = AT_HEADS // AT_KV
    gw = grp * AT_HD
    nq = seq // tq
    gate_blk0 = (AT_HEADS + 2 * AT_KV) * AT_HD // gw
    return pl.pallas_call(
        functools.partial(_gqa_flash_kernel, tk=min(tk, seq)),
        grid=(batch, AT_KV, nq),
        in_specs=[pl.BlockSpec((tq, gw), lambda b, h, i: (b * nq + i, h)),
                  pl.BlockSpec((seq, AT_HD), lambda b, h, i: (b, h)),
                  pl.BlockSpec((seq, AT_HD), lambda b, h, i: (b, h)),
                  pl.BlockSpec((tq, gw), lambda b, h, i: (b * nq + i, gate_blk0 + h))],
        out_specs=pl.BlockSpec((tq, gw), lambda b, h, i: (b * nq + i, h)),
        out_shape=jax.ShapeDtypeStruct((n, AT_HEADS * AT_HD), BF16),
        compiler_params=_params("parallel", "parallel", "arbitrary"),
        name="gqa_flash",
    )(q, k, v, u)


def gqa_layer(x, g, w_in, q_g, k_g, w_out, batch, seq, final_g=None):
    u = proj_in(x, g, w_in.astype(BF16))
    q, k, v = gqa_prep(u, q_g, k_g, seq)
    o = gqa_flash(q, k, v, u, batch, seq)
    return proj_out(o, w_out.astype(BF16), x, final_g)


def _t5_bucket_np(rel):
    half = REL_BUCKETS // 2
    exact = half // 2
    nabs = np.abs(rel)
    large = exact + (np.log(np.maximum(nabs, 1).astype(np.float32) / exact)
                     / math.log(REL_MAX_DIST / exact) * (half - exact)).astype(np.int32)
    large = np.minimum(large, half - 1)
    return np.where(rel > 0, half, 0) + np.where(nabs < exact, nabs, large)


DL_TQ = 128
DL_HALO = 64
DL_NK = DL_TQ + 2 * DL_HALO


def _dilated_buckets():
    qi = np.arange(DL_TQ)[:, None]
    kj = np.arange(DL_NK)[None, :]
    tabs = [_t5_bucket_np((kj - DL_HALO - qi) * dil).astype(np.int32) for _, dil in DL_PAIRS]
    band = np.abs(kj - DL_HALO - qi) <= DL_HALO
    present = [sorted(set(t[band].tolist())) for t in tabs]
    return np.stack(tabs), present


def _dilated_kernel(relb_ref, bucket_ref, *refs, seq, tok, present):
    ng = len(DL_PAIRS)
    io = refs[:7 * ng]
    gate_ref, out_ref, bias_ref, kbuf, vbuf, obuf, lbuf = refs[7 * ng:]
    i = pl.program_id(1)
    p = pl.program_id(2)
    qi = lax.broadcasted_iota(jnp.int32, (DL_TQ, DL_NK), 0)
    kj = lax.broadcasted_iota(jnp.int32, (DL_TQ, DL_NK), 1)
    band = jnp.abs(kj - DL_HALO - qi) <= DL_HALO
    lane = lax.broadcasted_iota(jnp.int32, (DL_TQ, LANES), 1)
    lo = lane < DL_HD
    scale = DL_HD ** -0.5

    @pl.when((pl.program_id(0) == 0) & (i == 0) & (p == 0))
    def _():
        for g in range(ng):
            bk = bucket_ref[g]

            def fill(h, carry, g=g, bk=bk):
                acc = jnp.zeros((DL_TQ, DL_NK), F32)
                for b in present[g]:
                    acc = jnp.where(bk == b, relb_ref[b, h], acc)
                bias_ref[g, h] = jnp.where(band, acc, NEG_BIG)
                return carry

            lax.fori_loop(0, DL_HEADS, fill, 0)

    for g, (_, dil) in enumerate(DL_PAIRS):
        q_ref, kp_ref, kc_ref, kn_ref, vp_ref, vc_ref, vn_ref = io[7 * g:7 * g + 7]
        halo = DL_HALO * dil
        ls = seq // dil
        ppb = tok // dil
        nsub = ppb // DL_TQ
        kbuf[0:halo, :] = kp_ref[...]
        kbuf[halo:halo + tok, :] = kc_ref[...]
        kbuf[halo + tok:2 * halo + tok, :] = kn_ref[...]
        vbuf[0:halo, :] = vp_ref[...]
        vbuf[halo:halo + tok, :] = vc_ref[...]
        vbuf[halo + tok:2 * halo + tok, :] = vn_ref[...]

        def block(idx, carry, g=g, dil=dil, ls=ls, ppb=ppb, nsub=nsub, q_ref=q_ref):
            r = idx // nsub
            j = idx % nsub
            start = r + j * (DL_TQ * dil)
            if dil == 1:
                qrows, krows = pl.ds(start, DL_TQ), pl.ds(start, DL_NK)
            else:
                qrows, krows = pl.ds(start, DL_TQ, stride=dil), pl.ds(start, DL_NK, stride=dil)
            q2 = q_ref[qrows, :] * scale
            k2 = kbuf[krows, :].astype(BF16)
            v2 = vbuf[krows, :].astype(BF16)
            mk = i * ppb + j * DL_TQ - DL_HALO + kj
            inside = (mk >= 0) & (mk < ls)
            outs, lses = [], []
            for half in range(2):
                qh = jnp.where(lo if half == 0 else ~lo, q2, 0.0).astype(BF16)
                s = _dot_nt(qh, k2) + bias_ref[g, 2 * p + half]
                s = jnp.where(inside, s, NEG_BIG)
                m = jnp.max(s, axis=-1, keepdims=True)
                pexp = jnp.exp(s - m)
                l = jnp.sum(pexp, axis=-1, keepdims=True)
                outs.append(_dot(pexp.astype(BF16), v2) / l)
                lses.append(m + jnp.log(l))
            obuf[g, qrows, :] = jnp.where(lo, outs[0], outs[1])
            lbuf[g, qrows, :] = jnp.where(lo, lses[0], lses[1])
            return carry

        lax.fori_loop(0, dil * nsub, block, 0, unroll=8)

    la, lb, lc = lbuf[0], lbuf[1], lbuf[2]
    m = jnp.maximum(jnp.maximum(la, lb), lc)
    ea, eb, ec = jnp.exp(la - m), jnp.exp(lb - m), jnp.exp(lc - m)
    o = (ea * obuf[0] + eb * obuf[1] + ec * obuf[2]) / (ea + eb + ec)
    out_ref[...] = (o * _silu(gate_ref[...])).astype(out_ref.dtype)


def dilated_attention(u, rel_bias, batch, seq, *, tok=2048):
    n, win = u.shape
    hw = DL_HEADS * DL_HD
    npair = hw // LANES
    tok = min(tok, seq)
    nblk = seq // tok
    max_halo = DL_HALO * max(d for _, d in DL_PAIRS)
    assert tok % (DL_TQ * max(d for _, d in DL_PAIRS)) == 0 and tok % max_halo == 0
    buckets, present = _dilated_buckets()

    in_specs = [pl.BlockSpec(memory_space=pltpu.SMEM),
                pl.BlockSpec((len(DL_PAIRS), DL_TQ, DL_NK), lambda b, i, p: (0, 0, 0))]
    args = [rel_bias, jnp.asarray(buckets)]
    for gi, (_, dil) in enumerate(DL_PAIRS):
        halo = DL_HALO * dil
        hpb = tok // halo
        last = n // halo - 1

        def col(c, gi=gi):
            return lambda p: (gi * 3 + c) * npair + p

        def cur(c):
            cf = col(c)
            return pl.BlockSpec((tok, LANES), lambda b, i, p: (b * nblk + i, cf(p)))

        def prev(c, hpb=hpb):
            cf = col(c)
            return pl.BlockSpec((halo, LANES), lambda b, i, p: (jnp.maximum((b * nblk + i) * hpb - 1, 0), cf(p)))

        def nxt(c, hpb=hpb, last=last):
            cf = col(c)
            return pl.BlockSpec((halo, LANES),
                                lambda b, i, p: (jnp.minimum((b * nblk + i + 1) * hpb, last), cf(p)))

        in_specs += [cur(0), prev(1), cur(1), nxt(1), prev(2), cur(2), nxt(2)]
        args += [u] * 7
    gate_col0 = 3 * len(DL_PAIRS) * npair
    in_specs.append(pl.BlockSpec((tok, LANES), lambda b, i, p: (b * nblk + i, gate_col0 + p)))
    args.append(u)
    return pl.pallas_call(
        functools.partial(_dilated_kernel, seq=seq, tok=tok, present=present),
        grid=(batch, nblk, npair),
        in_specs=in_specs,
        out_specs=pl.BlockSpec((tok, LANES), lambda b, i, p: (b * nblk + i, p)),
        out_shape=jax.ShapeDtypeStruct((n, hw), BF16),
        scratch_shapes=[pltpu.VMEM((len(DL_PAIRS), DL_HEADS, DL_TQ, DL_NK), F32),
                        pltpu.VMEM((tok + 2 * max_halo, LANES), F32),
                        pltpu.VMEM((tok + 2 * max_halo, LANES), F32),
                        pltpu.VMEM((len(DL_PAIRS), tok, LANES), F32),
                        pltpu.VMEM((len(DL_PAIRS), tok, LANES), F32)],
        compiler_params=_params("arbitrary", "arbitrary", "arbitrary"),
        name="dilated_attention",
    )(*args)


def dilated_layer(x, g, rel_bias, w_in, w_out, batch, seq, final_g=None):
    u = proj_in(x, g, w_in.astype(BF16))
    o = dilated_attention(u, rel_bias, batch, seq)
    return proj_out(o, w_out.astype(BF16), x, final_g)


def kernel(x, norm_g, final_g, rel_bias, hgrn_lb, ssd_w_in, ssd_conv_w, ssd_conv_b, ssd_dt_bias, ssd_a_log, ssd_d,
           ssd_norm_g, ssd_w_out, hg_w_in, hg_norm_g, hg_w_out, at_w_in, at_q_norm_g, at_k_norm_g, at_w_out,
           dl_w_in, dl_w_out):
    batch, seq, d = x.shape
    depth = norm_g.shape[0]
    n_mixers = 4
    lb_sm = jax.nn.softmax(hgrn_lb.astype(F32), axis=0)
    lb_all = jnp.cumsum(lb_sm, axis=0) - lb_sm[0:1]
    h = x.reshape(batch * seq, d)
    for layer in range(depth):
        kind, slot = layer % n_mixers, layer // n_mixers
        fg = final_g if layer == depth - 1 else None
        if kind == 0:
            h = ssd_layer(h, norm_g[layer], ssd_w_in[slot], ssd_conv_w[slot], ssd_conv_b[slot], ssd_dt_bias[slot],
                          ssd_a_log[slot], ssd_d[slot], ssd_norm_g[slot], ssd_w_out[slot], batch, seq, fg)
        elif kind == 1:
            h = hgrn_layer(h, norm_g[layer], lb_all[layer], hg_w_in[slot], hg_norm_g[slot], hg_w_out[slot],
                           batch, seq, fg)
        elif kind == 2:
            h = gqa_layer(h, norm_g[layer], at_w_in[slot], at_q_norm_g[slot], at_k_norm_g[slot], at_w_out[slot],
                          batch, seq, fg)
        else:
            h = dilated_layer(h, norm_g[layer], rel_bias, dl_w_in[slot], dl_w_out[slot], batch, seq, fg)
    return h.reshape(batch, seq, d)
```

```python
import functools
import math

import jax
import jax.numpy as jnp
import numpy as np
from jax import lax
from jax.experimental import pallas as pl
from jax.experimental.pallas import tpu as pltpu

F32 = jnp.float32
BF16 = jnp.bfloat16

EPS = 1e-6
NEG_BIG = -1e30
GRID_W = 64
ROPE_THETA = 10000.0

SSD_HEADDIM = 64
SSD_HEADS = 32
SSD_GROUPS = 4
SSD_STATE = 128
SSD_CONV = 7
HG_HEADS = 8
HG_SUB = 32
AT_HEADS = 16
AT_KV = 8
AT_HD = 128
DL_PAIRS = ((128, 1), (512, 4), (2048, 16))
DL_HEADS = 16
DL_HD = 64
REL_BUCKETS = 32
REL_MAX_DIST = 1024

LANES = 128
SUBLANES = 8
CHUNK = 128
VMEM_LIMIT = 56 * 1024 * 1024


def _params(*sem):
    return pltpu.CompilerParams(dimension_semantics=sem, vmem_limit_bytes=VMEM_LIMIT)


def _sigmoid(x):
    return 0.5 * jnp.tanh(0.5 * x) + 0.5


def _silu(x):
    return x * _sigmoid(x)


def _softplus(x):
    return jnp.maximum(x, 0.0) + jnp.log(1.0 + jnp.exp(-jnp.abs(x)))


def _dot(a, b):
    return jnp.dot(a, b, preferred_element_type=F32)


def _dot_nt(a, b):
    return lax.dot_general(a, b, (((1,), (1,)), ((), ())), preferred_element_type=F32)


def _dot_tn(a, b):
    return lax.dot_general(a, b, (((0,), (0,)), ((), ())), preferred_element_type=F32)


def _prefix_sum(tri, x):
    x1 = x.astype(BF16)
    r1 = x - x1.astype(F32)
    x2 = r1.astype(BF16)
    x3 = (r1 - x2.astype(F32)).astype(BF16)
    return _dot(tri, x1) + _dot(tri, x2) + _dot(tri, x3)


def _proj_in_kernel(x_ref, g_ref, w_ref, o_ref, xn_ref):
    @pl.when(pl.program_id(1) == 0)
    def _():
        x = x_ref[...]
        ms = jnp.mean(x * x, axis=-1, keepdims=True)
        xn_ref[...] = (x * lax.rsqrt(ms + EPS) * g_ref[...]).astype(BF16)

    o_ref[...] = _dot(xn_ref[...], w_ref[...]).astype(o_ref.dtype)


def proj_in(x, g, w, *, tm=1024, tn=1024, out_dtype=F32):
    n, d = x.shape
    dout = w.shape[1]
    tm = min(tm, n)
    tn = min(tn, dout)
    assert n % tm == 0 and dout % tn == 0
    return pl.pallas_call(
        _proj_in_kernel,
        grid=(n // tm, dout // tn),
        in_specs=[pl.BlockSpec((tm, d), lambda i, j: (i, 0)),
                  pl.BlockSpec((1, d), lambda i, j: (0, 0)),
                  pl.BlockSpec((d, tn), lambda i, j: (0, j))],
        out_specs=pl.BlockSpec((tm, tn), lambda i, j: (i, j)),
        out_shape=jax.ShapeDtypeStruct((n, dout), out_dtype),
        scratch_shapes=[pltpu.VMEM((tm, d), BF16)],
        compiler_params=_params("parallel", "arbitrary"),
        name="proj_in",
    )(x, g.reshape(1, d), w)


def _proj_out_kernel(a_ref, w_ref, r_ref, g_ref, o_ref, *, final):
    y = r_ref[...] + _dot(a_ref[...], w_ref[...])
    if final:
        ms = jnp.mean(y * y, axis=-1, keepdims=True)
        y = y * lax.rsqrt(ms + EPS) * g_ref[...]
    o_ref[...] = y


def proj_out(a, w, res, final_g=None, *, tm=512):
    n, k = a.shape
    d = w.shape[1]
    tm = min(tm, n)
    assert n % tm == 0
    g = jnp.ones((1, d), F32) if final_g is None else final_g.reshape(1, d)
    return pl.pallas_call(
        functools.partial(_proj_out_kernel, final=final_g is not None),
        grid=(n // tm,),
        in_specs=[pl.BlockSpec((tm, k), lambda i: (i, 0)),
                  pl.BlockSpec((k, d), lambda i: (0, 0)),
                  pl.BlockSpec((tm, d), lambda i: (i, 0)),
                  pl.BlockSpec((1, d), lambda i: (0, 0))],
        out_specs=pl.BlockSpec((tm, d), lambda i: (i, 0)),
        out_shape=jax.ShapeDtypeStruct((n, d), F32),
        compiler_params=_params("parallel"),
        name="proj_out",
    )(a, w, res, g)


def _ssd_conv_kernel(xp_ref, xc_ref, xn_ref, bp_ref, bc_ref, bn_ref, wx_ref, wb_ref, bx_ref, bb_ref,
                     ox_ref, ob_ref, *, nblk):
    i = pl.program_id(1)
    pad = SSD_CONV // 2

    def conv(prev_ref, cur_ref, next_ref, w_ref, b_ref, o_ref):
        rows = cur_ref.shape[0]
        prev = jnp.where(i > 0, prev_ref[...], 0.0)
        nxt = jnp.where(i < nblk - 1, next_ref[...], 0.0)
        ext = jnp.concatenate([prev, cur_ref[...], nxt], axis=0)
        acc = jnp.zeros(cur_ref.shape, F32) + b_ref[...]
        for t in range(SSD_CONV):
            off = SUBLANES - pad + t
            acc = acc + ext[off:off + rows, :] * w_ref[t:t + 1, :]
        o_ref[...] = _silu(acc)

    conv(xp_ref, xc_ref, xn_ref, wx_ref, bx_ref, ox_ref)
    conv(bp_ref, bc_ref, bn_ref, wb_ref, bb_ref, ob_ref)


def ssd_conv(u, conv_w, conv_b, batch, seq, *, tc=256):
    n = u.shape[0]
    di = SSD_HEADS * SSD_HEADDIM
    gn2 = 2 * SSD_GROUPS * SSD_STATE
    tc = min(tc, seq)
    nblk = seq // tc
    r8 = tc // SUBLANES
    last8 = n // SUBLANES - 1

    def cur(wblk):
        return lambda b, i: (b * nblk + i, wblk)

    def prev(wblk):
        return lambda b, i: (jnp.maximum((b * nblk + i) * r8 - 1, 0), wblk)

    def nxt(wblk):
        return lambda b, i: (jnp.minimum((b * nblk + i + 1) * r8, last8), wblk)

    wx, wb = conv_w[:, :di], conv_w[:, di:]
    bx, bb = conv_b[:di].reshape(1, di), conv_b[di:].reshape(1, gn2)
    const = lambda b, i: (0, 0)
    return pl.pallas_call(
        functools.partial(_ssd_conv_kernel, nblk=nblk),
        grid=(batch, nblk),
        in_specs=[pl.BlockSpec((SUBLANES, di), prev(1)), pl.BlockSpec((tc, di), cur(1)),
                  pl.BlockSpec((SUBLANES, di), nxt(1)),
                  pl.BlockSpec((SUBLANES, gn2), prev(4)), pl.BlockSpec((tc, gn2), cur(4)),
                  pl.BlockSpec((SUBLANES, gn2), nxt(4)),
                  pl.BlockSpec((SSD_CONV, di), const), pl.BlockSpec((SSD_CONV, gn2), const),
                  pl.BlockSpec((1, di), const), pl.BlockSpec((1, gn2), const)],
        out_specs=[pl.BlockSpec((tc, di), lambda b, i: (b * nblk + i, 0)),
                   pl.BlockSpec((tc, gn2), lambda b, i: (b * nblk + i, 0))],
        out_shape=[jax.ShapeDtypeStruct((n, di), F32), jax.ShapeDtypeStruct((n, gn2), F32)],
        compiler_params=_params("parallel", "parallel"),
        name="ssd_conv",
    )(u, u, u, u, u, u, wx, wb, bx, bb)


def _ssd_scan_kernel(x_ref, bc_ref, dtr_ref, dtb_ref, alog_ref, *rest, reverse, final):
    if final:
        z_ref, yo_ref, dskip_ref, ng_ref, o_ref, st_ref = rest
    else:
        o_ref, st_ref = rest
    c = pl.program_id(1)
    nb = x_ref.shape[0]
    q = CHUNK
    gn = SSD_GROUPS * SSD_STATE
    hpg = SSD_HEADS // SSD_GROUPS
    gw = hpg * SSD_HEADDIM
    hoff = SSD_HEADS if reverse else 0
    far = 0 if reverse else q - 1

    @pl.when(c == 0)
    def _():
        st_ref[...] = jnp.zeros(st_ref.shape, F32)

    row = lax.broadcasted_iota(jnp.int32, (q, q), 0)
    col = lax.broadcasted_iota(jnp.int32, (q, q), 1)
    valid = (col >= row) if reverse else (col <= row)
    tri = valid.astype(BF16)
    lane = lax.broadcasted_iota(jnp.int32, (q, LANES), 1)
    lo = lane < SSD_HEADDIM
    lo_row = lo[0:1, :]
    neg_a = -jnp.exp(alog_ref[...])

    for bb in range(nb):
        x = x_ref[bb]
        dt = _softplus(dtr_ref[bb] + dtb_ref[...])
        cs_col = _prefix_sum(tri, dt * neg_a)
        cs_row = cs_col.T
        dt_row = dt.T
        w_row = jnp.exp(cs_row[:, far:far + 1] - cs_row) * dt_row
        dec = jnp.exp(cs_col[far:far + 1, :])
        for g in range(SSD_GROUPS):
            b_f = bc_ref[bb, :, g * SSD_STATE:(g + 1) * SSD_STATE]
            c_f = bc_ref[bb, :, gn + g * SSD_STATE:gn + (g + 1) * SSD_STATE]
            cb = _dot_nt(c_f.astype(BF16), b_f.astype(BF16))
            b_t = b_f.T
            y_parts = []
            for pp in range(hpg // 2):
                h0 = g * hpg + 2 * pp
                cols = slice(h0 * SSD_HEADDIM, (h0 + 2) * SSD_HEADDIM)
                scols = slice(2 * pp * SSD_HEADDIM, (2 * pp + 2) * SSD_HEADDIM)
                x2 = x[:, cols]
                x2b = x2.astype(BF16)
                s_old = st_ref[bb, g, :, scols]
                rhs = jnp.concatenate([x2b, s_old.astype(BF16)], axis=0)
                ys, sts = [], []
                for h in (h0, h0 + 1):
                    hl = hoff + h
                    colb = jnp.broadcast_to(cs_col[:, hl:hl + 1], (q, q))
                    lmat = jnp.exp(jnp.where(valid, colb - cs_row[hl:hl + 1, :], NEG_BIG))
                    m = cb * (lmat * dt_row[hl:hl + 1, :])
                    ce = c_f * jnp.exp(colb)
                    ys.append(_dot(jnp.concatenate([m, ce], axis=1).astype(BF16), rhs))
                    sts.append(_dot((b_t * w_row[hl:hl + 1, :]).astype(BF16), x2b))
                y2 = jnp.where(lo, ys[0], ys[1])
                dec2 = jnp.where(lo_row, dec[:, hoff + h0:hoff + h0 + 1], dec[:, hoff + h0 + 1:hoff + h0 + 2])
                st_ref[bb, g, :, scols] = s_old * dec2 + jnp.where(lo, sts[0], sts[1])
                if final:
                    y2 = y2 + yo_ref[bb, :, cols] + x2 * dskip_ref[:, cols]
                    y_parts.append(y2 * _silu(z_ref[bb, :, cols]))
                else:
                    o_ref[bb, :, cols] = y2
            if final:
                gcols = slice(g * gw, (g + 1) * gw)
                yg = jnp.concatenate(y_parts, axis=1)
                ms = jnp.mean(yg * yg, axis=-1, keepdims=True)
                o_ref[bb, :, gcols] = (yg * lax.rsqrt(ms + EPS) * ng_ref[:, gcols]).astype(o_ref.dtype)


def ssd_scan(xc, bc, dtr, dt_bias, a_log, batch, seq, *, reverse, final_args=None, nb=2):
    n, di = xc.shape
    nc = seq // CHUNK
    gn2 = bc.shape[1]
    final = final_args is not None
    nb = min(nb, batch)
    assert batch % nb == 0
    v3 = lambda a: a.reshape(batch, seq, a.shape[1])

    def chunk(c):
        return nc - 1 - c if reverse else c

    blk = lambda w: pl.BlockSpec((nb, CHUNK, w), lambda b, c: (b, chunk(c), 0))
    const = lambda w: pl.BlockSpec((1, w), lambda b, c: (0, 0))
    in_specs = [blk(di), blk(gn2), blk(LANES), const(LANES), const(LANES)]
    args = [v3(xc), v3(bc), v3(dtr), dt_bias, a_log]
    if final:
        u, y_other, dskip, ng = final_args
        in_specs += [blk(di), blk(di), const(di), const(di)]
        args += [v3(u), v3(y_other), dskip, ng]
    out = pl.pallas_call(
        functools.partial(_ssd_scan_kernel, reverse=reverse, final=final),
        grid=(batch // nb, nc),
        in_specs=in_specs,
        out_specs=blk(di),
        out_shape=jax.ShapeDtypeStruct((batch, seq, di), BF16 if final else F32),
        scratch_shapes=[pltpu.VMEM((nb, SSD_GROUPS, SSD_STATE, (SSD_HEADS // SSD_GROUPS) * SSD_HEADDIM), F32)],
        compiler_params=_params("parallel", "arbitrary"),
        name="ssd_scan_rev" if reverse else "ssd_scan_fwd",
    )(*args)
    return out.reshape(n, di)


def ssd_layer(x, g, w_in, conv_w, conv_b, dt_bias, a_log, d_skip, norm_g, w_out, batch, seq, final_g=None):
    di = SSD_HEADS * SSD_HEADDIM
    main = 2 * di + 2 * SSD_GROUPS * SSD_STATE
    w_main = w_in[:, :main].astype(BF16)
    w_dt = jnp.pad(w_in[:, main:], ((0, 0), (0, LANES - 2 * SSD_HEADS))).astype(BF16)
    u = proj_in(x, g, w_main)
    dtr = proj_in(x, g, w_dt)
    xc, bc = ssd_conv(u, conv_w, conv_b, batch, seq)
    pad = lambda v: jnp.pad(v.reshape(1, 2 * SSD_HEADS), ((0, 0), (0, LANES - 2 * SSD_HEADS)))
    dtb, alog = pad(dt_bias), pad(a_log)
    y_rev = ssd_scan(xc, bc, dtr, dtb, alog, batch, seq, reverse=True)
    dskip = jnp.repeat(d_skip, SSD_HEADDIM).reshape(1, di)
    y = ssd_scan(xc, bc, dtr, dtb, alog, batch, seq, reverse=False,
                 final_args=(u, y_rev, dskip, norm_g.reshape(1, di)))
    return proj_out(y, w_out.astype(BF16), x, final_g)


def _hgrn_scan_kernel(q_ref, f_ref, v_ref, lb_ref, *rest, reverse, final):
    if final:
        gate_ref, oo_ref, ng_ref, o_ref, st_ref = rest
    else:
        o_ref, st_ref = rest
    c = pl.program_id(1)
    n = CHUNK
    nsub = n // HG_SUB
    dk = LANES

    @pl.when(c == 0)
    def _():
        st_ref[...] = jnp.zeros(st_ref.shape, F32)

    nb = q_ref.shape[0]
    lb = lb_ref[...]
    row = lax.broadcasted_iota(jnp.int32, (n, n), 0)
    col = lax.broadcasted_iota(jnp.int32, (n, n), 1)
    valid = (col >= row) if reverse else (col <= row)
    tri = valid.astype(BF16)
    ref_row = HG_SUB // 2 - 1 if reverse else HG_SUB // 2
    far = 0 if reverse else n - 1

    for bb in range(nb):
        qa = _silu(q_ref[bb])
        f = lb + (1.0 - lb) * _sigmoid(f_ref[bb])
        ka = 1.0 - f
        gsum = _prefix_sum(tri, jnp.log(f))
        for h in range(HG_HEADS):
            cols = slice(h * dk, (h + 1) * dk)
            gh, qh, kh = gsum[:, cols], qa[:, cols], ka[:, cols]
            vh = v_ref[bb, :, cols].astype(BF16)
            tot = gh[far:far + 1, :]
            qp, kn, anchors = [], [], []
            for s in range(nsub):
                rs = slice(s * HG_SUB, (s + 1) * HG_SUB)
                a = gh[s * HG_SUB + ref_row:s * HG_SUB + ref_row + 1, :]
                anchors.append(a)
                qp.append(qh[rs] * jnp.exp(gh[rs] - a))
                kn.append(kh[rs] * jnp.exp(a - gh[rs]))
            att_cols = []
            for j in range(nsub):
                lhs = []
                for i in range(nsub):
                    live = (i <= j) if reverse else (i >= j)
                    if not live:
                        lhs.append(jnp.zeros((HG_SUB, dk), F32))
                    elif i == j:
                        lhs.append(qp[i])
                    else:
                        lhs.append(qp[i] * jnp.exp(anchors[i] - anchors[j]))
                att_cols.append(_dot_nt(jnp.concatenate(lhs, axis=0).astype(BF16), kn[j].astype(BF16)))
            att = jnp.where(valid, jnp.concatenate(att_cols, axis=1), 0.0)
            s_old = st_ref[bb, h]
            qe = jnp.concatenate([qp[s] * jnp.exp(anchors[s]) for s in range(nsub)], axis=0)
            kd = jnp.concatenate([kn[s] * jnp.exp(tot - anchors[s]) for s in range(nsub)], axis=0)
            o = _dot(att.astype(BF16), vh) + _dot_nt(qe.astype(BF16), s_old.astype(BF16))
            st_ref[bb, h] = s_old * jnp.exp(tot) + _dot_tn(vh, kd.astype(BF16))
            if final:
                o = o + oo_ref[bb, :, cols]
                ms = jnp.mean(o * o, axis=-1, keepdims=True)
                o = o * lax.rsqrt(ms + EPS) * ng_ref[:, cols] * _silu(gate_ref[bb, :, cols])
            o_ref[bb, :, cols] = o.astype(o_ref.dtype)


def hgrn_scan(u, lb, batch, seq, *, reverse, final_args=None, nb=2):
    n = u.shape[0]
    w = HG_HEADS * LANES
    nc = seq // CHUNK
    final = final_args is not None
    nb = min(nb, batch)
    assert batch % nb == 0
    v3 = lambda a: a.reshape(batch, seq, a.shape[1])

    def chunk(c):
        return nc - 1 - c if reverse else c

    ublk = lambda j: pl.BlockSpec((nb, CHUNK, w), lambda b, c: (b, chunk(c), j))
    const = pl.BlockSpec((1, w), lambda b, c: (0, 0))
    u3 = v3(u)
    in_specs = [ublk(0), ublk(2 if reverse else 1), ublk(3), const]
    args = [u3, u3, u3, lb.reshape(1, w)]
    if final:
        o_other, ng = final_args
        in_specs += [ublk(4), ublk(0), const]
        args += [u3, v3(o_other), ng.reshape(1, w)]
    out = pl.pallas_call(
        functools.partial(_hgrn_scan_kernel, reverse=reverse, final=final),
        grid=(batch // nb, nc),
        in_specs=in_specs,
        out_specs=ublk(0),
        out_shape=jax.ShapeDtypeStruct((batch, seq, w), BF16 if final else F32),
        scratch_shapes=[pltpu.VMEM((nb, HG_HEADS, LANES, LANES), F32)],
        compiler_params=_params("parallel", "arbitrary"),
        name="hgrn_scan_rev" if reverse else "hgrn_scan_fwd",
    )(*args)
    return out.reshape(n, w)


def hgrn_layer(x, g, lb, w_in, norm_g, w_out, batch, seq, final_g=None):
    u = proj_in(x, g, w_in.astype(BF16))
    o_rev = hgrn_scan(u, lb, batch, seq, reverse=True)
    o = hgrn_scan(u, lb, batch, seq, reverse=False, final_args=(o_rev, norm_g))
    return proj_out(o, w_out.astype(BF16), x, final_g)


def _rope_tables(seq):
    pos = np.arange(seq)
    rowp = (pos // GRID_W).astype(np.float64)
    colp = (pos % GRID_W).astype(np.float64)
    half = AT_HD // 4
    inv = ROPE_THETA ** (-np.arange(0, 2 * half, 2, dtype=np.float64) / (2 * half))
    ar, ac = rowp[:, None] * inv, colp[:, None] * inv
    cos = np.concatenate([np.cos(ar), np.cos(ar), np.cos(ac), np.cos(ac)], axis=1)
    sin = np.concatenate([-np.sin(ar), np.sin(ar), -np.sin(ac), np.sin(ac)], axis=1)
    return jnp.asarray(cos, F32), jnp.asarray(sin, F32)


def _gqa_prep_kernel(q_ref, k_ref, v_ref, cos_ref, sin_ref, qg_ref, kg_ref, qo_ref, ko_ref, vo_ref):
    cos, sin = cos_ref[...], sin_ref[...]
    lane = lax.broadcasted_iota(jnp.int32, cos.shape, 1)
    first = (lane % (AT_HD // 2)) < (AT_HD // 4)

    def norm_rope(xh, gain, scale):
        ms = jnp.mean(xh * xh, axis=-1, keepdims=True)
        xn = xh * lax.rsqrt(ms + EPS) * gain
        swapped = jnp.where(first, pltpu.roll(xn, AT_HD - AT_HD // 4, axis=1), pltpu.roll(xn, AT_HD // 4, axis=1))
        return (xn * cos + swapped * sin) * scale

    for h in range(AT_HEADS):
        cols = slice(h * AT_HD, (h + 1) * AT_HD)
        qo_ref[:, cols] = norm_rope(q_ref[:, cols], qg_ref[...], AT_HD ** -0.5 * math.log2(math.e)).astype(BF16)
    for h in range(AT_KV):
        cols = slice(h * AT_HD, (h + 1) * AT_HD)
        ko_ref[:, cols] = norm_rope(k_ref[:, cols], kg_ref[...], 1.0).astype(BF16)
    vo_ref[...] = v_ref[...].astype(BF16)


def gqa_prep(u, q_g, k_g, seq, *, tm=256):
    n = u.shape[0]
    qw, kw = AT_HEADS * AT_HD, AT_KV * AT_HD
    nb = seq // tm
    cos, sin = _rope_tables(seq)
    row = lambda j: (lambda i: (i, j))
    pos = lambda i: (i % nb, 0)
    const = lambda i: (0, 0)
    return pl.pallas_call(
        _gqa_prep_kernel,
        grid=(n // tm,),
        in_specs=[pl.BlockSpec((tm, qw), row(0)), pl.BlockSpec((tm, kw), row(2)), pl.BlockSpec((tm, kw), row(3)),
                  pl.BlockSpec((tm, AT_HD), pos), pl.BlockSpec((tm, AT_HD), pos),
                  pl.BlockSpec((1, AT_HD), const), pl.BlockSpec((1, AT_HD), const)],
        out_specs=[pl.BlockSpec((tm, qw), row(0)), pl.BlockSpec((tm, kw), row(0)), pl.BlockSpec((tm, kw), row(0))],
        out_shape=[jax.ShapeDtypeStruct((n, qw), BF16), jax.ShapeDtypeStruct((n, kw), BF16),
                   jax.ShapeDtypeStruct((n, kw), BF16)],
        compiler_params=_params("parallel"),
        name="gqa_prep",
    )(u, u, u, cos, sin, q_g.reshape(1, AT_HD), k_g.reshape(1, AT_HD))


AT_SAFE_LOG2_RANGE = 100.0


def _gqa_flash_kernel(q_ref, k_ref, v_ref, gate_ref, o_ref, kmax_ref, *, tk):
    tq = q_ref.shape[0]
    seq = k_ref.shape[0]
    grp = AT_HEADS // AT_KV
    rows = grp * tq
    nkv = seq // tk

    @pl.when(pl.program_id(2) == 0)
    def _():
        kf = k_ref[...].astype(F32)
        k2 = jnp.max(jnp.sum(kf * kf, axis=-1, keepdims=True), axis=0, keepdims=True)
        kmax_ref[...] = jnp.broadcast_to(jnp.sqrt(k2), kmax_ref.shape)

    qs = jnp.concatenate([q_ref[:, j * AT_HD:(j + 1) * AT_HD] for j in range(grp)], axis=0)
    qf = qs.astype(F32)
    c = jnp.sqrt(jnp.sum(qf * qf, axis=-1, keepdims=True)) * kmax_ref[0:1, 0:1] * (1.0 + 2.0 ** -8)
    cmax = jnp.max(c)

    def finish(acc, l):
        o = acc * (1.0 / l)
        for j in range(grp):
            cols = slice(j * AT_HD, (j + 1) * AT_HD)
            o_ref[:, cols] = (o[j * tq:(j + 1) * tq, :] * _silu(gate_ref[:, cols])).astype(o_ref.dtype)

    @pl.when(2.0 * cmax <= AT_SAFE_LOG2_RANGE)
    def _():
        lvec = jnp.zeros((rows, LANES), F32)
        acc = jnp.zeros((rows, AT_HD), F32)
        for t in range(nkv):
            ks = k_ref[t * tk:(t + 1) * tk, :]
            vs = v_ref[t * tk:(t + 1) * tk, :]
            p = jnp.exp2(_dot_nt(qs, ks) - c)
            for w in range(tk // LANES):
                lvec = lvec + p[:, w * LANES:(w + 1) * LANES]
            acc = acc + _dot(p.astype(BF16), vs)
        finish(acc, jnp.sum(lvec, axis=-1, keepdims=True))

    @pl.when(2.0 * cmax > AT_SAFE_LOG2_RANGE)
    def _():
        def body(t, carry):
            m, l, acc = carry
            ks = k_ref[pl.ds(t * tk, tk), :]
            vs = v_ref[pl.ds(t * tk, tk), :]
            s = _dot_nt(qs, ks)
            m_new = jnp.maximum(m, jnp.max(s, axis=-1, keepdims=True))
            alpha = jnp.exp2(m - m_new)
            p = jnp.exp2(s - m_new)
            l = alpha * l + jnp.sum(p, axis=-1, keepdims=True)
            acc = alpha * acc + _dot(p.astype(BF16), vs)
            return m_new, l, acc

        init = (jnp.full((rows, 1), -jnp.inf, F32), jnp.zeros((rows, 1), F32), jnp.zeros((rows, AT_HD), F32))
        _, l, acc = lax.fori_loop(0, nkv, body, init)
        finish(acc, l)


def gqa_flash(q, k, v, u, batch, seq, *, tq=256, tk=512):
    n = q.shape[0]
    grp = AT_HEADS // AT_KV
    gw = grp * AT_HD
    nq = seq // tq
    gate_blk0 = (AT_HEADS + 2 * AT_KV) * AT_HD // gw
    return pl.pallas_call(
        functools.partial(_gqa_flash_kernel, tk=min(tk, seq)),
        grid=(batch, AT_KV, nq),
        in_specs=[pl.BlockSpec((tq, gw), lambda b, h, i: (b * nq + i, h)),
                  pl.BlockSpec((seq, AT_HD), lambda b, h, i: (b, h)),
                  pl.BlockSpec((seq, AT_HD), lambda b, h, i: (b, h)),
                  pl.BlockSpec((tq, gw), lambda b, h, i: (b * nq + i, gate_blk0 + h))],
        out_specs=pl.BlockSpec((tq, gw), lambda b, h, i: (b * nq + i, h)),
        out_shape=jax.ShapeDtypeStruct((n, AT_HEADS * AT_HD), BF16),
        scratch_shapes=[pltpu.VMEM((SUBLANES, LANES), F32)],
        compiler_params=_params("parallel", "parallel", "arbitrary"),
        name="gqa_flash",
    )(q, k, v, u)


def gqa_layer(x, g, w_in, q_g, k_g, w_out, batch, seq, final_g=None):
    u = proj_in(x, g, w_in.astype(BF16))
    q, k, v = gqa_prep(u, q_g, k_g, seq)
    o = gqa_flash(q, k, v, u, batch, seq)
    return proj_out(o, w_out.astype(BF16), x, final_g)


def _t5_bucket_np(rel):
    half = REL_BUCKETS // 2
    exact = half // 2
    nabs = np.abs(rel)
    large = exact + (np.log(np.maximum(nabs, 1).astype(np.float32) / exact)
                     / math.log(REL_MAX_DIST / exact) * (half - exact)).astype(np.int32)
    large = np.minimum(large, half - 1)
    return np.where(rel > 0, half, 0) + np.where(nabs < exact, nabs, large)


DL_TQ = 128
DL_HALO = 64
DL_NK = DL_TQ + 2 * DL_HALO


def _dilated_buckets():
    qi = np.arange(DL_TQ)[:, None]
    kj = np.arange(DL_NK)[None, :]
    tabs = [_t5_bucket_np((kj - DL_HALO - qi) * dil).astype(np.int32) for _, dil in DL_PAIRS]
    band = np.abs(kj - DL_HALO - qi) <= DL_HALO
    present = [sorted(set(t[band].tolist())) for t in tabs]
    return np.stack(tabs), present


def _dilated_kernel(relb_ref, bucket_ref, *refs, seq, tok, present):
    ng = len(DL_PAIRS)
    io = refs[:7 * ng]
    gate_ref, out_ref, bias_ref, kbuf, vbuf, obuf, lbuf = refs[7 * ng:]
    i = pl.program_id(1)
    p = pl.program_id(2)
    qi = lax.broadcasted_iota(jnp.int32, (DL_TQ, DL_NK), 0)
    kj = lax.broadcasted_iota(jnp.int32, (DL_TQ, DL_NK), 1)
    band = jnp.abs(kj - DL_HALO - qi) <= DL_HALO
    lane = lax.broadcasted_iota(jnp.int32, (DL_TQ, LANES), 1)
    lo = lane < DL_HD
    scale = DL_HD ** -0.5

    @pl.when((pl.program_id(0) == 0) & (i == 0) & (p == 0))
    def _():
        for g in range(ng):
            bk = bucket_ref[g]

            def fill(h, carry, g=g, bk=bk):
                acc = jnp.zeros((DL_TQ, DL_NK), F32)
                for b in present[g]:
                    acc = jnp.where(bk == b, relb_ref[b, h], acc)
                bias_ref[g, h] = jnp.where(band, acc, NEG_BIG)
                return carry

            lax.fori_loop(0, DL_HEADS, fill, 0)

    for g, (_, dil) in enumerate(DL_PAIRS):
        q_ref, kp_ref, kc_ref, kn_ref, vp_ref, vc_ref, vn_ref = io[7 * g:7 * g + 7]
        halo = DL_HALO * dil
        ls = seq // dil
        ppb = tok // dil
        nsub = ppb // DL_TQ
        kbuf[0:halo, :] = kp_ref[...]
        kbuf[halo:halo + tok, :] = kc_ref[...]
        kbuf[halo + tok:2 * halo + tok, :] = kn_ref[...]
        vbuf[0:halo, :] = vp_ref[...]
        vbuf[halo:halo + tok, :] = vc_ref[...]
        vbuf[halo + tok:2 * halo + tok, :] = vn_ref[...]

        def block(idx, carry, g=g, dil=dil, ls=ls, ppb=ppb, nsub=nsub, q_ref=q_ref):
            r = idx // nsub
            j = idx % nsub
            start = r + j * (DL_TQ * dil)
            if dil == 1:
                qrows, krows = pl.ds(start, DL_TQ), pl.ds(start, DL_NK)
            else:
                qrows, krows = pl.ds(start, DL_TQ, stride=dil), pl.ds(start, DL_NK, stride=dil)
            q2 = q_ref[qrows, :] * scale
            k2 = kbuf[krows, :].astype(BF16)
            v2 = vbuf[krows, :].astype(BF16)
            mk = i * ppb + j * DL_TQ - DL_HALO + kj
            inside = (mk >= 0) & (mk < ls)
            outs, lses = [], []
            for half in range(2):
                qh = jnp.where(lo if half == 0 else ~lo, q2, 0.0).astype(BF16)
                s = _dot_nt(qh, k2) + bias_ref[g, 2 * p + half]
                s = jnp.where(inside, s, NEG_BIG)
                m = jnp.max(s, axis=-1, keepdims=True)
                pexp = jnp.exp(s - m)
                l = jnp.sum(pexp, axis=-1, keepdims=True)
                outs.append(_dot(pexp.astype(BF16), v2) * (1.0 / l))
                lses.append(m + jnp.log(l))
            obuf[g, qrows, :] = jnp.where(lo, outs[0], outs[1])
            lbuf[g, qrows, :] = jnp.where(lo, lses[0], lses[1])
            return carry

        lax.fori_loop(0, dil * nsub, block, 0, unroll=8)

    la, lb, lc = lbuf[0], lbuf[1], lbuf[2]
    m = jnp.maximum(jnp.maximum(la, lb), lc)
    ea, eb, ec = jnp.exp(la - m), jnp.exp(lb - m), jnp.exp(lc - m)
    o = (ea * obuf[0] + eb * obuf[1] + ec * obuf[2]) / (ea + eb + ec)
    out_ref[...] = (o * _silu(gate_ref[...])).astype(out_ref.dtype)


def dilated_attention(u, rel_bias, batch, seq, *, tok=2048):
    n, win = u.shape
    hw = DL_HEADS * DL_HD
    npair = hw // LANES
    tok = min(tok, seq)
    nblk = seq // tok
    max_halo = DL_HALO * max(d for _, d in DL_PAIRS)
    assert tok % (DL_TQ * max(d for _, d in DL_PAIRS)) == 0 and tok % max_halo == 0
    buckets, present = _dilated_buckets()

    in_specs = [pl.BlockSpec(memory_space=pltpu.SMEM),
                pl.BlockSpec((len(DL_PAIRS), DL_TQ, DL_NK), lambda b, i, p: (0, 0, 0))]
    args = [rel_bias, jnp.asarray(buckets)]
    for gi, (_, dil) in enumerate(DL_PAIRS):
        halo = DL_HALO * dil
        hpb = tok // halo
        last = n // halo - 1

        def col(c, gi=gi):
            return lambda p: (gi * 3 + c) * npair + p

        def cur(c):
            cf = col(c)
            return pl.BlockSpec((tok, LANES), lambda b, i, p: (b * nblk + i, cf(p)))

        def prev(c, hpb=hpb):
            cf = col(c)
            return pl.BlockSpec((halo, LANES), lambda b, i, p: (jnp.maximum((b * nblk + i) * hpb - 1, 0), cf(p)))

        def nxt(c, hpb=hpb, last=last):
            cf = col(c)
            return pl.BlockSpec((halo, LANES),
                                lambda b, i, p: (jnp.minimum((b * nblk + i + 1) * hpb, last), cf(p)))

        in_specs += [cur(0), prev(1), cur(1), nxt(1), prev(2), cur(2), nxt(2)]
        args += [u] * 7
    gate_col0 = 3 * len(DL_PAIRS) * npair
    in_specs.append(pl.BlockSpec((tok, LANES), lambda b, i, p: (b * nblk + i, gate_col0 + p)))
    args.append(u)
    return pl.pallas_call(
        functools.partial(_dilated_kernel, seq=seq, tok=tok, present=present),
        grid=(batch, nblk, npair),
        in_specs=in_specs,
        out_specs=pl.BlockSpec((tok, LANES), lambda b, i, p: (b * nblk + i, p)),
        out_shape=jax.ShapeDtypeStruct((n, hw), BF16),
        scratch_shapes=[pltpu.VMEM((len(DL_PAIRS), DL_HEADS, DL_TQ, DL_NK), F32),
                        pltpu.VMEM((tok + 2 * max_halo, LANES), F32),
                        pltpu.VMEM((tok + 2 * max_halo, LANES), F32),
                        pltpu.VMEM((len(DL_PAIRS), tok, LANES), F32),
                        pltpu.VMEM((len(DL_PAIRS), tok, LANES), F32)],
        compiler_params=_params("arbitrary", "arbitrary", "arbitrary"),
        name="dilated_attention",
    )(*args)


def dilated_layer(x, g, rel_bias, w_in, w_out, batch, seq, final_g=None):
    u = proj_in(x, g, w_in.astype(BF16))
    o = dilated_attention(u, rel_bias, batch, seq)
    return proj_out(o, w_out.astype(BF16), x, final_g)


def kernel(x, norm_g, final_g, rel_bias, hgrn_lb, ssd_w_in, ssd_conv_w, ssd_conv_b, ssd_dt_bias, ssd_a_log, ssd_d,
           ssd_norm_g, ssd_w_out, hg_w_in, hg_norm_g, hg_w_out, at_w_in, at_q_norm_g, at_k_norm_g, at_w_out,
           dl_w_in, dl_w_out):
    batch, seq, d = x.shape
    depth = norm_g.shape[0]
    n_mixers = 4
    lb_sm = jax.nn.softmax(hgrn_lb.astype(F32), axis=0)
    lb_all = jnp.cumsum(lb_sm, axis=0) - lb_sm[0:1]
    h = x.reshape(batch * seq, d)
    for layer in range(depth):
        kind, slot = layer % n_mixers, layer // n_mixers
        fg = final_g if layer == depth - 1 else None
        if kind == 0:
            h = ssd_layer(h, norm_g[layer], ssd_w_in[slot], ssd_conv_w[slot], ssd_conv_b[slot], ssd_dt_bias[slot],
                          ssd_a_log[slot], ssd_d[slot], ssd_norm_g[slot], ssd_w_out[slot], batch, seq, fg)
        elif kind == 1:
            h = hgrn_layer(h, norm_g[layer], lb_all[layer], hg_w_in[slot], hg_norm_g[slot], hg_w_out[slot],
                           batch, seq, fg)
        elif kind == 2:
            h = gqa_layer(h, norm_g[layer], at_w_in[slot], at_q_norm_g[slot], at_k_norm_g[slot], at_w_out[slot],
                          batch, seq, fg)
        else:
            h = dilated_layer(h, norm_g[layer], rel_bias, dl_w_in[slot], dl_w_out[slot], batch, seq, fg)
    return h.reshape(batch, seq, d)
```

```python
import functools
import math

import jax
import jax.numpy as jnp
import numpy as np
from jax import lax
from jax.experimental import pallas as pl
from jax.experimental.pallas import tpu as pltpu

F32 = jnp.float32
BF16 = jnp.bfloat16

EPS = 1e-6
NEG_BIG = -1e30
GRID_W = 64
ROPE_THETA = 10000.0

SSD_HEADDIM = 64
SSD_HEADS = 32
SSD_GROUPS = 4
SSD_STATE = 128
SSD_CONV = 7
HG_HEADS = 8
HG_SUB = 32
AT_HEADS = 16
AT_KV = 8
AT_HD = 128
DL_PAIRS = ((128, 1), (512, 4), (2048, 16))
DL_HEADS = 16
DL_HD = 64
REL_BUCKETS = 32
REL_MAX_DIST = 1024

LANES = 128
SUBLANES = 8
CHUNK = 128
VMEM_LIMIT = 56 * 1024 * 1024


def _params(*sem):
    return pltpu.CompilerParams(dimension_semantics=sem, vmem_limit_bytes=VMEM_LIMIT)


def _sigmoid(x):
    return 0.5 * jnp.tanh(0.5 * x) + 0.5


def _silu(x):
    return x * _sigmoid(x)


def _softplus(x):
    return jnp.maximum(x, 0.0) + jnp.log(1.0 + jnp.exp(-jnp.abs(x)))


def _dot(a, b):
    return jnp.dot(a, b, preferred_element_type=F32)


def _dot_nt(a, b):
    return lax.dot_general(a, b, (((1,), (1,)), ((), ())), preferred_element_type=F32)


def _dot_tn(a, b):
    return lax.dot_general(a, b, (((0,), (0,)), ((), ())), preferred_element_type=F32)


def _prefix_sum(tri, x):
    x1 = x.astype(BF16)
    r1 = x - x1.astype(F32)
    x2 = r1.astype(BF16)
    x3 = (r1 - x2.astype(F32)).astype(BF16)
    return _dot(tri, x1) + _dot(tri, x2) + _dot(tri, x3)


def _proj_in_kernel(x_ref, g_ref, w_ref, o_ref, xn_ref):
    @pl.when(pl.program_id(1) == 0)
    def _():
        x = x_ref[...]
        ms = jnp.mean(x * x, axis=-1, keepdims=True)
        xn_ref[...] = (x * lax.rsqrt(ms + EPS) * g_ref[...]).astype(BF16)

    o_ref[...] = _dot(xn_ref[...], w_ref[...]).astype(o_ref.dtype)


def proj_in(x, g, w, *, tm=1024, tn=1024, out_dtype=F32):
    n, d = x.shape
    dout = w.shape[1]
    tm = min(tm, n)
    tn = min(tn, dout)
    assert n % tm == 0 and dout % tn == 0
    return pl.pallas_call(
        _proj_in_kernel,
        grid=(n // tm, dout // tn),
        in_specs=[pl.BlockSpec((tm, d), lambda i, j: (i, 0)),
                  pl.BlockSpec((1, d), lambda i, j: (0, 0)),
                  pl.BlockSpec((d, tn), lambda i, j: (0, j))],
        out_specs=pl.BlockSpec((tm, tn), lambda i, j: (i, j)),
        out_shape=jax.ShapeDtypeStruct((n, dout), out_dtype),
        scratch_shapes=[pltpu.VMEM((tm, d), BF16)],
        compiler_params=_params("parallel", "arbitrary"),
        name="proj_in",
    )(x, g.reshape(1, d), w)


def _proj_out_kernel(a_ref, w_ref, r_ref, g_ref, o_ref, *, final):
    y = r_ref[...] + _dot(a_ref[...], w_ref[...])
    if final:
        ms = jnp.mean(y * y, axis=-1, keepdims=True)
        y = y * lax.rsqrt(ms + EPS) * g_ref[...]
    o_ref[...] = y


def proj_out(a, w, res, final_g=None, *, tm=512):
    n, k = a.shape
    d = w.shape[1]
    tm = min(tm, n)
    assert n % tm == 0
    g = jnp.ones((1, d), F32) if final_g is None else final_g.reshape(1, d)
    return pl.pallas_call(
        functools.partial(_proj_out_kernel, final=final_g is not None),
        grid=(n // tm,),
        in_specs=[pl.BlockSpec((tm, k), lambda i: (i, 0)),
                  pl.BlockSpec((k, d), lambda i: (0, 0)),
                  pl.BlockSpec((tm, d), lambda i: (i, 0)),
                  pl.BlockSpec((1, d), lambda i: (0, 0))],
        out_specs=pl.BlockSpec((tm, d), lambda i: (i, 0)),
        out_shape=jax.ShapeDtypeStruct((n, d), F32),
        compiler_params=_params("parallel"),
        name="proj_out",
    )(a, w, res, g)


def _ssd_conv_kernel(xp_ref, xc_ref, xn_ref, bp_ref, bc_ref, bn_ref, wx_ref, wb_ref, bx_ref, bb_ref,
                     ox_ref, ob_ref, *, nblk):
    i = pl.program_id(1)
    pad = SSD_CONV // 2

    def conv(prev_ref, cur_ref, next_ref, w_ref, b_ref, o_ref):
        rows = cur_ref.shape[0]
        prev = jnp.where(i > 0, prev_ref[...], 0.0)
        nxt = jnp.where(i < nblk - 1, next_ref[...], 0.0)
        ext = jnp.concatenate([prev, cur_ref[...], nxt], axis=0)
        acc = jnp.zeros(cur_ref.shape, F32) + b_ref[...]
        for t in range(SSD_CONV):
            off = SUBLANES - pad + t
            acc = acc + ext[off:off + rows, :] * w_ref[t:t + 1, :]
        o_ref[...] = _silu(acc)

    conv(xp_ref, xc_ref, xn_ref, wx_ref, bx_ref, ox_ref)
    conv(bp_ref, bc_ref, bn_ref, wb_ref, bb_ref, ob_ref)


def ssd_conv(u, conv_w, conv_b, batch, seq, *, tc=256):
    n = u.shape[0]
    di = SSD_HEADS * SSD_HEADDIM
    gn2 = 2 * SSD_GROUPS * SSD_STATE
    tc = min(tc, seq)
    nblk = seq // tc
    r8 = tc // SUBLANES
    last8 = n // SUBLANES - 1

    def cur(wblk):
        return lambda b, i: (b * nblk + i, wblk)

    def prev(wblk):
        return lambda b, i: (jnp.maximum((b * nblk + i) * r8 - 1, 0), wblk)

    def nxt(wblk):
        return lambda b, i: (jnp.minimum((b * nblk + i + 1) * r8, last8), wblk)

    wx, wb = conv_w[:, :di], conv_w[:, di:]
    bx, bb = conv_b[:di].reshape(1, di), conv_b[di:].reshape(1, gn2)
    const = lambda b, i: (0, 0)
    return pl.pallas_call(
        functools.partial(_ssd_conv_kernel, nblk=nblk),
        grid=(batch, nblk),
        in_specs=[pl.BlockSpec((SUBLANES, di), prev(1)), pl.BlockSpec((tc, di), cur(1)),
                  pl.BlockSpec((SUBLANES, di), nxt(1)),
                  pl.BlockSpec((SUBLANES, gn2), prev(4)), pl.BlockSpec((tc, gn2), cur(4)),
                  pl.BlockSpec((SUBLANES, gn2), nxt(4)),
                  pl.BlockSpec((SSD_CONV, di), const), pl.BlockSpec((SSD_CONV, gn2), const),
                  pl.BlockSpec((1, di), const), pl.BlockSpec((1, gn2), const)],
        out_specs=[pl.BlockSpec((tc, di), lambda b, i: (b * nblk + i, 0)),
                   pl.BlockSpec((tc, gn2), lambda b, i: (b * nblk + i, 0))],
        out_shape=[jax.ShapeDtypeStruct((n, di), F32), jax.ShapeDtypeStruct((n, gn2), F32)],
        compiler_params=_params("parallel", "parallel"),
        name="ssd_conv",
    )(u, u, u, u, u, u, wx, wb, bx, bb)


def _ssd_scan_kernel(x_ref, bc_ref, dtr_ref, dtb_ref, alog_ref, *rest, reverse, final):
    if final:
        z_ref, yo_ref, dskip_ref, ng_ref, o_ref, st_ref = rest
    else:
        o_ref, st_ref = rest
    c = pl.program_id(1)
    nb = x_ref.shape[0]
    q = CHUNK
    gn = SSD_GROUPS * SSD_STATE
    hpg = SSD_HEADS // SSD_GROUPS
    gw = hpg * SSD_HEADDIM
    hoff = SSD_HEADS if reverse else 0
    far = 0 if reverse else q - 1

    @pl.when(c == 0)
    def _():
        st_ref[...] = jnp.zeros(st_ref.shape, F32)

    row = lax.broadcasted_iota(jnp.int32, (q, q), 0)
    col = lax.broadcasted_iota(jnp.int32, (q, q), 1)
    valid = (col >= row) if reverse else (col <= row)
    tri = valid.astype(BF16)
    lane = lax.broadcasted_iota(jnp.int32, (q, LANES), 1)
    lo = lane < SSD_HEADDIM
    lo_row = lo[0:1, :]
    neg_a = -jnp.exp(alog_ref[...])

    for bb in range(nb):
        x = x_ref[bb]
        dt = _softplus(dtr_ref[bb] + dtb_ref[...])
        cs_col = _prefix_sum(tri, dt * neg_a)
        cs_row = cs_col.T
        dt_row = dt.T
        w_row = jnp.exp(cs_row[:, far:far + 1] - cs_row) * dt_row
        dec = jnp.exp(cs_col[far:far + 1, :])
        for g in range(SSD_GROUPS):
            b_f = bc_ref[bb, :, g * SSD_STATE:(g + 1) * SSD_STATE]
            c_f = bc_ref[bb, :, gn + g * SSD_STATE:gn + (g + 1) * SSD_STATE]
            cb = _dot_nt(c_f.astype(BF16), b_f.astype(BF16))
            b_t = b_f.T
            y_parts = []
            for pp in range(hpg // 2):
                h0 = g * hpg + 2 * pp
                cols = slice(h0 * SSD_HEADDIM, (h0 + 2) * SSD_HEADDIM)
                scols = slice(2 * pp * SSD_HEADDIM, (2 * pp + 2) * SSD_HEADDIM)
                x2 = x[:, cols]
                x2b = x2.astype(BF16)
                s_old = st_ref[bb, g, :, scols]
                rhs = jnp.concatenate([x2b, s_old.astype(BF16)], axis=0)
                ys, sts = [], []
                for h in (h0, h0 + 1):
                    hl = hoff + h
                    colb = jnp.broadcast_to(cs_col[:, hl:hl + 1], (q, q))
                    lmat = jnp.exp(jnp.where(valid, colb - cs_row[hl:hl + 1, :], NEG_BIG))
                    m = cb * (lmat * dt_row[hl:hl + 1, :])
                    ce = c_f * jnp.exp(colb)
                    ys.append(_dot(jnp.concatenate([m, ce], axis=1).astype(BF16), rhs))
                    sts.append(_dot((b_t * w_row[hl:hl + 1, :]).astype(BF16), x2b))
                y2 = jnp.where(lo, ys[0], ys[1])
                dec2 = jnp.where(lo_row, dec[:, hoff + h0:hoff + h0 + 1], dec[:, hoff + h0 + 1:hoff + h0 + 2])
                st_ref[bb, g, :, scols] = s_old * dec2 + jnp.where(lo, sts[0], sts[1])
                if final:
                    y2 = y2 + yo_ref[bb, :, cols] + x2 * dskip_ref[:, cols]
                    y_parts.append(y2 * _silu(z_ref[bb, :, cols]))
                else:
                    o_ref[bb, :, cols] = y2
            if final:
                gcols = slice(g * gw, (g + 1) * gw)
                yg = jnp.concatenate(y_parts, axis=1)
                ms = jnp.mean(yg * yg, axis=-1, keepdims=True)
                o_ref[bb, :, gcols] = (yg * lax.rsqrt(ms + EPS) * ng_ref[:, gcols]).astype(o_ref.dtype)


def ssd_scan(xc, bc, dtr, dt_bias, a_log, batch, seq, *, reverse, final_args=None, nb=2):
    n, di = xc.shape
    nc = seq // CHUNK
    gn2 = bc.shape[1]
    final = final_args is not None
    nb = min(nb, batch)
    assert batch % nb == 0
    v3 = lambda a: a.reshape(batch, seq, a.shape[1])

    def chunk(c):
        return nc - 1 - c if reverse else c

    blk = lambda w: pl.BlockSpec((nb, CHUNK, w), lambda b, c: (b, chunk(c), 0))
    const = lambda w: pl.BlockSpec((1, w), lambda b, c: (0, 0))
    in_specs = [blk(di), blk(gn2), blk(LANES), const(LANES), const(LANES)]
    args = [v3(xc), v3(bc), v3(dtr), dt_bias, a_log]
    if final:
        u, y_other, dskip, ng = final_args
        in_specs += [blk(di), blk(di), const(di), const(di)]
        args += [v3(u), v3(y_other), dskip, ng]
    out = pl.pallas_call(
        functools.partial(_ssd_scan_kernel, reverse=reverse, final=final),
        grid=(batch // nb, nc),
        in_specs=in_specs,
        out_specs=blk(di),
        out_shape=jax.ShapeDtypeStruct((batch, seq, di), BF16 if final else F32),
        scratch_shapes=[pltpu.VMEM((nb, SSD_GROUPS, SSD_STATE, (SSD_HEADS // SSD_GROUPS) * SSD_HEADDIM), F32)],
        compiler_params=_params("parallel", "arbitrary"),
        name="ssd_scan_rev" if reverse else "ssd_scan_fwd",
    )(*args)
    return out.reshape(n, di)


def ssd_layer(x, g, w_in, conv_w, conv_b, dt_bias, a_log, d_skip, norm_g, w_out, batch, seq, final_g=None):
    di = SSD_HEADS * SSD_HEADDIM
    main = 2 * di + 2 * SSD_GROUPS * SSD_STATE
    w_main = w_in[:, :main].astype(BF16)
    w_dt = jnp.pad(w_in[:, main:], ((0, 0), (0, LANES - 2 * SSD_HEADS))).astype(BF16)
    u = proj_in(x, g, w_main)
    dtr = proj_in(x, g, w_dt)
    xc, bc = ssd_conv(u, conv_w, conv_b, batch, seq)
    pad = lambda v: jnp.pad(v.reshape(1, 2 * SSD_HEADS), ((0, 0), (0, LANES - 2 * SSD_HEADS)))
    dtb, alog = pad(dt_bias), pad(a_log)
    y_rev = ssd_scan(xc, bc, dtr, dtb, alog, batch, seq, reverse=True)
    dskip = jnp.repeat(d_skip, SSD_HEADDIM).reshape(1, di)
    y = ssd_scan(xc, bc, dtr, dtb, alog, batch, seq, reverse=False,
                 final_args=(u, y_rev, dskip, norm_g.reshape(1, di)))
    return proj_out(y, w_out.astype(BF16), x, final_g)


def _hgrn_scan_kernel(q_ref, f_ref, v_ref, lb_ref, *rest, reverse, final):
    if final:
        gate_ref, oo_ref, ng_ref, o_ref, st_ref = rest
    else:
        o_ref, st_ref = rest
    c = pl.program_id(1)
    n = CHUNK
    nsub = n // HG_SUB
    dk = LANES

    @pl.when(c == 0)
    def _():
        st_ref[...] = jnp.zeros(st_ref.shape, F32)

    nb = q_ref.shape[0]
    lb = lb_ref[...]
    row = lax.broadcasted_iota(jnp.int32, (n, n), 0)
    col = lax.broadcasted_iota(jnp.int32, (n, n), 1)
    valid = (col >= row) if reverse else (col <= row)
    tri = valid.astype(BF16)
    ref_row = HG_SUB // 2 - 1 if reverse else HG_SUB // 2
    far = 0 if reverse else n - 1

    for bb in range(nb):
        qa = _silu(q_ref[bb])
        f = lb + (1.0 - lb) * _sigmoid(f_ref[bb])
        ka = 1.0 - f
        gsum = _prefix_sum(tri, jnp.log(f))
        for h in range(HG_HEADS):
            cols = slice(h * dk, (h + 1) * dk)
            gh, qh, kh = gsum[:, cols], qa[:, cols], ka[:, cols]
            vh = v_ref[bb, :, cols].astype(BF16)
            tot = gh[far:far + 1, :]
            qp, kn, anchors = [], [], []
            for s in range(nsub):
                rs = slice(s * HG_SUB, (s + 1) * HG_SUB)
                a = gh[s * HG_SUB + ref_row:s * HG_SUB + ref_row + 1, :]
                anchors.append(a)
                qp.append(qh[rs] * jnp.exp(gh[rs] - a))
                kn.append(kh[rs] * jnp.exp(a - gh[rs]))
            att_cols = []
            for j in range(nsub):
                lhs = []
                for i in range(nsub):
                    live = (i <= j) if reverse else (i >= j)
                    if not live:
                        lhs.append(jnp.zeros((HG_SUB, dk), F32))
                    elif i == j:
                        lhs.append(qp[i])
                    else:
                        lhs.append(qp[i] * jnp.exp(anchors[i] - anchors[j]))
                att_cols.append(_dot_nt(jnp.concatenate(lhs, axis=0).astype(BF16), kn[j].astype(BF16)))
            att = jnp.where(valid, jnp.concatenate(att_cols, axis=1), 0.0)
            s_old = st_ref[bb, h]
            qe = jnp.concatenate([qp[s] * jnp.exp(anchors[s]) for s in range(nsub)], axis=0)
            kd = jnp.concatenate([kn[s] * jnp.exp(tot - anchors[s]) for s in range(nsub)], axis=0)
            o = _dot(att.astype(BF16), vh) + _dot_nt(qe.astype(BF16), s_old.astype(BF16))
            st_ref[bb, h] = s_old * jnp.exp(tot) + _dot_tn(vh, kd.astype(BF16))
            if final:
                o = o + oo_ref[bb, :, cols]
                ms = jnp.mean(o * o, axis=-1, keepdims=True)
                o = o * lax.rsqrt(ms + EPS) * ng_ref[:, cols] * _silu(gate_ref[bb, :, cols])
            o_ref[bb, :, cols] = o.astype(o_ref.dtype)


def hgrn_scan(u, lb, batch, seq, *, reverse, final_args=None, nb=2):
    n = u.shape[0]
    w = HG_HEADS * LANES
    nc = seq // CHUNK
    final = final_args is not None
    nb = min(nb, batch)
    assert batch % nb == 0
    v3 = lambda a: a.reshape(batch, seq, a.shape[1])

    def chunk(c):
        return nc - 1 - c if reverse else c

    ublk = lambda j: pl.BlockSpec((nb, CHUNK, w), lambda b, c: (b, chunk(c), j))
    const = pl.BlockSpec((1, w), lambda b, c: (0, 0))
    u3 = v3(u)
    in_specs = [ublk(0), ublk(2 if reverse else 1), ublk(3), const]
    args = [u3, u3, u3, lb.reshape(1, w)]
    if final:
        o_other, ng = final_args
        in_specs += [ublk(4), ublk(0), const]
        args += [u3, v3(o_other), ng.reshape(1, w)]
    out = pl.pallas_call(
        functools.partial(_hgrn_scan_kernel, reverse=reverse, final=final),
        grid=(batch // nb, nc),
        in_specs=in_specs,
        out_specs=ublk(0),
        out_shape=jax.ShapeDtypeStruct((batch, seq, w), BF16 if final else F32),
        scratch_shapes=[pltpu.VMEM((nb, HG_HEADS, LANES, LANES), F32)],
        compiler_params=_params("parallel", "arbitrary"),
        name="hgrn_scan_rev" if reverse else "hgrn_scan_fwd",
    )(*args)
    return out.reshape(n, w)


def hgrn_layer(x, g, lb, w_in, norm_g, w_out, batch, seq, final_g=None):
    u = proj_in(x, g, w_in.astype(BF16))
    o_rev = hgrn_scan(u, lb, batch, seq, reverse=True)
    o = hgrn_scan(u, lb, batch, seq, reverse=False, final_args=(o_rev, norm_g))
    return proj_out(o, w_out.astype(BF16), x, final_g)


def _rope_tables(seq):
    pos = np.arange(seq)
    rowp = (pos // GRID_W).astype(np.float64)
    colp = (pos % GRID_W).astype(np.float64)
    half = AT_HD // 4
    inv = ROPE_THETA ** (-np.arange(0, 2 * half, 2, dtype=np.float64) / (2 * half))
    ar, ac = rowp[:, None] * inv, colp[:, None] * inv
    cos = np.concatenate([np.cos(ar), np.cos(ac), np.cos(ar), np.cos(ac)], axis=1)
    sin = np.concatenate([-np.sin(ar), -np.sin(ac), np.sin(ar), np.sin(ac)], axis=1)
    return jnp.asarray(cos, F32), jnp.asarray(sin, F32)


def _pair_major(a, nheads):
    lead = a.shape[:-1]
    a = a.reshape(*lead, nheads, 2, 2, AT_HD // 4)
    return jnp.swapaxes(a, -3, -2).reshape(*lead, nheads * AT_HD)


def _gqa_proj_kernel(x_ref, g_ref, w_ref, cos_ref, sin_ref, qg_ref, kg_ref, qo_ref, ko_ref, vo_ref, go_ref, xn_ref,
                     *, tn):
    j = pl.program_id(1)
    jq = AT_HEADS * AT_HD // tn
    jk = jq + AT_KV * AT_HD // tn

    @pl.when(j == 0)
    def _():
        x = x_ref[...]
        ms = jnp.mean(x * x, axis=-1, keepdims=True)
        xn_ref[...] = (x * lax.rsqrt(ms + EPS) * g_ref[...]).astype(BF16)

    acc = _dot(xn_ref[...], w_ref[...])

    def heads(o_ref, gain_ref, scale):
        cos, sin = cos_ref[...], sin_ref[...]
        gain = gain_ref[...] * scale
        for h in range(tn // AT_HD):
            cols = slice(h * AT_HD, (h + 1) * AT_HD)
            xh = acc[:, cols]
            ms = jnp.mean(xh * xh, axis=-1, keepdims=True)
            xn = xh * lax.rsqrt(ms + EPS) * gain
            o_ref[:, cols] = (xn * cos + pltpu.roll(xn, AT_HD // 2, axis=1) * sin).astype(o_ref.dtype)

    @pl.when(j < jq)
    def _():
        heads(qo_ref, qg_ref, AT_HD ** -0.5 * math.log2(math.e))

    @pl.when((j >= jq) & (j < jk))
    def _():
        heads(ko_ref, kg_ref, 1.0)

    @pl.when((j >= jk) & (j < 2 * jk - jq))
    def _():
        vo_ref[...] = acc.astype(vo_ref.dtype)

    @pl.when(j >= 2 * jk - jq)
    def _():
        go_ref[...] = acc.astype(go_ref.dtype)


def gqa_proj(x, g, w, q_g, k_g, seq, *, tm=1024, tn=1024):
    n, d = x.shape
    qw, kw = AT_HEADS * AT_HD, AT_KV * AT_HD
    tm = min(tm, seq)
    assert w.shape[1] == 2 * qw + 2 * kw and kw == tn and qw % tn == 0 and seq % tm == 0
    nb = seq // tm
    jq = qw // tn
    jg = jq + 2
    cos, sin = _rope_tables(seq)
    pos = lambda i, j: (i % nb, 0)
    const = lambda i, j: (0, 0)
    blk = lambda f: pl.BlockSpec((tm, tn), f)
    return pl.pallas_call(
        functools.partial(_gqa_proj_kernel, tn=tn),
        grid=(n // tm, w.shape[1] // tn),
        in_specs=[pl.BlockSpec((tm, d), lambda i, j: (i, 0)), pl.BlockSpec((1, d), const),
                  pl.BlockSpec((d, tn), lambda i, j: (0, j)),
                  pl.BlockSpec((tm, AT_HD), pos), pl.BlockSpec((tm, AT_HD), pos),
                  pl.BlockSpec((1, AT_HD), const), pl.BlockSpec((1, AT_HD), const)],
        out_specs=[blk(lambda i, j: (i, jnp.minimum(j, jq - 1))), blk(lambda i, j: (i, 0)),
                   blk(lambda i, j: (i, 0)), blk(lambda i, j: (i, jnp.clip(j - jg, 0, jq - 1)))],
        out_shape=[jax.ShapeDtypeStruct((n, qw), BF16), jax.ShapeDtypeStruct((n, kw), BF16),
                   jax.ShapeDtypeStruct((n, kw), BF16), jax.ShapeDtypeStruct((n, qw), BF16)],
        scratch_shapes=[pltpu.VMEM((tm, d), BF16)],
        compiler_params=_params("parallel", "arbitrary"),
        name="gqa_proj",
    )(x, g.reshape(1, d), w, cos, sin, q_g.reshape(1, AT_HD), k_g.reshape(1, AT_HD))


AT_SAFE_LOG2_RANGE = 100.0


def _gqa_flash_kernel(q_ref, k_ref, v_ref, gate_ref, o_ref, kmax_ref, *, tk):
    tq = q_ref.shape[0]
    seq = k_ref.shape[0]
    grp = AT_HEADS // AT_KV
    rows = grp * tq
    nkv = seq // tk

    @pl.when(pl.program_id(2) == 0)
    def _():
        kf = k_ref[...].astype(F32)
        k2 = jnp.max(jnp.sum(kf * kf, axis=-1, keepdims=True), axis=0, keepdims=True)
        kmax_ref[...] = jnp.broadcast_to(jnp.sqrt(k2), kmax_ref.shape)

    qs = jnp.concatenate([q_ref[:, j * AT_HD:(j + 1) * AT_HD] for j in range(grp)], axis=0)
    qf = qs.astype(F32)
    c = jnp.sqrt(jnp.sum(qf * qf, axis=-1, keepdims=True)) * kmax_ref[0:1, 0:1] * (1.0 + 2.0 ** -8)
    cmax = jnp.max(c)

    def finish(acc, l):
        o = acc * (1.0 / l)
        for j in range(grp):
            cols = slice(j * AT_HD, (j + 1) * AT_HD)
            o_ref[:, cols] = (o[j * tq:(j + 1) * tq, :] * _silu(gate_ref[:, cols])).astype(o_ref.dtype)

    @pl.when(2.0 * cmax <= AT_SAFE_LOG2_RANGE)
    def _():
        lvec = jnp.zeros((rows, LANES), F32)
        acc = jnp.zeros((rows, AT_HD), F32)
        for t in range(nkv):
            ks = k_ref[t * tk:(t + 1) * tk, :]
            vs = v_ref[t * tk:(t + 1) * tk, :]
            p = jnp.exp2(_dot_nt(qs, ks) - c)
            for w in range(tk // LANES):
                lvec = lvec + p[:, w * LANES:(w + 1) * LANES]
            acc = acc + _dot(p.astype(BF16), vs)
        finish(acc, jnp.sum(lvec, axis=-1, keepdims=True))

    @pl.when(2.0 * cmax > AT_SAFE_LOG2_RANGE)
    def _():
        def body(t, carry):
            m, l, acc = carry
            ks = k_ref[pl.ds(t * tk, tk), :]
            vs = v_ref[pl.ds(t * tk, tk), :]
            s = _dot_nt(qs, ks)
            m_new = jnp.maximum(m, jnp.max(s, axis=-1, keepdims=True))
            alpha = jnp.exp2(m - m_new)
            p = jnp.exp2(s - m_new)
            l = alpha * l + jnp.sum(p, axis=-1, keepdims=True)
            acc = alpha * acc + _dot(p.astype(BF16), vs)
            return m_new, l, acc

        init = (jnp.full((rows, 1), -jnp.inf, F32), jnp.zeros((rows, 1), F32), jnp.zeros((rows, AT_HD), F32))
        _, l, acc = lax.fori_loop(0, nkv, body, init)
        finish(acc, l)


def gqa_flash(q, k, v, gate, batch, seq, *, tq=512, tk=512):
    n = q.shape[0]
    grp = AT_HEADS // AT_KV
    gw = grp * AT_HD
    tq = min(tq, seq)
    nq = seq // tq
    return pl.pallas_call(
        functools.partial(_gqa_flash_kernel, tk=min(tk, seq)),
        grid=(batch, AT_KV, nq),
        in_specs=[pl.BlockSpec((tq, gw), lambda b, h, i: (b * nq + i, h)),
                  pl.BlockSpec((seq, AT_HD), lambda b, h, i: (b, h)),
                  pl.BlockSpec((seq, AT_HD), lambda b, h, i: (b, h)),
                  pl.BlockSpec((tq, gw), lambda b, h, i: (b * nq + i, h))],
        out_specs=pl.BlockSpec((tq, gw), lambda b, h, i: (b * nq + i, h)),
        out_shape=jax.ShapeDtypeStruct((n, AT_HEADS * AT_HD), BF16),
        scratch_shapes=[pltpu.VMEM((SUBLANES, LANES), F32)],
        compiler_params=_params("parallel", "parallel", "arbitrary"),
        name="gqa_flash",
    )(q, k, v, gate)


def gqa_layer(x, g, w_in, q_g, k_g, w_out, batch, seq, final_g=None):
    nqk = AT_HEADS + AT_KV
    w = jnp.concatenate([_pair_major(w_in[:, :nqk * AT_HD], nqk), w_in[:, nqk * AT_HD:]], axis=1).astype(BF16)
    q, k, v, gate = gqa_proj(x, g, w, _pair_major(q_g, 1), _pair_major(k_g, 1), seq)
    o = gqa_flash(q, k, v, gate, batch, seq)
    return proj_out(o, w_out.astype(BF16), x, final_g)


def _t5_bucket_np(rel):
    half = REL_BUCKETS // 2
    exact = half // 2
    nabs = np.abs(rel)
    large = exact + (np.log(np.maximum(nabs, 1).astype(np.float32) / exact)
                     / math.log(REL_MAX_DIST / exact) * (half - exact)).astype(np.int32)
    large = np.minimum(large, half - 1)
    return np.where(rel > 0, half, 0) + np.where(nabs < exact, nabs, large)


DL_TQ = 128
DL_HALO = 64
DL_NK = DL_TQ + 2 * DL_HALO


def _dilated_buckets():
    qi = np.arange(DL_TQ)[:, None]
    kj = np.arange(DL_NK)[None, :]
    tabs = [_t5_bucket_np((kj - DL_HALO - qi) * dil).astype(np.int32) for _, dil in DL_PAIRS]
    band = np.abs(kj - DL_HALO - qi) <= DL_HALO
    present = [sorted(set(t[band].tolist())) for t in tabs]
    return np.stack(tabs), present


def _dilated_kernel(relb_ref, bucket_ref, *refs, seq, tok, present):
    ng = len(DL_PAIRS)
    io = refs[:7 * ng]
    gate_ref, out_ref, bias_ref, kbuf, vbuf, obuf, lbuf = refs[7 * ng:]
    i = pl.program_id(1)
    p = pl.program_id(2)
    qi = lax.broadcasted_iota(jnp.int32, (DL_TQ, DL_NK), 0)
    kj = lax.broadcasted_iota(jnp.int32, (DL_TQ, DL_NK), 1)
    band = jnp.abs(kj - DL_HALO - qi) <= DL_HALO
    lane = lax.broadcasted_iota(jnp.int32, (DL_TQ, LANES), 1)
    lo = lane < DL_HD
    scale = DL_HD ** -0.5

    @pl.when((pl.program_id(0) == 0) & (i == 0) & (p == 0))
    def _():
        for g in range(ng):
            bk = bucket_ref[g]

            def fill(h, carry, g=g, bk=bk):
                acc = jnp.zeros((DL_TQ, DL_NK), F32)
                for b in present[g]:
                    acc = jnp.where(bk == b, relb_ref[b, h], acc)
                bias_ref[g, h] = jnp.where(band, acc, NEG_BIG)
                return carry

            lax.fori_loop(0, DL_HEADS, fill, 0)

    for g, (_, dil) in enumerate(DL_PAIRS):
        q_ref, kp_ref, kc_ref, kn_ref, vp_ref, vc_ref, vn_ref = io[7 * g:7 * g + 7]
        halo = DL_HALO * dil
        ls = seq // dil
        ppb = tok // dil
        nsub = ppb // DL_TQ
        kbuf[0:halo, :] = kp_ref[...]
        kbuf[halo:halo + tok, :] = kc_ref[...]
        kbuf[halo + tok:2 * halo + tok, :] = kn_ref[...]
        vbuf[0:halo, :] = vp_ref[...]
        vbuf[halo:halo + tok, :] = vc_ref[...]
        vbuf[halo + tok:2 * halo + tok, :] = vn_ref[...]

        def block(idx, carry, g=g, dil=dil, ls=ls, ppb=ppb, nsub=nsub, q_ref=q_ref):
            r = idx // nsub
            j = idx % nsub
            start = r + j * (DL_TQ * dil)
            if dil == 1:
                qrows, krows = pl.ds(start, DL_TQ), pl.ds(start, DL_NK)
            else:
                qrows, krows = pl.ds(start, DL_TQ, stride=dil), pl.ds(start, DL_NK, stride=dil)
            q2 = q_ref[qrows, :] * scale
            k2 = kbuf[krows, :].astype(BF16)
            v2 = vbuf[krows, :].astype(BF16)
            mk = i * ppb + j * DL_TQ - DL_HALO + kj
            inside = (mk >= 0) & (mk < ls)
            outs, lses = [], []
            for half in range(2):
                qh = jnp.where(lo if half == 0 else ~lo, q2, 0.0).astype(BF16)
                s = _dot_nt(qh, k2) + bias_ref[g, 2 * p + half]
                s = jnp.where(inside, s, NEG_BIG)
                m = jnp.max(s, axis=-1, keepdims=True)
                pexp = jnp.exp(s - m)
                l = jnp.sum(pexp, axis=-1, keepdims=True)
                outs.append(_dot(pexp.astype(BF16), v2) * (1.0 / l))
                lses.append(m + jnp.log(l))
            obuf[g, qrows, :] = jnp.where(lo, outs[0], outs[1])
            lbuf[g, qrows, :] = jnp.where(lo, lses[0], lses[1])
            return carry

        lax.fori_loop(0, dil * nsub, block, 0, unroll=8)

    la, lb, lc = lbuf[0], lbuf[1], lbuf[2]
    m = jnp.maximum(jnp.maximum(la, lb), lc)
    ea, eb, ec = jnp.exp(la - m), jnp.exp(lb - m), jnp.exp(lc - m)
    o = (ea * obuf[0] + eb * obuf[1] + ec * obuf[2]) / (ea + eb + ec)
    out_ref[...] = (o * _silu(gate_ref[...])).astype(out_ref.dtype)


def dilated_attention(u, rel_bias, batch, seq, *, tok=2048):
    n, win = u.shape
    hw = DL_HEADS * DL_HD
    npair = hw // LANES
    tok = min(tok, seq)
    nblk = seq // tok
    max_halo = DL_HALO * max(d for _, d in DL_PAIRS)
    assert tok % (DL_TQ * max(d for _, d in DL_PAIRS)) == 0 and tok % max_halo == 0
    buckets, present = _dilated_buckets()

    in_specs = [pl.BlockSpec(memory_space=pltpu.SMEM),
                pl.BlockSpec((len(DL_PAIRS), DL_TQ, DL_NK), lambda b, i, p: (0, 0, 0))]
    args = [rel_bias, jnp.asarray(buckets)]
    for gi, (_, dil) in enumerate(DL_PAIRS):
        halo = DL_HALO * dil
        hpb = tok // halo
        last = n // halo - 1

        def col(c, gi=gi):
            return lambda p: (gi * 3 + c) * npair + p

        def cur(c):
            cf = col(c)
            return pl.BlockSpec((tok, LANES), lambda b, i, p: (b * nblk + i, cf(p)))

        def prev(c, hpb=hpb):
            cf = col(c)
            return pl.BlockSpec((halo, LANES), lambda b, i, p: (jnp.maximum((b * nblk + i) * hpb - 1, 0), cf(p)))

        def nxt(c, hpb=hpb, last=last):
            cf = col(c)
            return pl.BlockSpec((halo, LANES),
                                lambda b, i, p: (jnp.minimum((b * nblk + i + 1) * hpb, last), cf(p)))

        in_specs += [cur(0), prev(1), cur(1), nxt(1), prev(2), cur(2), nxt(2)]
        args += [u] * 7
    gate_col0 = 3 * len(DL_PAIRS) * npair
    in_specs.append(pl.BlockSpec((tok, LANES), lambda b, i, p: (b * nblk + i, gate_col0 + p)))
    args.append(u)
    return pl.pallas_call(
        functools.partial(_dilated_kernel, seq=seq, tok=tok, present=present),
        grid=(batch, nblk, npair),
        in_specs=in_specs,
        out_specs=pl.BlockSpec((tok, LANES), lambda b, i, p: (b * nblk + i, p)),
        out_shape=jax.ShapeDtypeStruct((n, hw), BF16),
        scratch_shapes=[pltpu.VMEM((len(DL_PAIRS), DL_HEADS, DL_TQ, DL_NK), F32),
                        pltpu.VMEM((tok + 2 * max_halo, LANES), F32),
                        pltpu.VMEM((tok + 2 * max_halo, LANES), F32),
                        pltpu.VMEM((len(DL_PAIRS), tok, LANES), F32),
                        pltpu.VMEM((len(DL_PAIRS), tok, LANES), F32)],
        compiler_params=_params("arbitrary", "arbitrary", "arbitrary"),
        name="dilated_attention",
    )(*args)


def dilated_layer(x, g, rel_bias, w_in, w_out, batch, seq, final_g=None):
    u = proj_in(x, g, w_in.astype(BF16))
    o = dilated_attention(u, rel_bias, batch, seq)
    return proj_out(o, w_out.astype(BF16), x, final_g)


def kernel(x, norm_g, final_g, rel_bias, hgrn_lb, ssd_w_in, ssd_conv_w, ssd_conv_b, ssd_dt_bias, ssd_a_log, ssd_d,
           ssd_norm_g, ssd_w_out, hg_w_in, hg_norm_g, hg_w_out, at_w_in, at_q_norm_g, at_k_norm_g, at_w_out,
           dl_w_in, dl_w_out):
    batch, seq, d = x.shape
    depth = norm_g.shape[0]
    n_mixers = 4
    lb_sm = jax.nn.softmax(hgrn_lb.astype(F32), axis=0)
    lb_all = jnp.cumsum(lb_sm, axis=0) - lb_sm[0:1]
    h = x.reshape(batch * seq, d)
    for layer in range(depth):
        kind, slot = layer % n_mixers, layer // n_mixers
        fg = final_g if layer == depth - 1 else None
        if kind == 0:
            h = ssd_layer(h, norm_g[layer], ssd_w_in[slot], ssd_conv_w[slot], ssd_conv_b[slot], ssd_dt_bias[slot],
                          ssd_a_log[slot], ssd_d[slot], ssd_norm_g[slot], ssd_w_out[slot], batch, seq, fg)
        elif kind == 1:
            h = hgrn_layer(h, norm_g[layer], lb_all[layer], hg_w_in[slot], hg_norm_g[slot], hg_w_out[slot],
                           batch, seq, fg)
        elif kind == 2:
            h = gqa_layer(h, norm_g[layer], at_w_in[slot], at_q_norm_g[slot], at_k_norm_g[slot], at_w_out[slot],
                          batch, seq, fg)
        else:
            h = dilated_layer(h, norm_g[layer], rel_bias, dl_w_in[slot], dl_w_out[slot], batch, seq, fg)
    return h.reshape(batch, seq, d)
```

```python
import functools
import math

import jax
import jax.numpy as jnp
import numpy as np
from jax import lax
from jax.experimental import pallas as pl
from jax.experimental.pallas import tpu as pltpu

F32 = jnp.float32
BF16 = jnp.bfloat16

EPS = 1e-6
NEG_BIG = -1e30
GRID_W = 64
ROPE_THETA = 10000.0

SSD_HEADDIM = 64
SSD_HEADS = 32
SSD_GROUPS = 4
SSD_STATE = 128
SSD_CONV = 7
HG_HEADS = 8
HG_SUB = 32
AT_HEADS = 16
AT_KV = 8
AT_HD = 128
DL_PAIRS = ((128, 1), (512, 4), (2048, 16))
DL_HEADS = 16
DL_HD = 64
REL_BUCKETS = 32
REL_MAX_DIST = 1024

LANES = 128
SUBLANES = 8
CHUNK = 128
VMEM_LIMIT = 56 * 1024 * 1024


def _params(*sem):
    return pltpu.CompilerParams(dimension_semantics=sem, vmem_limit_bytes=VMEM_LIMIT)


def _sigmoid(x):
    return 0.5 * jnp.tanh(0.5 * x) + 0.5


def _silu(x):
    return x * _sigmoid(x)


def _softplus(x):
    return jnp.maximum(x, 0.0) + jnp.log(1.0 + jnp.exp(-jnp.abs(x)))


def _dot(a, b):
    return jnp.dot(a, b, preferred_element_type=F32)


def _dot_nt(a, b):
    return lax.dot_general(a, b, (((1,), (1,)), ((), ())), preferred_element_type=F32)


def _dot_tn(a, b):
    return lax.dot_general(a, b, (((0,), (0,)), ((), ())), preferred_element_type=F32)


def _prefix_sum(tri, x):
    x1 = x.astype(BF16)
    r1 = x - x1.astype(F32)
    x2 = r1.astype(BF16)
    x3 = (r1 - x2.astype(F32)).astype(BF16)
    return _dot(tri, x1) + _dot(tri, x2) + _dot(tri, x3)


def _proj_in_kernel(x_ref, g_ref, w_ref, o_ref, xn_ref):
    @pl.when(pl.program_id(1) == 0)
    def _():
        x = x_ref[...]
        ms = jnp.mean(x * x, axis=-1, keepdims=True)
        xn_ref[...] = (x * lax.rsqrt(ms + EPS) * g_ref[...]).astype(BF16)

    o_ref[...] = _dot(xn_ref[...], w_ref[...]).astype(o_ref.dtype)


def proj_in(x, g, w, *, tm=1024, tn=1024, out_dtype=F32):
    n, d = x.shape
    dout = w.shape[1]
    tm = min(tm, n)
    tn = min(tn, dout)
    assert n % tm == 0 and dout % tn == 0
    return pl.pallas_call(
        _proj_in_kernel,
        grid=(n // tm, dout // tn),
        in_specs=[pl.BlockSpec((tm, d), lambda i, j: (i, 0)),
                  pl.BlockSpec((1, d), lambda i, j: (0, 0)),
                  pl.BlockSpec((d, tn), lambda i, j: (0, j))],
        out_specs=pl.BlockSpec((tm, tn), lambda i, j: (i, j)),
        out_shape=jax.ShapeDtypeStruct((n, dout), out_dtype),
        scratch_shapes=[pltpu.VMEM((tm, d), BF16)],
        compiler_params=_params("parallel", "arbitrary"),
        name="proj_in",
    )(x, g.reshape(1, d), w)


def _proj_out_kernel(a_ref, w_ref, r_ref, g_ref, o_ref, *, final):
    y = r_ref[...] + _dot(a_ref[...], w_ref[...])
    if final:
        ms = jnp.mean(y * y, axis=-1, keepdims=True)
        y = y * lax.rsqrt(ms + EPS) * g_ref[...]
    o_ref[...] = y


def proj_out(a, w, res, final_g=None, *, tm=512):
    n, k = a.shape
    d = w.shape[1]
    tm = min(tm, n)
    assert n % tm == 0
    g = jnp.ones((1, d), F32) if final_g is None else final_g.reshape(1, d)
    return pl.pallas_call(
        functools.partial(_proj_out_kernel, final=final_g is not None),
        grid=(n // tm,),
        in_specs=[pl.BlockSpec((tm, k), lambda i: (i, 0)),
                  pl.BlockSpec((k, d), lambda i: (0, 0)),
                  pl.BlockSpec((tm, d), lambda i: (i, 0)),
                  pl.BlockSpec((1, d), lambda i: (0, 0))],
        out_specs=pl.BlockSpec((tm, d), lambda i: (i, 0)),
        out_shape=jax.ShapeDtypeStruct((n, d), F32),
        compiler_params=_params("parallel"),
        name="proj_out",
    )(a, w, res, g)


def _ssd_conv_kernel(xp_ref, xc_ref, xn_ref, bp_ref, bc_ref, bn_ref, wx_ref, wb_ref, bx_ref, bb_ref,
                     ox_ref, ob_ref, *, nblk):
    i = pl.program_id(1)
    pad = SSD_CONV // 2

    def conv(prev_ref, cur_ref, next_ref, w_ref, b_ref, o_ref):
        rows = cur_ref.shape[0]
        prev = jnp.where(i > 0, prev_ref[...], 0.0)
        nxt = jnp.where(i < nblk - 1, next_ref[...], 0.0)
        ext = jnp.concatenate([prev, cur_ref[...], nxt], axis=0)
        acc = jnp.zeros(cur_ref.shape, F32) + b_ref[...]
        for t in range(SSD_CONV):
            off = SUBLANES - pad + t
            acc = acc + ext[off:off + rows, :] * w_ref[t:t + 1, :]
        o_ref[...] = _silu(acc)

    conv(xp_ref, xc_ref, xn_ref, wx_ref, bx_ref, ox_ref)
    conv(bp_ref, bc_ref, bn_ref, wb_ref, bb_ref, ob_ref)


def ssd_conv(u, conv_w, conv_b, batch, seq, *, tc=256):
    n = u.shape[0]
    di = SSD_HEADS * SSD_HEADDIM
    gn2 = 2 * SSD_GROUPS * SSD_STATE
    tc = min(tc, seq)
    nblk = seq // tc
    r8 = tc // SUBLANES
    last8 = n // SUBLANES - 1

    def cur(wblk):
        return lambda b, i: (b * nblk + i, wblk)

    def prev(wblk):
        return lambda b, i: (jnp.maximum((b * nblk + i) * r8 - 1, 0), wblk)

    def nxt(wblk):
        return lambda b, i: (jnp.minimum((b * nblk + i + 1) * r8, last8), wblk)

    wx, wb = conv_w[:, :di], conv_w[:, di:]
    bx, bb = conv_b[:di].reshape(1, di), conv_b[di:].reshape(1, gn2)
    const = lambda b, i: (0, 0)
    return pl.pallas_call(
        functools.partial(_ssd_conv_kernel, nblk=nblk),
        grid=(batch, nblk),
        in_specs=[pl.BlockSpec((SUBLANES, di), prev(1)), pl.BlockSpec((tc, di), cur(1)),
                  pl.BlockSpec((SUBLANES, di), nxt(1)),
                  pl.BlockSpec((SUBLANES, gn2), prev(4)), pl.BlockSpec((tc, gn2), cur(4)),
                  pl.BlockSpec((SUBLANES, gn2), nxt(4)),
                  pl.BlockSpec((SSD_CONV, di), const), pl.BlockSpec((SSD_CONV, gn2), const),
                  pl.BlockSpec((1, di), const), pl.BlockSpec((1, gn2), const)],
        out_specs=[pl.BlockSpec((tc, di), lambda b, i: (b * nblk + i, 0)),
                   pl.BlockSpec((tc, gn2), lambda b, i: (b * nblk + i, 0))],
        out_shape=[jax.ShapeDtypeStruct((n, di), F32), jax.ShapeDtypeStruct((n, gn2), F32)],
        compiler_params=_params("parallel", "parallel"),
        name="ssd_conv",
    )(u, u, u, u, u, u, wx, wb, bx, bb)


def _ssd_scan_kernel(x_ref, bc_ref, dtr_ref, dtb_ref, alog_ref, *rest, reverse, final):
    if final:
        z_ref, yo_ref, dskip_ref, ng_ref, o_ref, st_ref = rest
    else:
        o_ref, st_ref = rest
    c = pl.program_id(1)
    nb = x_ref.shape[0]
    q = CHUNK
    gn = SSD_GROUPS * SSD_STATE
    hpg = SSD_HEADS // SSD_GROUPS
    gw = hpg * SSD_HEADDIM
    hoff = SSD_HEADS if reverse else 0
    far = 0 if reverse else q - 1

    @pl.when(c == 0)
    def _():
        st_ref[...] = jnp.zeros(st_ref.shape, F32)

    row = lax.broadcasted_iota(jnp.int32, (q, q), 0)
    col = lax.broadcasted_iota(jnp.int32, (q, q), 1)
    valid = (col >= row) if reverse else (col <= row)
    tri = valid.astype(BF16)
    lane = lax.broadcasted_iota(jnp.int32, (q, LANES), 1)
    lo = lane < SSD_HEADDIM
    lo_row = lo[0:1, :]
    neg_a = -jnp.exp(alog_ref[...])

    prep = []
    for bb in range(nb):
        dt = _softplus(dtr_ref[bb] + dtb_ref[...])
        cs_col = _prefix_sum(tri, dt * neg_a)
        cs_row = cs_col.T
        dt_row = dt.T
        w_row = jnp.exp(cs_row[:, far:far + 1] - cs_row) * dt_row
        dec = jnp.exp(cs_col[far:far + 1, :])
        prep.append((cs_col, cs_row, dt_row, w_row, dec))

    for g in range(SSD_GROUPS):
        grp = []
        for bb in range(nb):
            b_f = bc_ref[bb, :, g * SSD_STATE:(g + 1) * SSD_STATE]
            c_f = bc_ref[bb, :, gn + g * SSD_STATE:gn + (g + 1) * SSD_STATE]
            grp.append((_dot_nt(c_f.astype(BF16), b_f.astype(BF16)), c_f, b_f.T))
        y_parts = [[] for _ in range(nb)]
        for pp in range(hpg // 2):
            h0 = g * hpg + 2 * pp
            cols = slice(h0 * SSD_HEADDIM, (h0 + 2) * SSD_HEADDIM)
            scols = slice(2 * pp * SSD_HEADDIM, (2 * pp + 2) * SSD_HEADDIM)
            ops = []
            for bb in range(nb):
                cs_col, cs_row, dt_row, w_row, dec = prep[bb]
                cb, c_f, b_t = grp[bb]
                x2 = x_ref[bb, :, cols]
                x2b = x2.astype(BF16)
                s_old = st_ref[bb, g, :, scols]
                rhs = jnp.concatenate([x2b, s_old.astype(BF16)], axis=0)
                lhs, bws = [], []
                for h in (h0, h0 + 1):
                    hl = hoff + h
                    colb = jnp.broadcast_to(cs_col[:, hl:hl + 1], (q, q))
                    lmat = jnp.exp(jnp.where(valid, colb - cs_row[hl:hl + 1, :], NEG_BIG))
                    m = cb * (lmat * dt_row[hl:hl + 1, :])
                    ce = c_f * jnp.exp(colb)
                    lhs.append(jnp.concatenate([m, ce], axis=1).astype(BF16))
                    bws.append((b_t * w_row[hl:hl + 1, :]).astype(BF16))
                dec2 = jnp.where(lo_row, dec[:, hoff + h0:hoff + h0 + 1], dec[:, hoff + h0 + 1:hoff + h0 + 2])
                ops.append((x2, x2b, s_old, rhs, lhs, bws, dec2))
            prods = []
            for bb in range(nb):
                _, x2b, _, rhs, lhs, bws, _ = ops[bb]
                prods.append(([_dot(l, rhs) for l in lhs], [_dot(w, x2b) for w in bws]))
            for bb in range(nb):
                x2, _, s_old, _, _, _, dec2 = ops[bb]
                ys, sts = prods[bb]
                y2 = jnp.where(lo, ys[0], ys[1])
                st_ref[bb, g, :, scols] = s_old * dec2 + jnp.where(lo, sts[0], sts[1])
                if final:
                    y2 = y2 + yo_ref[bb, :, cols] + x2 * dskip_ref[:, cols]
                    y_parts[bb].append(y2 * _silu(z_ref[bb, :, cols]))
                else:
                    o_ref[bb, :, cols] = y2
        if final:
            gcols = slice(g * gw, (g + 1) * gw)
            for bb in range(nb):
                yg = jnp.concatenate(y_parts[bb], axis=1)
                ms = jnp.mean(yg * yg, axis=-1, keepdims=True)
                o_ref[bb, :, gcols] = (yg * lax.rsqrt(ms + EPS) * ng_ref[:, gcols]).astype(o_ref.dtype)


def ssd_scan(xc, bc, dtr, dt_bias, a_log, batch, seq, *, reverse, final_args=None, nb=4):
    n, di = xc.shape
    nc = seq // CHUNK
    gn2 = bc.shape[1]
    final = final_args is not None
    nb = min(nb, batch)
    assert batch % nb == 0
    v3 = lambda a: a.reshape(batch, seq, a.shape[1])

    def chunk(c):
        return nc - 1 - c if reverse else c

    blk = lambda w: pl.BlockSpec((nb, CHUNK, w), lambda b, c: (b, chunk(c), 0))
    const = lambda w: pl.BlockSpec((1, w), lambda b, c: (0, 0))
    in_specs = [blk(di), blk(gn2), blk(LANES), const(LANES), const(LANES)]
    args = [v3(xc), v3(bc), v3(dtr), dt_bias, a_log]
    if final:
        u, y_other, dskip, ng = final_args
        in_specs += [blk(di), blk(di), const(di), const(di)]
        args += [v3(u), v3(y_other), dskip, ng]
    out = pl.pallas_call(
        functools.partial(_ssd_scan_kernel, reverse=reverse, final=final),
        grid=(batch // nb, nc),
        in_specs=in_specs,
        out_specs=blk(di),
        out_shape=jax.ShapeDtypeStruct((batch, seq, di), BF16 if final else F32),
        scratch_shapes=[pltpu.VMEM((nb, SSD_GROUPS, SSD_STATE, (SSD_HEADS // SSD_GROUPS) * SSD_HEADDIM), F32)],
        compiler_params=_params("parallel", "arbitrary"),
        name="ssd_scan_rev" if reverse else "ssd_scan_fwd",
    )(*args)
    return out.reshape(n, di)


def ssd_layer(x, g, w_in, conv_w, conv_b, dt_bias, a_log, d_skip, norm_g, w_out, batch, seq, final_g=None):
    di = SSD_HEADS * SSD_HEADDIM
    main = 2 * di + 2 * SSD_GROUPS * SSD_STATE
    w_main = w_in[:, :main].astype(BF16)
    w_dt = jnp.pad(w_in[:, main:], ((0, 0), (0, LANES - 2 * SSD_HEADS))).astype(BF16)
    u = proj_in(x, g, w_main)
    dtr = proj_in(x, g, w_dt)
    xc, bc = ssd_conv(u, conv_w, conv_b, batch, seq)
    pad = lambda v: jnp.pad(v.reshape(1, 2 * SSD_HEADS), ((0, 0), (0, LANES - 2 * SSD_HEADS)))
    dtb, alog = pad(dt_bias), pad(a_log)
    y_rev = ssd_scan(xc, bc, dtr, dtb, alog, batch, seq, reverse=True)
    dskip = jnp.repeat(d_skip, SSD_HEADDIM).reshape(1, di)
    y = ssd_scan(xc, bc, dtr, dtb, alog, batch, seq, reverse=False,
                 final_args=(u, y_rev, dskip, norm_g.reshape(1, di)))
    return proj_out(y, w_out.astype(BF16), x, final_g)


def _hgrn_scan_kernel(q_ref, f_ref, v_ref, lb_ref, *rest, reverse, final):
    if final:
        gate_ref, oo_ref, ng_ref, o_ref, st_ref = rest
    else:
        o_ref, st_ref = rest
    c = pl.program_id(1)
    n = CHUNK
    nsub = n // HG_SUB
    dk = LANES

    @pl.when(c == 0)
    def _():
        st_ref[...] = jnp.zeros(st_ref.shape, F32)

    nb = q_ref.shape[0]
    lb = lb_ref[...]
    row = lax.broadcasted_iota(jnp.int32, (n, n), 0)
    col = lax.broadcasted_iota(jnp.int32, (n, n), 1)
    valid = (col >= row) if reverse else (col <= row)
    tri = valid.astype(BF16)
    ref_row = HG_SUB // 2 - 1 if reverse else HG_SUB // 2
    far = 0 if reverse else n - 1

    prep = []
    for bb in range(nb):
        f = lb + (1.0 - lb) * _sigmoid(f_ref[bb])
        prep.append((_silu(q_ref[bb]), 1.0 - f, _prefix_sum(tri, jnp.log(f))))

    for h in range(HG_HEADS):
        cols = slice(h * dk, (h + 1) * dk)
        staged = []
        for bb in range(nb):
            qa, ka, gsum = prep[bb]
            gh, qh, kh = gsum[:, cols], qa[:, cols], ka[:, cols]
            tot = gh[far:far + 1, :]
            qp, kn, anchors = [], [], []
            for s in range(nsub):
                rs = slice(s * HG_SUB, (s + 1) * HG_SUB)
                a = gh[s * HG_SUB + ref_row:s * HG_SUB + ref_row + 1, :]
                anchors.append(a)
                qp.append(qh[rs] * jnp.exp(gh[rs] - a))
                kn.append(kh[rs] * jnp.exp(a - gh[rs]))
            lhs_cols = []
            for j in range(nsub):
                lhs = []
                for i in range(nsub):
                    live = (i <= j) if reverse else (i >= j)
                    if not live:
                        lhs.append(jnp.zeros((HG_SUB, dk), F32))
                    elif i == j:
                        lhs.append(qp[i])
                    else:
                        lhs.append(qp[i] * jnp.exp(anchors[i] - anchors[j]))
                lhs_cols.append(jnp.concatenate(lhs, axis=0).astype(BF16))
            qe = jnp.concatenate([qp[s] * jnp.exp(anchors[s]) for s in range(nsub)], axis=0).astype(BF16)
            kd = jnp.concatenate([kn[s] * jnp.exp(tot - anchors[s]) for s in range(nsub)], axis=0).astype(BF16)
            staged.append((lhs_cols, [k.astype(BF16) for k in kn], qe, kd, jnp.exp(tot)))
        atts = []
        for bb in range(nb):
            lhs_cols, kn_b = staged[bb][0], staged[bb][1]
            att = jnp.concatenate([_dot_nt(lhs_cols[j], kn_b[j]) for j in range(nsub)], axis=1)
            atts.append(jnp.where(valid, att, 0.0).astype(BF16))
        for bb in range(nb):
            _, _, qe, kd, decay = staged[bb]
            vh = v_ref[bb, :, cols].astype(BF16)
            s_old = st_ref[bb, h]
            o = _dot(atts[bb], vh) + _dot_nt(qe, s_old.astype(BF16))
            st_ref[bb, h] = s_old * decay + _dot_tn(vh, kd)
            if final:
                o = o + oo_ref[bb, :, cols]
                ms = jnp.mean(o * o, axis=-1, keepdims=True)
                o = o * lax.rsqrt(ms + EPS) * ng_ref[:, cols] * _silu(gate_ref[bb, :, cols])
            o_ref[bb, :, cols] = o.astype(o_ref.dtype)


def hgrn_scan(u, lb, batch, seq, *, reverse, final_args=None, nb=4):
    n = u.shape[0]
    w = HG_HEADS * LANES
    nc = seq // CHUNK
    final = final_args is not None
    nb = min(nb, batch)
    assert batch % nb == 0
    v3 = lambda a: a.reshape(batch, seq, a.shape[1])

    def chunk(c):
        return nc - 1 - c if reverse else c

    ublk = lambda j: pl.BlockSpec((nb, CHUNK, w), lambda b, c: (b, chunk(c), j))
    const = pl.BlockSpec((1, w), lambda b, c: (0, 0))
    u3 = v3(u)
    in_specs = [ublk(0), ublk(2 if reverse else 1), ublk(3), const]
    args = [u3, u3, u3, lb.reshape(1, w)]
    if final:
        o_other, ng = final_args
        in_specs += [ublk(4), ublk(0), const]
        args += [u3, v3(o_other), ng.reshape(1, w)]
    out = pl.pallas_call(
        functools.partial(_hgrn_scan_kernel, reverse=reverse, final=final),
        grid=(batch // nb, nc),
        in_specs=in_specs,
        out_specs=ublk(0),
        out_shape=jax.ShapeDtypeStruct((batch, seq, w), BF16 if final else F32),
        scratch_shapes=[pltpu.VMEM((nb, HG_HEADS, LANES, LANES), F32)],
        compiler_params=_params("parallel", "arbitrary"),
        name="hgrn_scan_rev" if reverse else "hgrn_scan_fwd",
    )(*args)
    return out.reshape(n, w)


def hgrn_layer(x, g, lb, w_in, norm_g, w_out, batch, seq, final_g=None):
    u = proj_in(x, g, w_in.astype(BF16))
    o_rev = hgrn_scan(u, lb, batch, seq, reverse=True)
    o = hgrn_scan(u, lb, batch, seq, reverse=False, final_args=(o_rev, norm_g))
    return proj_out(o, w_out.astype(BF16), x, final_g)


def _rope_tables(seq):
    pos = np.arange(seq)
    rowp = (pos // GRID_W).astype(np.float64)
    colp = (pos % GRID_W).astype(np.float64)
    half = AT_HD // 4
    inv = ROPE_THETA ** (-np.arange(0, 2 * half, 2, dtype=np.float64) / (2 * half))
    ar, ac = rowp[:, None] * inv, colp[:, None] * inv
    cos = np.concatenate([np.cos(ar), np.cos(ac), np.cos(ar), np.cos(ac)], axis=1)
    sin = np.concatenate([-np.sin(ar), -np.sin(ac), np.sin(ar), np.sin(ac)], axis=1)
    return jnp.asarray(cos, F32), jnp.asarray(sin, F32)


def _pair_major(a, nheads):
    lead = a.shape[:-1]
    a = a.reshape(*lead, nheads, 2, 2, AT_HD // 4)
    return jnp.swapaxes(a, -3, -2).reshape(*lead, nheads * AT_HD)


def _gqa_proj_kernel(x_ref, g_ref, w_ref, cos_ref, sin_ref, qg_ref, kg_ref, qo_ref, ko_ref, vo_ref, go_ref, xn_ref,
                     *, tn):
    j = pl.program_id(1)
    jq = AT_HEADS * AT_HD // tn
    jk = jq + AT_KV * AT_HD // tn

    @pl.when(j == 0)
    def _():
        x = x_ref[...]
        ms = jnp.mean(x * x, axis=-1, keepdims=True)
        xn_ref[...] = (x * lax.rsqrt(ms + EPS) * g_ref[...]).astype(BF16)

    acc = _dot(xn_ref[...], w_ref[...])

    def heads(o_ref, gain_ref, scale):
        cos, sin = cos_ref[...], sin_ref[...]
        gain = gain_ref[...] * scale
        for h in range(tn // AT_HD):
            cols = slice(h * AT_HD, (h + 1) * AT_HD)
            xh = acc[:, cols]
            ms = jnp.mean(xh * xh, axis=-1, keepdims=True)
            xn = xh * lax.rsqrt(ms + EPS) * gain
            o_ref[:, cols] = (xn * cos + pltpu.roll(xn, AT_HD // 2, axis=1) * sin).astype(o_ref.dtype)

    @pl.when(j < jq)
    def _():
        heads(qo_ref, qg_ref, AT_HD ** -0.5 * math.log2(math.e))

    @pl.when((j >= jq) & (j < jk))
    def _():
        heads(ko_ref, kg_ref, 1.0)

    @pl.when((j >= jk) & (j < 2 * jk - jq))
    def _():
        vo_ref[...] = acc.astype(vo_ref.dtype)

    @pl.when(j >= 2 * jk - jq)
    def _():
        go_ref[...] = acc.astype(go_ref.dtype)


def gqa_proj(x, g, w, q_g, k_g, seq, *, tm=1024, tn=1024):
    n, d = x.shape
    qw, kw = AT_HEADS * AT_HD, AT_KV * AT_HD
    tm = min(tm, seq)
    assert w.shape[1] == 2 * qw + 2 * kw and kw == tn and qw % tn == 0 and seq % tm == 0
    nb = seq // tm
    jq = qw // tn
    jg = jq + 2
    cos, sin = _rope_tables(seq)
    pos = lambda i, j: (i % nb, 0)
    const = lambda i, j: (0, 0)
    blk = lambda f: pl.BlockSpec((tm, tn), f)
    return pl.pallas_call(
        functools.partial(_gqa_proj_kernel, tn=tn),
        grid=(n // tm, w.shape[1] // tn),
        in_specs=[pl.BlockSpec((tm, d), lambda i, j: (i, 0)), pl.BlockSpec((1, d), const),
                  pl.BlockSpec((d, tn), lambda i, j: (0, j)),
                  pl.BlockSpec((tm, AT_HD), pos), pl.BlockSpec((tm, AT_HD), pos),
                  pl.BlockSpec((1, AT_HD), const), pl.BlockSpec((1, AT_HD), const)],
        out_specs=[blk(lambda i, j: (i, jnp.minimum(j, jq - 1))), blk(lambda i, j: (i, 0)),
                   blk(lambda i, j: (i, 0)), blk(lambda i, j: (i, jnp.clip(j - jg, 0, jq - 1)))],
        out_shape=[jax.ShapeDtypeStruct((n, qw), BF16), jax.ShapeDtypeStruct((n, kw), BF16),
                   jax.ShapeDtypeStruct((n, kw), BF16), jax.ShapeDtypeStruct((n, qw), BF16)],
        scratch_shapes=[pltpu.VMEM((tm, d), BF16)],
        compiler_params=_params("parallel", "arbitrary"),
        name="gqa_proj",
    )(x, g.reshape(1, d), w, cos, sin, q_g.reshape(1, AT_HD), k_g.reshape(1, AT_HD))


AT_SAFE_LOG2_RANGE = 100.0


def _gqa_flash_kernel(q_ref, k_ref, v_ref, gate_ref, o_ref, kmax_ref, *, tk):
    tq = q_ref.shape[0]
    seq = k_ref.shape[0]
    grp = AT_HEADS // AT_KV
    rows = grp * tq
    nkv = seq // tk

    @pl.when(pl.program_id(2) == 0)
    def _():
        kf = k_ref[...].astype(F32)
        k2 = jnp.max(jnp.sum(kf * kf, axis=-1, keepdims=True), axis=0, keepdims=True)
        kmax_ref[...] = jnp.broadcast_to(jnp.sqrt(k2), kmax_ref.shape)

    qs = jnp.concatenate([q_ref[:, j * AT_HD:(j + 1) * AT_HD] for j in range(grp)], axis=0)
    qf = qs.astype(F32)
    c = jnp.sqrt(jnp.sum(qf * qf, axis=-1, keepdims=True)) * kmax_ref[0:1, 0:1] * (1.0 + 2.0 ** -8)
    cmax = jnp.max(c)

    def finish(acc, l):
        o = acc * (1.0 / l)
        for j in range(grp):
            cols = slice(j * AT_HD, (j + 1) * AT_HD)
            o_ref[:, cols] = (o[j * tq:(j + 1) * tq, :] * _silu(gate_ref[:, cols])).astype(o_ref.dtype)

    @pl.when(2.0 * cmax <= AT_SAFE_LOG2_RANGE)
    def _():
        lvec = jnp.zeros((rows, LANES), F32)
        acc = jnp.zeros((rows, AT_HD), F32)
        for t in range(nkv):
            ks = k_ref[t * tk:(t + 1) * tk, :]
            vs = v_ref[t * tk:(t + 1) * tk, :]
            p = jnp.exp2(_dot_nt(qs, ks) - c)
            for w in range(tk // LANES):
                lvec = lvec + p[:, w * LANES:(w + 1) * LANES]
            acc = acc + _dot(p.astype(BF16), vs)
        finish(acc, jnp.sum(lvec, axis=-1, keepdims=True))

    @pl.when(2.0 * cmax > AT_SAFE_LOG2_RANGE)
    def _():
        def body(t, carry):
            m, l, acc = carry
            ks = k_ref[pl.ds(t * tk, tk), :]
            vs = v_ref[pl.ds(t * tk, tk), :]
            s = _dot_nt(qs, ks)
            m_new = jnp.maximum(m, jnp.max(s, axis=-1, keepdims=True))
            alpha = jnp.exp2(m - m_new)
            p = jnp.exp2(s - m_new)
            l = alpha * l + jnp.sum(p, axis=-1, keepdims=True)
            acc = alpha * acc + _dot(p.astype(BF16), vs)
            return m_new, l, acc

        init = (jnp.full((rows, 1), -jnp.inf, F32), jnp.zeros((rows, 1), F32), jnp.zeros((rows, AT_HD), F32))
        _, l, acc = lax.fori_loop(0, nkv, body, init)
        finish(acc, l)


def gqa_flash(q, k, v, gate, batch, seq, *, tq=512, tk=512):
    n = q.shape[0]
    grp = AT_HEADS // AT_KV
    gw = grp * AT_HD
    tq = min(tq, seq)
    nq = seq // tq
    return pl.pallas_call(
        functools.partial(_gqa_flash_kernel, tk=min(tk, seq)),
        grid=(batch, AT_KV, nq),
        in_specs=[pl.BlockSpec((tq, gw), lambda b, h, i: (b * nq + i, h)),
                  pl.BlockSpec((seq, AT_HD), lambda b, h, i: (b, h)),
                  pl.BlockSpec((seq, AT_HD), lambda b, h, i: (b, h)),
                  pl.BlockSpec((tq, gw), lambda b, h, i: (b * nq + i, h))],
        out_specs=pl.BlockSpec((tq, gw), lambda b, h, i: (b * nq + i, h)),
        out_shape=jax.ShapeDtypeStruct((n, AT_HEADS * AT_HD), BF16),
        scratch_shapes=[pltpu.VMEM((SUBLANES, LANES), F32)],
        compiler_params=_params("parallel", "parallel", "arbitrary"),
        name="gqa_flash",
    )(q, k, v, gate)


def gqa_layer(x, g, w_in, q_g, k_g, w_out, batch, seq, final_g=None):
    nqk = AT_HEADS + AT_KV
    w = jnp.concatenate([_pair_major(w_in[:, :nqk * AT_HD], nqk), w_in[:, nqk * AT_HD:]], axis=1).astype(BF16)
    q, k, v, gate = gqa_proj(x, g, w, _pair_major(q_g, 1), _pair_major(k_g, 1), seq)
    o = gqa_flash(q, k, v, gate, batch, seq)
    return proj_out(o, w_out.astype(BF16), x, final_g)


def _t5_bucket_np(rel):
    half = REL_BUCKETS // 2
    exact = half // 2
    nabs = np.abs(rel)
    large = exact + (np.log(np.maximum(nabs, 1).astype(np.float32) / exact)
                     / math.log(REL_MAX_DIST / exact) * (half - exact)).astype(np.int32)
    large = np.minimum(large, half - 1)
    return np.where(rel > 0, half, 0) + np.where(nabs < exact, nabs, large)


DL_TQ = 128
DL_HALO = 64
DL_NK = DL_TQ + 2 * DL_HALO
DL_BATCH = 4


def _dilated_buckets():
    qi = np.arange(DL_TQ)[:, None]
    kj = np.arange(DL_NK)[None, :]
    tabs = [_t5_bucket_np((kj - DL_HALO - qi) * dil).astype(np.int32) for _, dil in DL_PAIRS]
    band = np.abs(kj - DL_HALO - qi) <= DL_HALO
    present = [sorted(set(t[band].tolist())) for t in tabs]
    return np.stack(tabs), present


def _dilated_kernel(relb_ref, bucket_ref, *refs, seq, tok, present):
    ng = len(DL_PAIRS)
    io = refs[:7 * ng]
    gate_ref, out_ref, bias_ref, kbuf, vbuf, obuf, lbuf = refs[7 * ng:]
    i = pl.program_id(1)
    p = pl.program_id(2)
    qi = lax.broadcasted_iota(jnp.int32, (DL_TQ, DL_NK), 0)
    kj = lax.broadcasted_iota(jnp.int32, (DL_TQ, DL_NK), 1)
    band = jnp.abs(kj - DL_HALO - qi) <= DL_HALO
    lane = lax.broadcasted_iota(jnp.int32, (DL_TQ, LANES), 1)
    lo = lane < DL_HD
    log2e = math.log2(math.e)
    scale = DL_HD ** -0.5 * log2e

    @pl.when((pl.program_id(0) == 0) & (i == 0) & (p == 0))
    def _():
        for g in range(ng):
            bk = bucket_ref[g]

            def fill(h, carry, g=g, bk=bk):
                acc = jnp.zeros((DL_TQ, DL_NK), F32)
                for b in present[g]:
                    acc = jnp.where(bk == b, relb_ref[b, h], acc)
                bias_ref[g, h] = jnp.where(band, acc * log2e, NEG_BIG)
                return carry

            lax.fori_loop(0, DL_HEADS, fill, 0)

    for g, (_, dil) in enumerate(DL_PAIRS):
        q_ref, kp_ref, kc_ref, kn_ref, vp_ref, vc_ref, vn_ref = io[7 * g:7 * g + 7]
        halo = DL_HALO * dil
        ls = seq // dil
        ppb = tok // dil
        nsub = ppb // DL_TQ
        kbuf[0:halo, :] = kp_ref[...]
        kbuf[halo:halo + tok, :] = kc_ref[...]
        kbuf[halo + tok:2 * halo + tok, :] = kn_ref[...]
        vbuf[0:halo, :] = vp_ref[...]
        vbuf[halo:halo + tok, :] = vc_ref[...]
        vbuf[halo + tok:2 * halo + tok, :] = vn_ref[...]

        def blocks(it, carry, g=g, dil=dil, ls=ls, ppb=ppb, nsub=nsub, q_ref=q_ref):
            ld = []
            for t in range(DL_BATCH):
                idx = it * DL_BATCH + t
                r = idx // nsub
                j = idx % nsub
                start = r + j * (DL_TQ * dil)
                if dil == 1:
                    qrows, krows = pl.ds(start, DL_TQ), pl.ds(start, DL_NK)
                else:
                    qrows, krows = pl.ds(start, DL_TQ, stride=dil), pl.ds(start, DL_NK, stride=dil)
                q2 = q_ref[qrows, :] * scale
                qh = [jnp.where(lo if half == 0 else ~lo, q2, 0.0).astype(BF16) for half in range(2)]
                mk = i * ppb + j * DL_TQ - DL_HALO + kj
                ld.append((qrows, qh, kbuf[krows, :].astype(BF16), vbuf[krows, :].astype(BF16),
                           (mk >= 0) & (mk < ls)))
            scores = [[_dot_nt(qh, k2) for qh in qhs] for _, qhs, k2, _, _ in ld]
            soft = []
            for t in range(DL_BATCH):
                inside = ld[t][4]
                per_head = []
                for half in range(2):
                    s = jnp.where(inside, scores[t][half] + bias_ref[g, 2 * p + half], NEG_BIG)
                    m = jnp.max(s, axis=-1, keepdims=True)
                    pexp = jnp.exp2(s - m)
                    per_head.append((pexp.astype(BF16), m, jnp.sum(pexp, axis=-1, keepdims=True)))
                soft.append(per_head)
            pv = [[_dot(pb, ld[t][3]) for pb, _, _ in soft[t]] for t in range(DL_BATCH)]
            for t in range(DL_BATCH):
                qrows = ld[t][0]
                outs = [pv[t][half] * (1.0 / soft[t][half][2]) for half in range(2)]
                lses = [soft[t][half][1] + jnp.log2(soft[t][half][2]) for half in range(2)]
                obuf[g, qrows, :] = jnp.where(lo, outs[0], outs[1])
                lbuf[g, qrows, :] = jnp.where(lo, lses[0], lses[1])
            return carry

        lax.fori_loop(0, dil * nsub // DL_BATCH, blocks, 0)

    la, lb, lc = lbuf[0], lbuf[1], lbuf[2]
    m = jnp.maximum(jnp.maximum(la, lb), lc)
    ea, eb, ec = jnp.exp2(la - m), jnp.exp2(lb - m), jnp.exp2(lc - m)
    o = (ea * obuf[0] + eb * obuf[1] + ec * obuf[2]) / (ea + eb + ec)
    out_ref[...] = (o * _silu(gate_ref[...])).astype(out_ref.dtype)


def dilated_attention(u, rel_bias, batch, seq, *, tok=2048):
    n, win = u.shape
    hw = DL_HEADS * DL_HD
    npair = hw // LANES
    tok = min(tok, seq)
    nblk = seq // tok
    max_halo = DL_HALO * max(d for _, d in DL_PAIRS)
    assert tok % (DL_TQ * max(d for _, d in DL_PAIRS)) == 0 and tok % max_halo == 0
    buckets, present = _dilated_buckets()

    in_specs = [pl.BlockSpec(memory_space=pltpu.SMEM),
                pl.BlockSpec((len(DL_PAIRS), DL_TQ, DL_NK), lambda b, i, p: (0, 0, 0))]
    args = [rel_bias, jnp.asarray(buckets)]
    for gi, (_, dil) in enumerate(DL_PAIRS):
        halo = DL_HALO * dil
        hpb = tok // halo
        last = n // halo - 1

        def col(c, gi=gi):
            return lambda p: (gi * 3 + c) * npair + p

        def cur(c):
            cf = col(c)
            return pl.BlockSpec((tok, LANES), lambda b, i, p: (b * nblk + i, cf(p)))

        def prev(c, hpb=hpb):
            cf = col(c)
            return pl.BlockSpec((halo, LANES), lambda b, i, p: (jnp.maximum((b * nblk + i) * hpb - 1, 0), cf(p)))

        def nxt(c, hpb=hpb, last=last):
            cf = col(c)
            return pl.BlockSpec((halo, LANES),
                                lambda b, i, p: (jnp.minimum((b * nblk + i + 1) * hpb, last), cf(p)))

        in_specs += [cur(0), prev(1), cur(1), nxt(1), prev(2), cur(2), nxt(2)]
        args += [u] * 7
    gate_col0 = 3 * len(DL_PAIRS) * npair
    in_specs.append(pl.BlockSpec((tok, LANES), lambda b, i, p: (b * nblk + i, gate_col0 + p)))
    args.append(u)
    return pl.pallas_call(
        functools.partial(_dilated_kernel, seq=seq, tok=tok, present=present),
        grid=(batch, nblk, npair),
        in_specs=in_specs,
        out_specs=pl.BlockSpec((tok, LANES), lambda b, i, p: (b * nblk + i, p)),
        out_shape=jax.ShapeDtypeStruct((n, hw), BF16),
        scratch_shapes=[pltpu.VMEM((len(DL_PAIRS), DL_HEADS, DL_TQ, DL_NK), F32),
                        pltpu.VMEM((tok + 2 * max_halo, LANES), F32),
                        pltpu.VMEM((tok + 2 * max_halo, LANES), F32),
                        pltpu.VMEM((len(DL_PAIRS), tok, LANES), F32),
                        pltpu.VMEM((len(DL_PAIRS), tok, LANES), F32)],
        compiler_params=_params("arbitrary", "arbitrary", "arbitrary"),
        name="dilated_attention",
    )(*args)


def dilated_layer(x, g, rel_bias, w_in, w_out, batch, seq, final_g=None):
    u = proj_in(x, g, w_in.astype(BF16))
    o = dilated_attention(u, rel_bias, batch, seq)
    return proj_out(o, w_out.astype(BF16), x, final_g)


def kernel(x, norm_g, final_g, rel_bias, hgrn_lb, ssd_w_in, ssd_conv_w, ssd_conv_b, ssd_dt_bias, ssd_a_log, ssd_d,
           ssd_norm_g, ssd_w_out, hg_w_in, hg_norm_g, hg_w_out, at_w_in, at_q_norm_g, at_k_norm_g, at_w_out,
           dl_w_in, dl_w_out):
    batch, seq, d = x.shape
    depth = norm_g.shape[0]
    n_mixers = 4
    lb_sm = jax.nn.softmax(hgrn_lb.astype(F32), axis=0)
    lb_all = jnp.cumsum(lb_sm, axis=0) - lb_sm[0:1]
    h = x.reshape(batch * seq, d)
    for layer in range(depth):
        kind, slot = layer % n_mixers, layer // n_mixers
        fg = final_g if layer == depth - 1 else None
        if kind == 0:
            h = ssd_layer(h, norm_g[layer], ssd_w_in[slot], ssd_conv_w[slot], ssd_conv_b[slot], ssd_dt_bias[slot],
                          ssd_a_log[slot], ssd_d[slot], ssd_norm_g[slot], ssd_w_out[slot], batch, seq, fg)
        elif kind == 1:
            h = hgrn_layer(h, norm_g[layer], lb_all[layer], hg_w_in[slot], hg_norm_g[slot], hg_w_out[slot],
                           batch, seq, fg)
        elif kind == 2:
            h = gqa_layer(h, norm_g[layer], at_w_in[slot], at_q_norm_g[slot], at_k_norm_g[slot], at_w_out[slot],
                          batch, seq, fg)
        else:
            h = dilated_layer(h, norm_g[layer], rel_bias, dl_w_in[slot], dl_w_out[slot], batch, seq, fg)
    return h.reshape(batch, seq, d)
```

```python
import functools
import math

import jax
import jax.numpy as jnp
import numpy as np
from jax import lax
from jax.experimental import pallas as pl
from jax.experimental.pallas import tpu as pltpu

F32 = jnp.float32
BF16 = jnp.bfloat16

EPS = 1e-6
NEG_BIG = -1e30
GRID_W = 64
ROPE_THETA = 10000.0

SSD_HEADDIM = 64
SSD_HEADS = 32
SSD_GROUPS = 4
SSD_STATE = 128
SSD_CONV = 7
HG_HEADS = 8
HG_SUB = 32
AT_HEADS = 16
AT_KV = 8
AT_HD = 128
DL_PAIRS = ((128, 1), (512, 4), (2048, 16))
DL_HEADS = 16
DL_HD = 64
REL_BUCKETS = 32
REL_MAX_DIST = 1024

LANES = 128
SUBLANES = 8
CHUNK = 128
VMEM_LIMIT = 56 * 1024 * 1024


def _params(*sem):
    return pltpu.CompilerParams(dimension_semantics=sem, vmem_limit_bytes=VMEM_LIMIT)


def _sigmoid(x):
    return 0.5 * jnp.tanh(0.5 * x) + 0.5


def _silu(x):
    return x * _sigmoid(x)


def _softplus(x):
    return jnp.maximum(x, 0.0) + jnp.log(1.0 + jnp.exp(-jnp.abs(x)))


def _dot(a, b):
    return jnp.dot(a, b, preferred_element_type=F32)


def _dot_nt(a, b):
    return lax.dot_general(a, b, (((1,), (1,)), ((), ())), preferred_element_type=F32)


def _dot_tn(a, b):
    return lax.dot_general(a, b, (((0,), (0,)), ((), ())), preferred_element_type=F32)


def _prefix_sum(tri, x):
    x1 = x.astype(BF16)
    r1 = x - x1.astype(F32)
    x2 = r1.astype(BF16)
    x3 = (r1 - x2.astype(F32)).astype(BF16)
    return _dot(tri, x1) + _dot(tri, x2) + _dot(tri, x3)


def _proj_in_kernel(x_ref, g_ref, w_ref, o_ref, xn_ref):
    @pl.when(pl.program_id(1) == 0)
    def _():
        x = x_ref[...]
        ms = jnp.mean(x * x, axis=-1, keepdims=True)
        xn_ref[...] = (x * lax.rsqrt(ms + EPS) * g_ref[...]).astype(BF16)

    o_ref[...] = _dot(xn_ref[...], w_ref[...]).astype(o_ref.dtype)


def proj_in(x, g, w, *, tm=2048, tn=1024, out_dtype=F32):
    n, d = x.shape
    dout = w.shape[1]
    tm = min(tm, n)
    tn = min(tn, dout)
    assert n % tm == 0 and dout % tn == 0
    return pl.pallas_call(
        _proj_in_kernel,
        grid=(n // tm, dout // tn),
        in_specs=[pl.BlockSpec((tm, d), lambda i, j: (i, 0)),
                  pl.BlockSpec((1, d), lambda i, j: (0, 0)),
                  pl.BlockSpec((d, tn), lambda i, j: (0, j))],
        out_specs=pl.BlockSpec((tm, tn), lambda i, j: (i, j)),
        out_shape=jax.ShapeDtypeStruct((n, dout), out_dtype),
        scratch_shapes=[pltpu.VMEM((tm, d), BF16)],
        compiler_params=_params("parallel", "arbitrary"),
        name="proj_in",
    )(x, g.reshape(1, d), w)


def _proj_out_kernel(a_ref, w_ref, r_ref, g_ref, o_ref, *, final):
    y = r_ref[...] + _dot(a_ref[...], w_ref[...])
    if final:
        ms = jnp.mean(y * y, axis=-1, keepdims=True)
        y = y * lax.rsqrt(ms + EPS) * g_ref[...]
    o_ref[...] = y


def proj_out(a, w, res, final_g=None, *, tm=512):
    n, k = a.shape
    d = w.shape[1]
    tm = min(tm, n)
    assert n % tm == 0
    g = jnp.ones((1, d), F32) if final_g is None else final_g.reshape(1, d)
    return pl.pallas_call(
        functools.partial(_proj_out_kernel, final=final_g is not None),
        grid=(n // tm,),
        in_specs=[pl.BlockSpec((tm, k), lambda i: (i, 0)),
                  pl.BlockSpec((k, d), lambda i: (0, 0)),
                  pl.BlockSpec((tm, d), lambda i: (i, 0)),
                  pl.BlockSpec((1, d), lambda i: (0, 0))],
        out_specs=pl.BlockSpec((tm, d), lambda i: (i, 0)),
        out_shape=jax.ShapeDtypeStruct((n, d), F32),
        compiler_params=_params("parallel"),
        name="proj_out",
    )(a, w, res, g)


def _ssd_conv_kernel(xp_ref, xc_ref, xn_ref, bp_ref, bc_ref, bn_ref, wx_ref, wb_ref, bx_ref, bb_ref,
                     ox_ref, ob_ref, *, nblk):
    i = pl.program_id(1)
    pad = SSD_CONV // 2

    def conv(prev_ref, cur_ref, next_ref, w_ref, b_ref, o_ref):
        rows = cur_ref.shape[0]
        prev = jnp.where(i > 0, prev_ref[...], 0.0)
        nxt = jnp.where(i < nblk - 1, next_ref[...], 0.0)
        ext = jnp.concatenate([prev, cur_ref[...], nxt], axis=0)
        acc = jnp.zeros(cur_ref.shape, F32) + b_ref[...]
        total = rows + 2 * SUBLANES
        for t in range(SSD_CONV):
            shifted = ext if t == pad else pltpu.roll(ext, (pad - t) % total, axis=0)
            acc = acc + shifted[SUBLANES:SUBLANES + rows, :] * w_ref[t:t + 1, :]
        o_ref[...] = _silu(acc)

    conv(xp_ref, xc_ref, xn_ref, wx_ref, bx_ref, ox_ref)
    conv(bp_ref, bc_ref, bn_ref, wb_ref, bb_ref, ob_ref)


def ssd_conv(u, conv_w, conv_b, batch, seq, *, tc=256):
    n = u.shape[0]
    di = SSD_HEADS * SSD_HEADDIM
    gn2 = 2 * SSD_GROUPS * SSD_STATE
    tc = min(tc, seq)
    nblk = seq // tc
    r8 = tc // SUBLANES
    last8 = n // SUBLANES - 1

    def cur(wblk):
        return lambda b, i: (b * nblk + i, wblk)

    def prev(wblk):
        return lambda b, i: (jnp.maximum((b * nblk + i) * r8 - 1, 0), wblk)

    def nxt(wblk):
        return lambda b, i: (jnp.minimum((b * nblk + i + 1) * r8, last8), wblk)

    wx, wb = conv_w[:, :di], conv_w[:, di:]
    bx, bb = conv_b[:di].reshape(1, di), conv_b[di:].reshape(1, gn2)
    const = lambda b, i: (0, 0)
    return pl.pallas_call(
        functools.partial(_ssd_conv_kernel, nblk=nblk),
        grid=(batch, nblk),
        in_specs=[pl.BlockSpec((SUBLANES, di), prev(1)), pl.BlockSpec((tc, di), cur(1)),
                  pl.BlockSpec((SUBLANES, di), nxt(1)),
                  pl.BlockSpec((SUBLANES, gn2), prev(4)), pl.BlockSpec((tc, gn2), cur(4)),
                  pl.BlockSpec((SUBLANES, gn2), nxt(4)),
                  pl.BlockSpec((SSD_CONV, di), const), pl.BlockSpec((SSD_CONV, gn2), const),
                  pl.BlockSpec((1, di), const), pl.BlockSpec((1, gn2), const)],
        out_specs=[pl.BlockSpec((tc, di), lambda b, i: (b * nblk + i, 0)),
                   pl.BlockSpec((tc, gn2), lambda b, i: (b * nblk + i, 0))],
        out_shape=[jax.ShapeDtypeStruct((n, di), F32), jax.ShapeDtypeStruct((n, gn2), F32)],
        compiler_params=_params("parallel", "parallel"),
        name="ssd_conv",
    )(u, u, u, u, u, u, wx, wb, bx, bb)


def _ssd_scan_kernel(x_ref, bc_ref, dtr_ref, dtb_ref, alog_ref, *rest, reverse, final):
    if final:
        z_ref, yo_ref, dskip_ref, ng_ref, o_ref, st_ref = rest
    else:
        o_ref, st_ref = rest
    c = pl.program_id(1)
    nb = x_ref.shape[0]
    q = CHUNK
    gn = SSD_GROUPS * SSD_STATE
    hpg = SSD_HEADS // SSD_GROUPS
    gw = hpg * SSD_HEADDIM
    hoff = SSD_HEADS if reverse else 0
    far = 0 if reverse else q - 1

    @pl.when(c == 0)
    def _():
        st_ref[...] = jnp.zeros(st_ref.shape, F32)

    row = lax.broadcasted_iota(jnp.int32, (q, q), 0)
    col = lax.broadcasted_iota(jnp.int32, (q, q), 1)
    valid = (col >= row) if reverse else (col <= row)
    tri = valid.astype(BF16)
    lane = lax.broadcasted_iota(jnp.int32, (q, LANES), 1)
    lo = lane < SSD_HEADDIM
    lo_row = lo[0:1, :]
    neg_a = -jnp.exp(alog_ref[...])

    prep = []
    for bb in range(nb):
        dt = _softplus(dtr_ref[bb] + dtb_ref[...])
        cs_col = _prefix_sum(tri, dt * neg_a)
        cs_row = cs_col.T
        dt_row = dt.T
        w_row = jnp.exp(cs_row[:, far:far + 1] - cs_row) * dt_row
        dec = jnp.exp(cs_col[far:far + 1, :])
        prep.append((cs_col, cs_row, dt_row, w_row, dec))

    for g in range(SSD_GROUPS):
        grp = []
        for bb in range(nb):
            b_f = bc_ref[bb, :, g * SSD_STATE:(g + 1) * SSD_STATE]
            c_f = bc_ref[bb, :, gn + g * SSD_STATE:gn + (g + 1) * SSD_STATE]
            grp.append((_dot_nt(c_f.astype(BF16), b_f.astype(BF16)), c_f, b_f.T))
        y_parts = [[] for _ in range(nb)]
        for pp in range(hpg // 2):
            h0 = g * hpg + 2 * pp
            cols = slice(h0 * SSD_HEADDIM, (h0 + 2) * SSD_HEADDIM)
            scols = slice(2 * pp * SSD_HEADDIM, (2 * pp + 2) * SSD_HEADDIM)
            ops = []
            for bb in range(nb):
                cs_col, cs_row, dt_row, w_row, dec = prep[bb]
                cb, c_f, b_t = grp[bb]
                x2 = x_ref[bb, :, cols]
                x2b = x2.astype(BF16)
                s_old = st_ref[bb, g, :, scols]
                rhs = jnp.concatenate([x2b, s_old.astype(BF16)], axis=0)
                lhs, bws = [], []
                for h in (h0, h0 + 1):
                    hl = hoff + h
                    colb = jnp.broadcast_to(cs_col[:, hl:hl + 1], (q, q))
                    lmat = jnp.exp(jnp.where(valid, colb - cs_row[hl:hl + 1, :], NEG_BIG))
                    m = cb * (lmat * dt_row[hl:hl + 1, :])
                    ce = c_f * jnp.exp(colb)
                    lhs.append(jnp.concatenate([m, ce], axis=1).astype(BF16))
                    bws.append((b_t * w_row[hl:hl + 1, :]).astype(BF16))
                dec2 = jnp.where(lo_row, dec[:, hoff + h0:hoff + h0 + 1], dec[:, hoff + h0 + 1:hoff + h0 + 2])
                ops.append((x2, x2b, s_old, rhs, lhs, bws, dec2))
            prods = []
            for bb in range(nb):
                _, x2b, _, rhs, lhs, bws, _ = ops[bb]
                prods.append(([_dot(l, rhs) for l in lhs], [_dot(w, x2b) for w in bws]))
            for bb in range(nb):
                x2, _, s_old, _, _, _, dec2 = ops[bb]
                ys, sts = prods[bb]
                y2 = jnp.where(lo, ys[0], ys[1])
                st_ref[bb, g, :, scols] = s_old * dec2 + jnp.where(lo, sts[0], sts[1])
                if final:
                    y2 = y2 + yo_ref[bb, :, cols] + x2 * dskip_ref[:, cols]
                    y_parts[bb].append(y2 * _silu(z_ref[bb, :, cols]))
                else:
                    o_ref[bb, :, cols] = y2
        if final:
            gcols = slice(g * gw, (g + 1) * gw)
            for bb in range(nb):
                yg = jnp.concatenate(y_parts[bb], axis=1)
                ms = jnp.mean(yg * yg, axis=-1, keepdims=True)
                o_ref[bb, :, gcols] = (yg * lax.rsqrt(ms + EPS) * ng_ref[:, gcols]).astype(o_ref.dtype)


def ssd_scan(xc, bc, dtr, dt_bias, a_log, batch, seq, *, reverse, final_args=None, nb=4):
    n, di = xc.shape
    nc = seq // CHUNK
    gn2 = bc.shape[1]
    final = final_args is not None
    nb = min(nb, batch)
    assert batch % nb == 0
    v3 = lambda a: a.reshape(batch, seq, a.shape[1])

    def chunk(c):
        return nc - 1 - c if reverse else c

    blk = lambda w: pl.BlockSpec((nb, CHUNK, w), lambda b, c: (b, chunk(c), 0))
    const = lambda w: pl.BlockSpec((1, w), lambda b, c: (0, 0))
    in_specs = [blk(di), blk(gn2), blk(LANES), const(LANES), const(LANES)]
    args = [v3(xc), v3(bc), v3(dtr), dt_bias, a_log]
    if final:
        u, y_other, dskip, ng = final_args
        in_specs += [blk(di), blk(di), const(di), const(di)]
        args += [v3(u), v3(y_other), dskip, ng]
    out = pl.pallas_call(
        functools.partial(_ssd_scan_kernel, reverse=reverse, final=final),
        grid=(batch // nb, nc),
        in_specs=in_specs,
        out_specs=blk(di),
        out_shape=jax.ShapeDtypeStruct((batch, seq, di), BF16 if final else F32),
        scratch_shapes=[pltpu.VMEM((nb, SSD_GROUPS, SSD_STATE, (SSD_HEADS // SSD_GROUPS) * SSD_HEADDIM), F32)],
        compiler_params=_params("parallel", "arbitrary"),
        name="ssd_scan_rev" if reverse else "ssd_scan_fwd",
    )(*args)
    return out.reshape(n, di)


def ssd_layer(x, g, w_in, conv_w, conv_b, dt_bias, a_log, d_skip, norm_g, w_out, batch, seq, final_g=None):
    di = SSD_HEADS * SSD_HEADDIM
    main = 2 * di + 2 * SSD_GROUPS * SSD_STATE
    w_main = w_in[:, :main].astype(BF16)
    w_dt = jnp.pad(w_in[:, main:], ((0, 0), (0, LANES - 2 * SSD_HEADS))).astype(BF16)
    u = proj_in(x, g, w_main)
    dtr = proj_in(x, g, w_dt)
    xc, bc = ssd_conv(u, conv_w, conv_b, batch, seq)
    pad = lambda v: jnp.pad(v.reshape(1, 2 * SSD_HEADS), ((0, 0), (0, LANES - 2 * SSD_HEADS)))
    dtb, alog = pad(dt_bias), pad(a_log)
    y_rev = ssd_scan(xc, bc, dtr, dtb, alog, batch, seq, reverse=True)
    dskip = jnp.repeat(d_skip, SSD_HEADDIM).reshape(1, di)
    y = ssd_scan(xc, bc, dtr, dtb, alog, batch, seq, reverse=False,
                 final_args=(u, y_rev, dskip, norm_g.reshape(1, di)))
    return proj_out(y, w_out.astype(BF16), x, final_g)


def _hgrn_scan_kernel(q_ref, f_ref, v_ref, lb_ref, *rest, reverse, final):
    if final:
        gate_ref, oo_ref, ng_ref, o_ref, st_ref = rest
    else:
        o_ref, st_ref = rest
    c = pl.program_id(1)
    n = CHUNK
    nsub = n // HG_SUB
    dk = LANES

    @pl.when(c == 0)
    def _():
        st_ref[...] = jnp.zeros(st_ref.shape, F32)

    nb = q_ref.shape[0]
    lb = lb_ref[...]
    row = lax.broadcasted_iota(jnp.int32, (n, n), 0)
    col = lax.broadcasted_iota(jnp.int32, (n, n), 1)
    valid = (col >= row) if reverse else (col <= row)
    tri = valid.astype(BF16)
    ref_row = HG_SUB // 2 - 1 if reverse else HG_SUB // 2
    far = 0 if reverse else n - 1

    prep = []
    for bb in range(nb):
        f = lb + (1.0 - lb) * _sigmoid(f_ref[bb])
        prep.append((_silu(q_ref[bb]), 1.0 - f, _prefix_sum(tri, jnp.log(f))))

    for h in range(HG_HEADS):
        cols = slice(h * dk, (h + 1) * dk)
        staged = []
        for bb in range(nb):
            qa, ka, gsum = prep[bb]
            gh, qh, kh = gsum[:, cols], qa[:, cols], ka[:, cols]
            tot = gh[far:far + 1, :]
            qp, kn, anchors = [], [], []
            for s in range(nsub):
                rs = slice(s * HG_SUB, (s + 1) * HG_SUB)
                a = gh[s * HG_SUB + ref_row:s * HG_SUB + ref_row + 1, :]
                anchors.append(a)
                qp.append(qh[rs] * jnp.exp(gh[rs] - a))
                kn.append(kh[rs] * jnp.exp(a - gh[rs]))
            lhs_cols = []
            for j in range(nsub):
                lhs = []
                for i in range(nsub):
                    live = (i <= j) if reverse else (i >= j)
                    if not live:
                        lhs.append(jnp.zeros((HG_SUB, dk), F32))
                    elif i == j:
                        lhs.append(qp[i])
                    else:
                        lhs.append(qp[i] * jnp.exp(anchors[i] - anchors[j]))
                lhs_cols.append(jnp.concatenate(lhs, axis=0).astype(BF16))
            qe = jnp.concatenate([qp[s] * jnp.exp(anchors[s]) for s in range(nsub)], axis=0).astype(BF16)
            kd = jnp.concatenate([kn[s] * jnp.exp(tot - anchors[s]) for s in range(nsub)], axis=0).astype(BF16)
            staged.append((lhs_cols, [k.astype(BF16) for k in kn], qe, kd, jnp.exp(tot)))
        atts = []
        for bb in range(nb):
            lhs_cols, kn_b = staged[bb][0], staged[bb][1]
            att = jnp.concatenate([_dot_nt(lhs_cols[j], kn_b[j]) for j in range(nsub)], axis=1)
            atts.append(jnp.where(valid, att, 0.0).astype(BF16))
        for bb in range(nb):
            _, _, qe, kd, decay = staged[bb]
            vh = v_ref[bb, :, cols].astype(BF16)
            s_old = st_ref[bb, h]
            o = _dot(atts[bb], vh) + _dot_nt(qe, s_old.astype(BF16))
            st_ref[bb, h] = s_old * decay + _dot_tn(vh, kd)
            if final:
                o = o + oo_ref[bb, :, cols]
                ms = jnp.mean(o * o, axis=-1, keepdims=True)
                o = o * lax.rsqrt(ms + EPS) * ng_ref[:, cols] * _silu(gate_ref[bb, :, cols])
            o_ref[bb, :, cols] = o.astype(o_ref.dtype)


def hgrn_scan(u, lb, batch, seq, *, reverse, final_args=None, nb=4):
    n = u.shape[0]
    w = HG_HEADS * LANES
    nc = seq // CHUNK
    final = final_args is not None
    nb = min(nb, batch)
    assert batch % nb == 0
    v3 = lambda a: a.reshape(batch, seq, a.shape[1])

    def chunk(c):
        return nc - 1 - c if reverse else c

    ublk = lambda j: pl.BlockSpec((nb, CHUNK, w), lambda b, c: (b, chunk(c), j))
    const = pl.BlockSpec((1, w), lambda b, c: (0, 0))
    u3 = v3(u)
    in_specs = [ublk(0), ublk(2 if reverse else 1), ublk(3), const]
    args = [u3, u3, u3, lb.reshape(1, w)]
    if final:
        o_other, ng = final_args
        in_specs += [ublk(4), ublk(0), const]
        args += [u3, v3(o_other), ng.reshape(1, w)]
    out = pl.pallas_call(
        functools.partial(_hgrn_scan_kernel, reverse=reverse, final=final),
        grid=(batch // nb, nc),
        in_specs=in_specs,
        out_specs=ublk(0),
        out_shape=jax.ShapeDtypeStruct((batch, seq, w), BF16 if final else F32),
        scratch_shapes=[pltpu.VMEM((nb, HG_HEADS, LANES, LANES), F32)],
        compiler_params=_params("parallel", "arbitrary"),
        name="hgrn_scan_rev" if reverse else "hgrn_scan_fwd",
    )(*args)
    return out.reshape(n, w)


def hgrn_layer(x, g, lb, w_in, norm_g, w_out, batch, seq, final_g=None):
    u = proj_in(x, g, w_in.astype(BF16))
    o_rev = hgrn_scan(u, lb, batch, seq, reverse=True)
    o = hgrn_scan(u, lb, batch, seq, reverse=False, final_args=(o_rev, norm_g))
    return proj_out(o, w_out.astype(BF16), x, final_g)


def _rope_tables(seq):
    pos = np.arange(seq)
    rowp = (pos // GRID_W).astype(np.float64)
    colp = (pos % GRID_W).astype(np.float64)
    half = AT_HD // 4
    inv = ROPE_THETA ** (-np.arange(0, 2 * half, 2, dtype=np.float64) / (2 * half))
    ar, ac = rowp[:, None] * inv, colp[:, None] * inv
    cos = np.concatenate([np.cos(ar), np.cos(ac), np.cos(ar), np.cos(ac)], axis=1)
    sin = np.concatenate([-np.sin(ar), -np.sin(ac), np.sin(ar), np.sin(ac)], axis=1)
    return jnp.asarray(cos, F32), jnp.asarray(sin, F32)


def _pair_major(a, nheads):
    lead = a.shape[:-1]
    a = a.reshape(*lead, nheads, 2, 2, AT_HD // 4)
    return jnp.swapaxes(a, -3, -2).reshape(*lead, nheads * AT_HD)


def _gqa_proj_kernel(x_ref, g_ref, w_ref, cos_ref, sin_ref, qg_ref, kg_ref, qk_ref, vo_ref, go_ref, xn_ref, acc_ref,
                     *, tn, ncol):
    j = pl.program_id(1)
    nqk = (AT_HEADS + AT_KV) * AT_HD // tn
    jq = AT_HEADS * AT_HD // tn

    @pl.when(j == 0)
    def _():
        x = x_ref[...]
        ms = jnp.mean(x * x, axis=-1, keepdims=True)
        xn_ref[...] = (x * lax.rsqrt(ms + EPS) * g_ref[...]).astype(BF16)

    def heads(blk):
        cos, sin = cos_ref[...], sin_ref[...]
        gain = qg_ref[...] * (AT_HD ** -0.5 * math.log2(math.e)) if blk < jq else kg_ref[...]
        for h in range(tn // AT_HD):
            cols = slice(h * AT_HD, (h + 1) * AT_HD)
            xh = acc_ref[blk % 2, :, cols]
            ms = jnp.mean(xh * xh, axis=-1, keepdims=True)
            xn = xh * lax.rsqrt(ms + EPS) * gain
            qk_ref[:, cols] = (xn * cos + pltpu.roll(xn, AT_HD // 2, axis=1) * sin).astype(qk_ref.dtype)

    for jj in range(ncol + 1):
        @pl.when(j == jj)
        def _(jj=jj):
            if jj < ncol:
                acc_ref[jj % 2] = _dot(xn_ref[...], w_ref[...])
            prev = jj - 1
            if 0 <= prev < nqk:
                heads(prev)
            elif prev == nqk:
                vo_ref[...] = acc_ref[prev % 2].astype(vo_ref.dtype)
            elif prev > nqk:
                go_ref[...] = acc_ref[prev % 2].astype(go_ref.dtype)


def gqa_proj(x, g, w, q_g, k_g, seq, *, tm=1024, tn=1024):
    n, d = x.shape
    qw, kw = AT_HEADS * AT_HD, AT_KV * AT_HD
    tm = min(tm, seq)
    assert w.shape[1] == 2 * qw + 2 * kw and kw == tn and qw % tn == 0 and seq % tm == 0
    nb = seq // tm
    ncol = w.shape[1] // tn
    nqk = (qw + kw) // tn
    cos, sin = _rope_tables(seq)
    pos = lambda i, j: (i % nb, 0)
    const = lambda i, j: (0, 0)
    blk = lambda f: pl.BlockSpec((tm, tn), f)
    return pl.pallas_call(
        functools.partial(_gqa_proj_kernel, tn=tn, ncol=ncol),
        grid=(n // tm, ncol + 1),
        in_specs=[pl.BlockSpec((tm, d), lambda i, j: (i, 0)), pl.BlockSpec((1, d), const),
                  pl.BlockSpec((d, tn), lambda i, j: (0, jnp.minimum(j, ncol - 1))),
                  pl.BlockSpec((tm, AT_HD), pos), pl.BlockSpec((tm, AT_HD), pos),
                  pl.BlockSpec((1, AT_HD), const), pl.BlockSpec((1, AT_HD), const)],
        out_specs=[blk(lambda i, j: (i, jnp.clip(j - 1, 0, nqk - 1))), blk(lambda i, j: (i, 0)),
                   blk(lambda i, j: (i, jnp.clip(j - 2 - nqk, 0, qw // tn - 1)))],
        out_shape=[jax.ShapeDtypeStruct((n, qw + kw), BF16), jax.ShapeDtypeStruct((n, kw), BF16),
                   jax.ShapeDtypeStruct((n, qw), BF16)],
        scratch_shapes=[pltpu.VMEM((tm, d), BF16), pltpu.VMEM((2, tm, tn), F32)],
        compiler_params=_params("parallel", "arbitrary"),
        name="gqa_proj",
    )(x, g.reshape(1, d), w, cos, sin, q_g.reshape(1, AT_HD), k_g.reshape(1, AT_HD))


AT_SAFE_LOG2_RANGE = 100.0


def _gqa_flash_kernel(q_ref, k_ref, v_ref, gate_ref, o_ref, kmax_ref, *, tk):
    tq = q_ref.shape[0]
    seq = k_ref.shape[0]
    grp = AT_HEADS // AT_KV
    rows = grp * tq
    nkv = seq // tk

    @pl.when(pl.program_id(2) == 0)
    def _():
        kf = k_ref[...].astype(F32)
        k2 = jnp.max(jnp.sum(kf * kf, axis=-1, keepdims=True), axis=0, keepdims=True)
        kmax_ref[...] = jnp.broadcast_to(jnp.sqrt(k2), kmax_ref.shape)

    qs = jnp.concatenate([q_ref[:, j * AT_HD:(j + 1) * AT_HD] for j in range(grp)], axis=0)
    qf = qs.astype(F32)
    c = jnp.sqrt(jnp.sum(qf * qf, axis=-1, keepdims=True)) * kmax_ref[0:1, 0:1] * (1.0 + 2.0 ** -8)
    cmax = jnp.max(c)

    def finish(acc, l):
        o = acc * (1.0 / l)
        for j in range(grp):
            cols = slice(j * AT_HD, (j + 1) * AT_HD)
            o_ref[:, cols] = (o[j * tq:(j + 1) * tq, :] * _silu(gate_ref[:, cols])).astype(o_ref.dtype)

    @pl.when(2.0 * cmax <= AT_SAFE_LOG2_RANGE)
    def _():
        lvec = jnp.zeros((rows, LANES), F32)
        acc = jnp.zeros((rows, AT_HD), F32)
        for t in range(nkv):
            ks = k_ref[t * tk:(t + 1) * tk, :]
            vs = v_ref[t * tk:(t + 1) * tk, :]
            p = jnp.exp2(_dot_nt(qs, ks) - c)
            for w in range(tk // LANES):
                lvec = lvec + p[:, w * LANES:(w + 1) * LANES]
            acc = acc + _dot(p.astype(BF16), vs)
        finish(acc, jnp.sum(lvec, axis=-1, keepdims=True))

    @pl.when(2.0 * cmax > AT_SAFE_LOG2_RANGE)
    def _():
        def body(t, carry):
            m, l, acc = carry
            ks = k_ref[pl.ds(t * tk, tk), :]
            vs = v_ref[pl.ds(t * tk, tk), :]
            s = _dot_nt(qs, ks)
            m_new = jnp.maximum(m, jnp.max(s, axis=-1, keepdims=True))
            alpha = jnp.exp2(m - m_new)
            p = jnp.exp2(s - m_new)
            l = alpha * l + jnp.sum(p, axis=-1, keepdims=True)
            acc = alpha * acc + _dot(p.astype(BF16), vs)
            return m_new, l, acc

        init = (jnp.full((rows, 1), -jnp.inf, F32), jnp.zeros((rows, 1), F32), jnp.zeros((rows, AT_HD), F32))
        _, l, acc = lax.fori_loop(0, nkv, body, init)
        finish(acc, l)


def gqa_flash(qk, v, gate, batch, seq, *, tq=512, tk=512):
    n = qk.shape[0]
    grp = AT_HEADS // AT_KV
    gw = grp * AT_HD
    tq = min(tq, seq)
    nq = seq // tq
    return pl.pallas_call(
        functools.partial(_gqa_flash_kernel, tk=min(tk, seq)),
        grid=(batch, AT_KV, nq),
        in_specs=[pl.BlockSpec((tq, gw), lambda b, h, i: (b * nq + i, h)),
                  pl.BlockSpec((seq, AT_HD), lambda b, h, i: (b, AT_HEADS + h)),
                  pl.BlockSpec((seq, AT_HD), lambda b, h, i: (b, h)),
                  pl.BlockSpec((tq, gw), lambda b, h, i: (b * nq + i, h))],
        out_specs=pl.BlockSpec((tq, gw), lambda b, h, i: (b * nq + i, h)),
        out_shape=jax.ShapeDtypeStruct((n, AT_HEADS * AT_HD), BF16),
        scratch_shapes=[pltpu.VMEM((SUBLANES, LANES), F32)],
        compiler_params=_params("parallel", "parallel", "arbitrary"),
        name="gqa_flash",
    )(qk, qk, v, gate)


def gqa_layer(x, g, w_in, q_g, k_g, w_out, batch, seq, final_g=None):
    nqk = AT_HEADS + AT_KV
    w = jnp.concatenate([_pair_major(w_in[:, :nqk * AT_HD], nqk), w_in[:, nqk * AT_HD:]], axis=1).astype(BF16)
    qk, v, gate = gqa_proj(x, g, w, _pair_major(q_g, 1), _pair_major(k_g, 1), seq)
    o = gqa_flash(qk, v, gate, batch, seq)
    return proj_out(o, w_out.astype(BF16), x, final_g)


def _t5_bucket_np(rel):
    half = REL_BUCKETS // 2
    exact = half // 2
    nabs = np.abs(rel)
    large = exact + (np.log(np.maximum(nabs, 1).astype(np.float32) / exact)
                     / math.log(REL_MAX_DIST / exact) * (half - exact)).astype(np.int32)
    large = np.minimum(large, half - 1)
    return np.where(rel > 0, half, 0) + np.where(nabs < exact, nabs, large)


DL_TQ = 128
DL_HALO = 64
DL_NK = DL_TQ + 2 * DL_HALO
DL_BATCH = 4


def _dilated_buckets():
    qi = np.arange(DL_TQ)[:, None]
    kj = np.arange(DL_NK)[None, :]
    tabs = [_t5_bucket_np((kj - DL_HALO - qi) * dil).astype(np.int32) for _, dil in DL_PAIRS]
    band = np.abs(kj - DL_HALO - qi) <= DL_HALO
    present = [sorted(set(t[band].tolist())) for t in tabs]
    return np.stack(tabs), present


def _dilated_kernel(relb_ref, bucket_ref, *refs, seq, tok, present):
    ng = len(DL_PAIRS)
    io = refs[:7 * ng]
    gate_ref, out_ref, bias_ref, kbuf, vbuf, obuf, lbuf = refs[7 * ng:]
    i = pl.program_id(1)
    p = pl.program_id(2)
    qi = lax.broadcasted_iota(jnp.int32, (DL_TQ, DL_NK), 0)
    kj = lax.broadcasted_iota(jnp.int32, (DL_TQ, DL_NK), 1)
    band = jnp.abs(kj - DL_HALO - qi) <= DL_HALO
    lane = lax.broadcasted_iota(jnp.int32, (DL_TQ, LANES), 1)
    lo = lane < DL_HD
    log2e = math.log2(math.e)
    scale = DL_HD ** -0.5 * log2e

    @pl.when((pl.program_id(0) == 0) & (i == 0) & (p == 0))
    def _():
        for g in range(ng):
            bk = bucket_ref[g]

            def fill(h, carry, g=g, bk=bk):
                acc = jnp.zeros((DL_TQ, DL_NK), F32)
                for b in present[g]:
                    acc = jnp.where(bk == b, relb_ref[b, h], acc)
                bias_ref[g, h] = jnp.where(band, acc * log2e, NEG_BIG)
                return carry

            lax.fori_loop(0, DL_HEADS, fill, 0)

    for g, (_, dil) in enumerate(DL_PAIRS):
        q_ref, kp_ref, kc_ref, kn_ref, vp_ref, vc_ref, vn_ref = io[7 * g:7 * g + 7]
        halo = DL_HALO * dil
        ls = seq // dil
        ppb = tok // dil
        nsub = ppb // DL_TQ
        kbuf[0:halo, :] = kp_ref[...]
        kbuf[halo:halo + tok, :] = kc_ref[...]
        kbuf[halo + tok:2 * halo + tok, :] = kn_ref[...]
        vbuf[0:halo, :] = vp_ref[...]
        vbuf[halo:halo + tok, :] = vc_ref[...]
        vbuf[halo + tok:2 * halo + tok, :] = vn_ref[...]

        def blocks(it, carry, g=g, dil=dil, ls=ls, ppb=ppb, nsub=nsub, q_ref=q_ref):
            ld = []
            for t in range(DL_BATCH):
                idx = it * DL_BATCH + t
                r = idx // nsub
                j = idx % nsub
                start = r + j * (DL_TQ * dil)
                if dil == 1:
                    qrows, krows = pl.ds(start, DL_TQ), pl.ds(start, DL_NK)
                else:
                    qrows, krows = pl.ds(start, DL_TQ, stride=dil), pl.ds(start, DL_NK, stride=dil)
                q2 = q_ref[qrows, :] * scale
                qh = [jnp.where(lo if half == 0 else ~lo, q2, 0.0).astype(BF16) for half in range(2)]
                mk = i * ppb + j * DL_TQ - DL_HALO + kj
                ld.append((qrows, qh, kbuf[krows, :].astype(BF16), vbuf[krows, :].astype(BF16),
                           (mk >= 0) & (mk < ls)))
            scores = [[_dot_nt(qh, k2) for qh in qhs] for _, qhs, k2, _, _ in ld]
            soft = []
            for t in range(DL_BATCH):
                inside = ld[t][4]
                per_head = []
                for half in range(2):
                    s = jnp.where(inside, scores[t][half] + bias_ref[g, 2 * p + half], NEG_BIG)
                    m = jnp.max(s, axis=-1, keepdims=True)
                    pexp = jnp.exp2(s - m)
                    per_head.append((pexp.astype(BF16), m, jnp.sum(pexp, axis=-1, keepdims=True)))
                soft.append(per_head)
            pv = [[_dot(pb, ld[t][3]) for pb, _, _ in soft[t]] for t in range(DL_BATCH)]
            for t in range(DL_BATCH):
                qrows = ld[t][0]
                outs = [pv[t][half] * (1.0 / soft[t][half][2]) for half in range(2)]
                lses = [soft[t][half][1] + jnp.log2(soft[t][half][2]) for half in range(2)]
                obuf[g, qrows, :] = jnp.where(lo, outs[0], outs[1])
                lbuf[g, qrows, :] = jnp.where(lo, lses[0], lses[1])
            return carry

        lax.fori_loop(0, dil * nsub // DL_BATCH, blocks, 0)

    la, lb, lc = lbuf[0], lbuf[1], lbuf[2]
    m = jnp.maximum(jnp.maximum(la, lb), lc)
    ea, eb, ec = jnp.exp2(la - m), jnp.exp2(lb - m), jnp.exp2(lc - m)
    o = (ea * obuf[0] + eb * obuf[1] + ec * obuf[2]) / (ea + eb + ec)
    out_ref[...] = (o * _silu(gate_ref[...])).astype(out_ref.dtype)


def dilated_attention(u, rel_bias, batch, seq, *, tok=2048):
    n, win = u.shape
    hw = DL_HEADS * DL_HD
    npair = hw // LANES
    tok = min(tok, seq)
    nblk = seq // tok
    max_halo = DL_HALO * max(d for _, d in DL_PAIRS)
    assert tok % (DL_TQ * max(d for _, d in DL_PAIRS)) == 0 and tok % max_halo == 0
    buckets, present = _dilated_buckets()

    in_specs = [pl.BlockSpec(memory_space=pltpu.SMEM),
                pl.BlockSpec((len(DL_PAIRS), DL_TQ, DL_NK), lambda b, i, p: (0, 0, 0))]
    args = [rel_bias, jnp.asarray(buckets)]
    for gi, (_, dil) in enumerate(DL_PAIRS):
        halo = DL_HALO * dil
        hpb = tok // halo
        last = n // halo - 1

        def col(c, gi=gi):
            return lambda p: (gi * 3 + c) * npair + p

        def cur(c):
            cf = col(c)
            return pl.BlockSpec((tok, LANES), lambda b, i, p: (b * nblk + i, cf(p)))

        def prev(c, hpb=hpb):
            cf = col(c)
            return pl.BlockSpec((halo, LANES), lambda b, i, p: (jnp.maximum((b * nblk + i) * hpb - 1, 0), cf(p)))

        def nxt(c, hpb=hpb, last=last):
            cf = col(c)
            return pl.BlockSpec((halo, LANES),
                                lambda b, i, p: (jnp.minimum((b * nblk + i + 1) * hpb, last), cf(p)))

        in_specs += [cur(0), prev(1), cur(1), nxt(1), prev(2), cur(2), nxt(2)]
        args += [u] * 7
    gate_col0 = 3 * len(DL_PAIRS) * npair
    in_specs.append(pl.BlockSpec((tok, LANES), lambda b, i, p: (b * nblk + i, gate_col0 + p)))
    args.append(u)
    return pl.pallas_call(
        functools.partial(_dilated_kernel, seq=seq, tok=tok, present=present),
        grid=(batch, nblk, npair),
        in_specs=in_specs,
        out_specs=pl.BlockSpec((tok, LANES), lambda b, i, p: (b * nblk + i, p)),
        out_shape=jax.ShapeDtypeStruct((n, hw), BF16),
        scratch_shapes=[pltpu.VMEM((len(DL_PAIRS), DL_HEADS, DL_TQ, DL_NK), F32),
                        pltpu.VMEM((tok + 2 * max_halo, LANES), F32),
                        pltpu.VMEM((tok + 2 * max_halo, LANES), F32),
                        pltpu.VMEM((len(DL_PAIRS), tok, LANES), F32),
                        pltpu.VMEM((len(DL_PAIRS), tok, LANES), F32)],
        compiler_params=_params("arbitrary", "arbitrary", "arbitrary"),
        name="dilated_attention",
    )(*args)


def dilated_layer(x, g, rel_bias, w_in, w_out, batch, seq, final_g=None):
    u = proj_in(x, g, w_in.astype(BF16))
    o = dilated_attention(u, rel_bias, batch, seq)
    return proj_out(o, w_out.astype(BF16), x, final_g)


def kernel(x, norm_g, final_g, rel_bias, hgrn_lb, ssd_w_in, ssd_conv_w, ssd_conv_b, ssd_dt_bias, ssd_a_log, ssd_d,
           ssd_norm_g, ssd_w_out, hg_w_in, hg_norm_g, hg_w_out, at_w_in, at_q_norm_g, at_k_norm_g, at_w_out,
           dl_w_in, dl_w_out):
    batch, seq, d = x.shape
    depth = norm_g.shape[0]
    n_mixers = 4
    lb_sm = jax.nn.softmax(hgrn_lb.astype(F32), axis=0)
    lb_all = jnp.cumsum(lb_sm, axis=0) - lb_sm[0:1]
    h = x.reshape(batch * seq, d)
    for layer in range(depth):
        kind, slot = layer % n_mixers, layer // n_mixers
        fg = final_g if layer == depth - 1 else None
        if kind == 0:
            h = ssd_layer(h, norm_g[layer], ssd_w_in[slot], ssd_conv_w[slot], ssd_conv_b[slot], ssd_dt_bias[slot],
                          ssd_a_log[slot], ssd_d[slot], ssd_norm_g[slot], ssd_w_out[slot], batch, seq, fg)
        elif kind == 1:
            h = hgrn_layer(h, norm_g[layer], lb_all[layer], hg_w_in[slot], hg_norm_g[slot], hg_w_out[slot],
                           batch, seq, fg)
        elif kind == 2:
            h = gqa_layer(h, norm_g[layer], at_w_in[slot], at_q_norm_g[slot], at_k_norm_g[slot], at_w_out[slot],
                          batch, seq, fg)
        else:
            h = dilated_layer(h, norm_g[layer], rel_bias, dl_w_in[slot], dl_w_out[slot], batch, seq, fg)
    return h.reshape(batch, seq, d)
```

```python
import functools
import math

import jax
import jax.numpy as jnp
import numpy as np
from jax import lax
from jax.experimental import pallas as pl
from jax.experimental.pallas import tpu as pltpu

F32 = jnp.float32
BF16 = jnp.bfloat16

EPS = 1e-6
NEG_BIG = -1e30
GRID_W = 64
ROPE_THETA = 10000.0

SSD_HEADDIM = 64
SSD_HEADS = 32
SSD_GROUPS = 4
SSD_STATE = 128
SSD_CONV = 7
HG_HEADS = 8
HG_SUB = 32
AT_HEADS = 16
AT_KV = 8
AT_HD = 128
DL_PAIRS = ((128, 1), (512, 4), (2048, 16))
DL_HEADS = 16
DL_HD = 64
REL_BUCKETS = 32
REL_MAX_DIST = 1024

LANES = 128
SUBLANES = 8
CHUNK = 128
VMEM_LIMIT = 56 * 1024 * 1024


def _params(*sem):
    return pltpu.CompilerParams(dimension_semantics=sem, vmem_limit_bytes=VMEM_LIMIT)


def _sigmoid(x):
    return 0.5 * jnp.tanh(0.5 * x) + 0.5


def _silu(x):
    return x * _sigmoid(x)


def _softplus(x):
    return jnp.maximum(x, 0.0) + jnp.log(1.0 + jnp.exp(-jnp.abs(x)))


def _dot(a, b):
    return jnp.dot(a, b, preferred_element_type=F32)


def _dot_nt(a, b):
    return lax.dot_general(a, b, (((1,), (1,)), ((), ())), preferred_element_type=F32)


def _dot_tn(a, b):
    return lax.dot_general(a, b, (((0,), (0,)), ((), ())), preferred_element_type=F32)


def _prefix_sum(tri, x):
    x1 = x.astype(BF16)
    r1 = x - x1.astype(F32)
    x2 = r1.astype(BF16)
    x3 = (r1 - x2.astype(F32)).astype(BF16)
    return _dot(tri, x1) + _dot(tri, x2) + _dot(tri, x3)


def _proj_in_kernel(x_ref, g_ref, w_ref, o_ref, xn_ref):
    @pl.when(pl.program_id(1) == 0)
    def _():
        x = x_ref[...]
        ms = jnp.mean(x * x, axis=-1, keepdims=True)
        xn_ref[...] = (x * lax.rsqrt(ms + EPS) * g_ref[...]).astype(BF16)

    o_ref[...] = _dot(xn_ref[...], w_ref[...]).astype(o_ref.dtype)


def proj_in(x, g, w, *, tm=2048, tn=1024, out_dtype=F32):
    n, d = x.shape
    dout = w.shape[1]
    tm = min(tm, n)
    tn = min(tn, dout)
    assert n % tm == 0 and dout % tn == 0
    return pl.pallas_call(
        _proj_in_kernel,
        grid=(n // tm, dout // tn),
        in_specs=[pl.BlockSpec((tm, d), lambda i, j: (i, 0)),
                  pl.BlockSpec((1, d), lambda i, j: (0, 0)),
                  pl.BlockSpec((d, tn), lambda i, j: (0, j))],
        out_specs=pl.BlockSpec((tm, tn), lambda i, j: (i, j)),
        out_shape=jax.ShapeDtypeStruct((n, dout), out_dtype),
        scratch_shapes=[pltpu.VMEM((tm, d), BF16)],
        compiler_params=_params("parallel", "arbitrary"),
        name="proj_in",
    )(x, g.reshape(1, d), w)


def _proj_out_kernel(a_ref, w_ref, r_ref, g_ref, o_ref, *, final):
    y = r_ref[...] + _dot(a_ref[...], w_ref[...])
    if final:
        ms = jnp.mean(y * y, axis=-1, keepdims=True)
        y = y * lax.rsqrt(ms + EPS) * g_ref[...]
    o_ref[...] = y


def proj_out(a, w, res, final_g=None, *, tm=512):
    n, k = a.shape
    d = w.shape[1]
    tm = min(tm, n)
    assert n % tm == 0
    g = jnp.ones((1, d), F32) if final_g is None else final_g.reshape(1, d)
    return pl.pallas_call(
        functools.partial(_proj_out_kernel, final=final_g is not None),
        grid=(n // tm,),
        in_specs=[pl.BlockSpec((tm, k), lambda i: (i, 0)),
                  pl.BlockSpec((k, d), lambda i: (0, 0)),
                  pl.BlockSpec((tm, d), lambda i: (i, 0)),
                  pl.BlockSpec((1, d), lambda i: (0, 0))],
        out_specs=pl.BlockSpec((tm, d), lambda i: (i, 0)),
        out_shape=jax.ShapeDtypeStruct((n, d), F32),
        compiler_params=_params("parallel"),
        name="proj_out",
    )(a, w, res, g)


def _ssd_conv_kernel(xp_ref, xc_ref, xn_ref, bp_ref, bc_ref, bn_ref, wx_ref, wb_ref, bx_ref, bb_ref,
                     ox_ref, ob_ref, *, nblk):
    i = pl.program_id(1)
    pad = SSD_CONV // 2

    def conv(prev_ref, cur_ref, next_ref, w_ref, b_ref, o_ref):
        rows = cur_ref.shape[0]
        prev = jnp.where(i > 0, prev_ref[...], 0.0)
        nxt = jnp.where(i < nblk - 1, next_ref[...], 0.0)
        ext = jnp.concatenate([prev, cur_ref[...], nxt], axis=0)
        acc = jnp.zeros(cur_ref.shape, F32) + b_ref[...]
        total = rows + 2 * SUBLANES
        for t in range(SSD_CONV):
            shifted = ext if t == pad else pltpu.roll(ext, (pad - t) % total, axis=0)
            acc = acc + shifted[SUBLANES:SUBLANES + rows, :] * w_ref[t:t + 1, :]
        o_ref[...] = _silu(acc)

    conv(xp_ref, xc_ref, xn_ref, wx_ref, bx_ref, ox_ref)
    conv(bp_ref, bc_ref, bn_ref, wb_ref, bb_ref, ob_ref)


def ssd_conv(u, conv_w, conv_b, batch, seq, *, tc=256):
    n = u.shape[0]
    di = SSD_HEADS * SSD_HEADDIM
    gn2 = 2 * SSD_GROUPS * SSD_STATE
    tc = min(tc, seq)
    nblk = seq // tc
    r8 = tc // SUBLANES
    last8 = n // SUBLANES - 1

    def cur(wblk):
        return lambda b, i: (b * nblk + i, wblk)

    def prev(wblk):
        return lambda b, i: (jnp.maximum((b * nblk + i) * r8 - 1, 0), wblk)

    def nxt(wblk):
        return lambda b, i: (jnp.minimum((b * nblk + i + 1) * r8, last8), wblk)

    wx, wb = conv_w[:, :di], conv_w[:, di:]
    bx, bb = conv_b[:di].reshape(1, di), conv_b[di:].reshape(1, gn2)
    const = lambda b, i: (0, 0)
    return pl.pallas_call(
        functools.partial(_ssd_conv_kernel, nblk=nblk),
        grid=(batch, nblk),
        in_specs=[pl.BlockSpec((SUBLANES, di), prev(1)), pl.BlockSpec((tc, di), cur(1)),
                  pl.BlockSpec((SUBLANES, di), nxt(1)),
                  pl.BlockSpec((SUBLANES, gn2), prev(4)), pl.BlockSpec((tc, gn2), cur(4)),
                  pl.BlockSpec((SUBLANES, gn2), nxt(4)),
                  pl.BlockSpec((SSD_CONV, di), const), pl.BlockSpec((SSD_CONV, gn2), const),
                  pl.BlockSpec((1, di), const), pl.BlockSpec((1, gn2), const)],
        out_specs=[pl.BlockSpec((tc, di), lambda b, i: (b * nblk + i, 0)),
                   pl.BlockSpec((tc, gn2), lambda b, i: (b * nblk + i, 0))],
        out_shape=[jax.ShapeDtypeStruct((n, di), F32), jax.ShapeDtypeStruct((n, gn2), F32)],
        compiler_params=_params("parallel", "parallel"),
        name="ssd_conv",
    )(u, u, u, u, u, u, wx, wb, bx, bb)


def _ssd_scan_kernel(x_ref, bc_ref, dtr_ref, dtb_ref, alog_ref, *rest, reverse, final):
    if final:
        z_ref, yo_ref, dskip_ref, ng_ref, o_ref, st_ref = rest
    else:
        o_ref, st_ref = rest
    c = pl.program_id(1)
    nb = x_ref.shape[0]
    q = CHUNK
    gn = SSD_GROUPS * SSD_STATE
    hpg = SSD_HEADS // SSD_GROUPS
    gw = hpg * SSD_HEADDIM
    hoff = SSD_HEADS if reverse else 0
    far = 0 if reverse else q - 1

    @pl.when(c == 0)
    def _():
        st_ref[...] = jnp.zeros(st_ref.shape, F32)

    row = lax.broadcasted_iota(jnp.int32, (q, q), 0)
    col = lax.broadcasted_iota(jnp.int32, (q, q), 1)
    valid = (col >= row) if reverse else (col <= row)
    tri = valid.astype(BF16)
    lane = lax.broadcasted_iota(jnp.int32, (q, LANES), 1)
    lo = lane < SSD_HEADDIM
    lo_row = lo[0:1, :]
    neg_a = -jnp.exp(alog_ref[...])

    prep = []
    for bb in range(nb):
        dt = _softplus(dtr_ref[bb] + dtb_ref[...])
        cs_col = _prefix_sum(tri, dt * neg_a)
        cs_row = cs_col.T
        dt_row = dt.T
        w_row = jnp.exp(cs_row[:, far:far + 1] - cs_row) * dt_row
        dec = jnp.exp(cs_col[far:far + 1, :])
        prep.append((cs_col, cs_row, dt_row, w_row, dec))

    for g in range(SSD_GROUPS):
        grp = []
        for bb in range(nb):
            b_f = bc_ref[bb, :, g * SSD_STATE:(g + 1) * SSD_STATE]
            c_f = bc_ref[bb, :, gn + g * SSD_STATE:gn + (g + 1) * SSD_STATE]
            c_b = c_f.astype(BF16)
            grp.append((_dot_nt(c_b, b_f.astype(BF16)).astype(BF16), c_b, b_f.T.astype(BF16)))
        y_parts = [[] for _ in range(nb)]
        for pp in range(hpg // 2):
            h0 = g * hpg + 2 * pp
            cols = slice(h0 * SSD_HEADDIM, (h0 + 2) * SSD_HEADDIM)
            scols = slice(2 * pp * SSD_HEADDIM, (2 * pp + 2) * SSD_HEADDIM)
            ops = []
            for bb in range(nb):
                cs_col, cs_row, dt_row, w_row, dec = prep[bb]
                cb, c_f, b_t = grp[bb]
                x2 = x_ref[bb, :, cols]
                x2b = x2.astype(BF16)
                s_old = st_ref[bb, g, :, scols]
                rhs = jnp.concatenate([x2b, s_old.astype(BF16)], axis=0)
                lhs, bws = [], []
                for h in (h0, h0 + 1):
                    hl = hoff + h
                    colb = jnp.broadcast_to(cs_col[:, hl:hl + 1], (q, q))
                    lmat = jnp.exp(jnp.where(valid, colb - cs_row[hl:hl + 1, :], NEG_BIG))
                    m = cb * (lmat.astype(BF16) * dt_row[hl:hl + 1, :].astype(BF16))
                    ce = c_f * jnp.exp(colb).astype(BF16)
                    lhs.append(jnp.concatenate([m, ce], axis=1))
                    bws.append(b_t * w_row[hl:hl + 1, :].astype(BF16))
                dec2 = jnp.where(lo_row, dec[:, hoff + h0:hoff + h0 + 1], dec[:, hoff + h0 + 1:hoff + h0 + 2])
                ops.append((x2, x2b, s_old, rhs, lhs, bws, dec2))
            prods = []
            for bb in range(nb):
                _, x2b, _, rhs, lhs, bws, _ = ops[bb]
                prods.append(([_dot(l, rhs) for l in lhs], [_dot(w, x2b) for w in bws]))
            for bb in range(nb):
                x2, _, s_old, _, _, _, dec2 = ops[bb]
                ys, sts = prods[bb]
                y2 = jnp.where(lo, ys[0], ys[1])
                st_ref[bb, g, :, scols] = s_old * dec2 + jnp.where(lo, sts[0], sts[1])
                if final:
                    y2 = y2 + yo_ref[bb, :, cols] + x2 * dskip_ref[:, cols]
                    y_parts[bb].append(y2 * _silu(z_ref[bb, :, cols]))
                else:
                    o_ref[bb, :, cols] = y2
        if final:
            gcols = slice(g * gw, (g + 1) * gw)
            for bb in range(nb):
                yg = jnp.concatenate(y_parts[bb], axis=1)
                ms = jnp.mean(yg * yg, axis=-1, keepdims=True)
                o_ref[bb, :, gcols] = (yg * lax.rsqrt(ms + EPS) * ng_ref[:, gcols]).astype(o_ref.dtype)


def ssd_scan(xc, bc, dtr, dt_bias, a_log, batch, seq, *, reverse, final_args=None, nb=4):
    n, di = xc.shape
    nc = seq // CHUNK
    gn2 = bc.shape[1]
    final = final_args is not None
    nb = min(nb, batch)
    assert batch % nb == 0
    v3 = lambda a: a.reshape(batch, seq, a.shape[1])

    def chunk(c):
        return nc - 1 - c if reverse else c

    blk = lambda w: pl.BlockSpec((nb, CHUNK, w), lambda b, c: (b, chunk(c), 0))
    const = lambda w: pl.BlockSpec((1, w), lambda b, c: (0, 0))
    in_specs = [blk(di), blk(gn2), blk(LANES), const(LANES), const(LANES)]
    args = [v3(xc), v3(bc), v3(dtr), dt_bias, a_log]
    if final:
        u, y_other, dskip, ng = final_args
        in_specs += [blk(di), blk(di), const(di), const(di)]
        args += [v3(u), v3(y_other), dskip, ng]
    out = pl.pallas_call(
        functools.partial(_ssd_scan_kernel, reverse=reverse, final=final),
        grid=(batch // nb, nc),
        in_specs=in_specs,
        out_specs=blk(di),
        out_shape=jax.ShapeDtypeStruct((batch, seq, di), BF16 if final else F32),
        scratch_shapes=[pltpu.VMEM((nb, SSD_GROUPS, SSD_STATE, (SSD_HEADS // SSD_GROUPS) * SSD_HEADDIM), F32)],
        compiler_params=_params("parallel", "arbitrary"),
        name="ssd_scan_rev" if reverse else "ssd_scan_fwd",
    )(*args)
    return out.reshape(n, di)


def ssd_layer(x, g, w_in, conv_w, conv_b, dt_bias, a_log, d_skip, norm_g, w_out, batch, seq, final_g=None):
    di = SSD_HEADS * SSD_HEADDIM
    main = 2 * di + 2 * SSD_GROUPS * SSD_STATE
    w_main = w_in[:, :main].astype(BF16)
    w_dt = jnp.pad(w_in[:, main:], ((0, 0), (0, LANES - 2 * SSD_HEADS))).astype(BF16)
    u = proj_in(x, g, w_main)
    dtr = proj_in(x, g, w_dt)
    xc, bc = ssd_conv(u, conv_w, conv_b, batch, seq)
    pad = lambda v: jnp.pad(v.reshape(1, 2 * SSD_HEADS), ((0, 0), (0, LANES - 2 * SSD_HEADS)))
    dtb, alog = pad(dt_bias), pad(a_log)
    y_rev = ssd_scan(xc, bc, dtr, dtb, alog, batch, seq, reverse=True)
    dskip = jnp.repeat(d_skip, SSD_HEADDIM).reshape(1, di)
    y = ssd_scan(xc, bc, dtr, dtb, alog, batch, seq, reverse=False,
                 final_args=(u, y_rev, dskip, norm_g.reshape(1, di)))
    return proj_out(y, w_out.astype(BF16), x, final_g)


def _hgrn_scan_kernel(q_ref, f_ref, v_ref, lb_ref, *rest, reverse, final):
    if final:
        gate_ref, oo_ref, ng_ref, o_ref, st_ref = rest
    else:
        o_ref, st_ref = rest
    c = pl.program_id(1)
    n = CHUNK
    nsub = n // HG_SUB
    dk = LANES

    @pl.when(c == 0)
    def _():
        st_ref[...] = jnp.zeros(st_ref.shape, F32)

    nb = q_ref.shape[0]
    lb = lb_ref[...]
    row = lax.broadcasted_iota(jnp.int32, (n, n), 0)
    col = lax.broadcasted_iota(jnp.int32, (n, n), 1)
    valid = (col >= row) if reverse else (col <= row)
    tri = valid.astype(BF16)
    ref_row = HG_SUB // 2 - 1 if reverse else HG_SUB // 2
    far = 0 if reverse else n - 1

    prep = []
    for bb in range(nb):
        f = lb + (1.0 - lb) * _sigmoid(f_ref[bb])
        prep.append((_silu(q_ref[bb]), 1.0 - f, _prefix_sum(tri, jnp.log(f))))

    for h in range(HG_HEADS):
        cols = slice(h * dk, (h + 1) * dk)
        staged = []
        for bb in range(nb):
            qa, ka, gsum = prep[bb]
            gh, qh, kh = gsum[:, cols], qa[:, cols], ka[:, cols]
            tot = gh[far:far + 1, :]
            qp, kn, anchors = [], [], []
            for s in range(nsub):
                rs = slice(s * HG_SUB, (s + 1) * HG_SUB)
                a = gh[s * HG_SUB + ref_row:s * HG_SUB + ref_row + 1, :]
                anchors.append(a)
                qp.append((qh[rs] * jnp.exp(gh[rs] - a)).astype(BF16))
                kn.append((kh[rs] * jnp.exp(a - gh[rs])).astype(BF16))
            lhs_cols = []
            for j in range(nsub):
                lhs = []
                for i in range(nsub):
                    live = (i <= j) if reverse else (i >= j)
                    if not live:
                        lhs.append(jnp.zeros((HG_SUB, dk), BF16))
                    elif i == j:
                        lhs.append(qp[i])
                    else:
                        lhs.append(qp[i] * jnp.exp(anchors[i] - anchors[j]).astype(BF16))
                lhs_cols.append(jnp.concatenate(lhs, axis=0))
            qe = jnp.concatenate([qp[s] * jnp.exp(anchors[s]).astype(BF16) for s in range(nsub)], axis=0)
            kd = jnp.concatenate([kn[s] * jnp.exp(tot - anchors[s]).astype(BF16) for s in range(nsub)], axis=0)
            staged.append((lhs_cols, kn, qe, kd, jnp.exp(tot)))
        atts = []
        for bb in range(nb):
            lhs_cols, kn_b = staged[bb][0], staged[bb][1]
            att = jnp.concatenate([_dot_nt(lhs_cols[j], kn_b[j]) for j in range(nsub)], axis=1)
            atts.append(jnp.where(valid, att, 0.0).astype(BF16))
        for bb in range(nb):
            _, _, qe, kd, decay = staged[bb]
            vh = v_ref[bb, :, cols].astype(BF16)
            s_old = st_ref[bb, h]
            o = _dot(atts[bb], vh) + _dot_nt(qe, s_old.astype(BF16))
            st_ref[bb, h] = s_old * decay + _dot_tn(vh, kd)
            if final:
                o = o + oo_ref[bb, :, cols]
                ms = jnp.mean(o * o, axis=-1, keepdims=True)
                o = o * lax.rsqrt(ms + EPS) * ng_ref[:, cols] * _silu(gate_ref[bb, :, cols])
            o_ref[bb, :, cols] = o.astype(o_ref.dtype)


def hgrn_scan(u, lb, batch, seq, *, reverse, final_args=None, nb=4):
    n = u.shape[0]
    w = HG_HEADS * LANES
    nc = seq // CHUNK
    final = final_args is not None
    nb = min(nb, batch)
    assert batch % nb == 0
    v3 = lambda a: a.reshape(batch, seq, a.shape[1])

    def chunk(c):
        return nc - 1 - c if reverse else c

    ublk = lambda j: pl.BlockSpec((nb, CHUNK, w), lambda b, c: (b, chunk(c), j))
    const = pl.BlockSpec((1, w), lambda b, c: (0, 0))
    u3 = v3(u)
    in_specs = [ublk(0), ublk(2 if reverse else 1), ublk(3), const]
    args = [u3, u3, u3, lb.reshape(1, w)]
    if final:
        o_other, ng = final_args
        in_specs += [ublk(4), ublk(0), const]
        args += [u3, v3(o_other), ng.reshape(1, w)]
    out = pl.pallas_call(
        functools.partial(_hgrn_scan_kernel, reverse=reverse, final=final),
        grid=(batch // nb, nc),
        in_specs=in_specs,
        out_specs=ublk(0),
        out_shape=jax.ShapeDtypeStruct((batch, seq, w), BF16 if final else F32),
        scratch_shapes=[pltpu.VMEM((nb, HG_HEADS, LANES, LANES), F32)],
        compiler_params=_params("parallel", "arbitrary"),
        name="hgrn_scan_rev" if reverse else "hgrn_scan_fwd",
    )(*args)
    return out.reshape(n, w)


def hgrn_layer(x, g, lb, w_in, norm_g, w_out, batch, seq, final_g=None):
    u = proj_in(x, g, w_in.astype(BF16))
    o_rev = hgrn_scan(u, lb, batch, seq, reverse=True)
    o = hgrn_scan(u, lb, batch, seq, reverse=False, final_args=(o_rev, norm_g))
    return proj_out(o, w_out.astype(BF16), x, final_g)


def _rope_tables(seq):
    pos = np.arange(seq)
    rowp = (pos // GRID_W).astype(np.float64)
    colp = (pos % GRID_W).astype(np.float64)
    half = AT_HD // 4
    inv = ROPE_THETA ** (-np.arange(0, 2 * half, 2, dtype=np.float64) / (2 * half))
    ar, ac = rowp[:, None] * inv, colp[:, None] * inv
    cos = np.concatenate([np.cos(ar), np.cos(ac), np.cos(ar), np.cos(ac)], axis=1)
    sin = np.concatenate([-np.sin(ar), -np.sin(ac), np.sin(ar), np.sin(ac)], axis=1)
    return jnp.asarray(cos, F32), jnp.asarray(sin, F32)


def _pair_major(a, nheads):
    lead = a.shape[:-1]
    a = a.reshape(*lead, nheads, 2, 2, AT_HD // 4)
    return jnp.swapaxes(a, -3, -2).reshape(*lead, nheads * AT_HD)


def _gqa_proj_kernel(x_ref, g_ref, w_ref, cos_ref, sin_ref, qg_ref, kg_ref, qk_ref, vo_ref, go_ref, xn_ref, acc_ref,
                     *, tn, ncol):
    j = pl.program_id(1)
    nqk = (AT_HEADS + AT_KV) * AT_HD // tn
    jq = AT_HEADS * AT_HD // tn

    @pl.when(j == 0)
    def _():
        x = x_ref[...]
        ms = jnp.mean(x * x, axis=-1, keepdims=True)
        xn_ref[...] = (x * lax.rsqrt(ms + EPS) * g_ref[...]).astype(BF16)

    def heads(blk):
        cos, sin = cos_ref[...], sin_ref[...]
        gain = qg_ref[...] * (AT_HD ** -0.5 * math.log2(math.e)) if blk < jq else kg_ref[...]
        for h in range(tn // AT_HD):
            cols = slice(h * AT_HD, (h + 1) * AT_HD)
            xh = acc_ref[blk % 2, :, cols]
            ms = jnp.mean(xh * xh, axis=-1, keepdims=True)
            xn = xh * lax.rsqrt(ms + EPS) * gain
            qk_ref[:, cols] = (xn * cos + pltpu.roll(xn, AT_HD // 2, axis=1) * sin).astype(qk_ref.dtype)

    for jj in range(ncol + 1):
        @pl.when(j == jj)
        def _(jj=jj):
            if jj < ncol:
                acc_ref[jj % 2] = _dot(xn_ref[...], w_ref[...])
            prev = jj - 1
            if 0 <= prev < nqk:
                heads(prev)
            elif prev == nqk:
                vo_ref[...] = acc_ref[prev % 2].astype(vo_ref.dtype)
            elif prev > nqk:
                go_ref[...] = acc_ref[prev % 2].astype(go_ref.dtype)


def gqa_proj(x, g, w, q_g, k_g, seq, *, tm=1024, tn=1024):
    n, d = x.shape
    qw, kw = AT_HEADS * AT_HD, AT_KV * AT_HD
    tm = min(tm, seq)
    assert w.shape[1] == 2 * qw + 2 * kw and kw == tn and qw % tn == 0 and seq % tm == 0
    nb = seq // tm
    ncol = w.shape[1] // tn
    nqk = (qw + kw) // tn
    cos, sin = _rope_tables(seq)
    pos = lambda i, j: (i % nb, 0)
    const = lambda i, j: (0, 0)
    blk = lambda f: pl.BlockSpec((tm, tn), f)
    return pl.pallas_call(
        functools.partial(_gqa_proj_kernel, tn=tn, ncol=ncol),
        grid=(n // tm, ncol + 1),
        in_specs=[pl.BlockSpec((tm, d), lambda i, j: (i, 0)), pl.BlockSpec((1, d), const),
                  pl.BlockSpec((d, tn), lambda i, j: (0, jnp.minimum(j, ncol - 1))),
                  pl.BlockSpec((tm, AT_HD), pos), pl.BlockSpec((tm, AT_HD), pos),
                  pl.BlockSpec((1, AT_HD), const), pl.BlockSpec((1, AT_HD), const)],
        out_specs=[blk(lambda i, j: (i, jnp.clip(j - 1, 0, nqk - 1))), blk(lambda i, j: (i, 0)),
                   blk(lambda i, j: (i, jnp.clip(j - 2 - nqk, 0, qw // tn - 1)))],
        out_shape=[jax.ShapeDtypeStruct((n, qw + kw), BF16), jax.ShapeDtypeStruct((n, kw), BF16),
                   jax.ShapeDtypeStruct((n, qw), BF16)],
        scratch_shapes=[pltpu.VMEM((tm, d), BF16), pltpu.VMEM((2, tm, tn), F32)],
        compiler_params=_params("parallel", "arbitrary"),
        name="gqa_proj",
    )(x, g.reshape(1, d), w, cos, sin, q_g.reshape(1, AT_HD), k_g.reshape(1, AT_HD))


AT_SAFE_LOG2_RANGE = 100.0


def _gqa_flash_kernel(q_ref, k_ref, v_ref, gate_ref, o_ref, kmax_ref, *, tk):
    tq = q_ref.shape[0]
    seq = k_ref.shape[0]
    grp = AT_HEADS // AT_KV
    rows = grp * tq
    nkv = seq // tk

    @pl.when(pl.program_id(2) == 0)
    def _():
        kf = k_ref[...].astype(F32)
        k2 = jnp.max(jnp.sum(kf * kf, axis=-1, keepdims=True), axis=0, keepdims=True)
        kmax_ref[...] = jnp.broadcast_to(jnp.sqrt(k2), kmax_ref.shape)

    qs = jnp.concatenate([q_ref[:, j * AT_HD:(j + 1) * AT_HD] for j in range(grp)], axis=0)
    qf = qs.astype(F32)
    c = jnp.sqrt(jnp.sum(qf * qf, axis=-1, keepdims=True)) * kmax_ref[0:1, 0:1] * (1.0 + 2.0 ** -8)
    cmax = jnp.max(c)

    def finish(acc, l):
        o = acc * (1.0 / l)
        for j in range(grp):
            cols = slice(j * AT_HD, (j + 1) * AT_HD)
            o_ref[:, cols] = (o[j * tq:(j + 1) * tq, :] * _silu(gate_ref[:, cols])).astype(o_ref.dtype)

    @pl.when(2.0 * cmax <= AT_SAFE_LOG2_RANGE)
    def _():
        lvec = jnp.zeros((rows, LANES), F32)
        acc = jnp.zeros((rows, AT_HD), F32)
        for t in range(nkv):
            ks = k_ref[t * tk:(t + 1) * tk, :]
            vs = v_ref[t * tk:(t + 1) * tk, :]
            p = jnp.exp2(_dot_nt(qs, ks) - c)
            for w in range(tk // LANES):
                lvec = lvec + p[:, w * LANES:(w + 1) * LANES]
            acc = acc + _dot(p.astype(BF16), vs)
        finish(acc, jnp.sum(lvec, axis=-1, keepdims=True))

    @pl.when(2.0 * cmax > AT_SAFE_LOG2_RANGE)
    def _():
        def body(t, carry):
            m, l, acc = carry
            ks = k_ref[pl.ds(t * tk, tk), :]
            vs = v_ref[pl.ds(t * tk, tk), :]
            s = _dot_nt(qs, ks)
            m_new = jnp.maximum(m, jnp.max(s, axis=-1, keepdims=True))
            alpha = jnp.exp2(m - m_new)
            p = jnp.exp2(s - m_new)
            l = alpha * l + jnp.sum(p, axis=-1, keepdims=True)
            acc = alpha * acc + _dot(p.astype(BF16), vs)
            return m_new, l, acc

        init = (jnp.full((rows, 1), -jnp.inf, F32), jnp.zeros((rows, 1), F32), jnp.zeros((rows, AT_HD), F32))
        _, l, acc = lax.fori_loop(0, nkv, body, init)
        finish(acc, l)


def gqa_flash(qk, v, gate, batch, seq, *, tq=512, tk=512):
    n = qk.shape[0]
    grp = AT_HEADS // AT_KV
    gw = grp * AT_HD
    tq = min(tq, seq)
    nq = seq // tq
    return pl.pallas_call(
        functools.partial(_gqa_flash_kernel, tk=min(tk, seq)),
        grid=(batch, AT_KV, nq),
        in_specs=[pl.BlockSpec((tq, gw), lambda b, h, i: (b * nq + i, h)),
                  pl.BlockSpec((seq, AT_HD), lambda b, h, i: (b, AT_HEADS + h)),
                  pl.BlockSpec((seq, AT_HD), lambda b, h, i: (b, h)),
                  pl.BlockSpec((tq, gw), lambda b, h, i: (b * nq + i, h))],
        out_specs=pl.BlockSpec((tq, gw), lambda b, h, i: (b * nq + i, h)),
        out_shape=jax.ShapeDtypeStruct((n, AT_HEADS * AT_HD), BF16),
        scratch_shapes=[pltpu.VMEM((SUBLANES, LANES), F32)],
        compiler_params=_params("parallel", "parallel", "arbitrary"),
        name="gqa_flash",
    )(qk, qk, v, gate)


def gqa_layer(x, g, w_in, q_g, k_g, w_out, batch, seq, final_g=None):
    nqk = AT_HEADS + AT_KV
    w = jnp.concatenate([_pair_major(w_in[:, :nqk * AT_HD], nqk), w_in[:, nqk * AT_HD:]], axis=1).astype(BF16)
    qk, v, gate = gqa_proj(x, g, w, _pair_major(q_g, 1), _pair_major(k_g, 1), seq)
    o = gqa_flash(qk, v, gate, batch, seq)
    return proj_out(o, w_out.astype(BF16), x, final_g)


def _t5_bucket_np(rel):
    half = REL_BUCKETS // 2
    exact = half // 2
    nabs = np.abs(rel)
    large = exact + (np.log(np.maximum(nabs, 1).astype(np.float32) / exact)
                     / math.log(REL_MAX_DIST / exact) * (half - exact)).astype(np.int32)
    large = np.minimum(large, half - 1)
    return np.where(rel > 0, half, 0) + np.where(nabs < exact, nabs, large)


DL_TQ = 128
DL_HALO = 64
DL_NK = DL_TQ + 2 * DL_HALO
DL_BATCH = 4


def _dilated_buckets():
    qi = np.arange(DL_TQ)[:, None]
    kj = np.arange(DL_NK)[None, :]
    tabs = [_t5_bucket_np((kj - DL_HALO - qi) * dil).astype(np.int32) for _, dil in DL_PAIRS]
    band = np.abs(kj - DL_HALO - qi) <= DL_HALO
    present = [sorted(set(t[band].tolist())) for t in tabs]
    return np.stack(tabs), present


def _dilated_kernel(relb_ref, bucket_ref, *refs, seq, tok, present):
    ng = len(DL_PAIRS)
    q_refs, gate_ref, kv_ref = refs[:ng], refs[ng], refs[ng + 1]
    halos = refs[ng + 2:3 * ng + 2]
    out_ref, bias_ref, kbuf, vbuf, obuf, lbuf = refs[3 * ng + 2:]
    i = pl.program_id(1)
    p = pl.program_id(2)
    qi = lax.broadcasted_iota(jnp.int32, (DL_TQ, DL_NK), 0)
    kj = lax.broadcasted_iota(jnp.int32, (DL_TQ, DL_NK), 1)
    band = jnp.abs(kj - DL_HALO - qi) <= DL_HALO
    lane = lax.broadcasted_iota(jnp.int32, (DL_TQ, LANES), 1)
    lo = lane < DL_HD
    log2e = math.log2(math.e)
    scale = DL_HD ** -0.5 * log2e

    @pl.when((pl.program_id(0) == 0) & (i == 0) & (p == 0))
    def _():
        for g in range(ng):
            bk = bucket_ref[g]

            def fill(h, carry, g=g, bk=bk):
                acc = jnp.zeros((DL_TQ, DL_NK), F32)
                for b in present[g]:
                    acc = jnp.where(bk == b, relb_ref[b, h], acc)
                bias_ref[g, h] = jnp.where(band, acc * log2e, NEG_BIG)
                return carry

            lax.fori_loop(0, DL_HEADS, fill, 0)

    for g, (_, dil) in enumerate(DL_PAIRS):
        prev_ref, next_ref = halos[2 * g], halos[2 * g + 1]
        kcols = slice(2 * g * LANES, (2 * g + 1) * LANES)
        vcols = slice((2 * g + 1) * LANES, (2 * g + 2) * LANES)
        q_ref = q_refs[g]
        halo = DL_HALO * dil
        ls = seq // dil
        ppb = tok // dil
        nsub = ppb // DL_TQ
        kbuf[0:halo, :] = prev_ref[:, 0:LANES]
        kbuf[halo:halo + tok, :] = kv_ref[:, kcols]
        kbuf[halo + tok:2 * halo + tok, :] = next_ref[:, 0:LANES]
        vbuf[0:halo, :] = prev_ref[:, LANES:2 * LANES]
        vbuf[halo:halo + tok, :] = kv_ref[:, vcols]
        vbuf[halo + tok:2 * halo + tok, :] = next_ref[:, LANES:2 * LANES]

        def blocks(it, carry, g=g, dil=dil, ls=ls, ppb=ppb, nsub=nsub, q_ref=q_ref):
            ld = []
            for t in range(DL_BATCH):
                idx = it * DL_BATCH + t
                r = idx // nsub
                j = idx % nsub
                start = r + j * (DL_TQ * dil)
                if dil == 1:
                    start = pl.multiple_of(start, DL_TQ)
                    qrows, krows = pl.ds(start, DL_TQ), pl.ds(start, DL_NK)
                else:
                    qrows, krows = pl.ds(start, DL_TQ, stride=dil), pl.ds(start, DL_NK, stride=dil)
                q2 = q_ref[qrows, :] * scale
                qh = [jnp.where(lo if half == 0 else ~lo, q2, 0.0).astype(BF16) for half in range(2)]
                mk = i * ppb + j * DL_TQ - DL_HALO + kj
                ld.append((qrows, qh, kbuf[krows, :].astype(BF16), vbuf[krows, :].astype(BF16),
                           (mk >= 0) & (mk < ls)))
            scores = [[_dot_nt(qh, k2) for qh in qhs] for _, qhs, k2, _, _ in ld]
            soft = []
            for t in range(DL_BATCH):
                inside = ld[t][4]
                per_head = []
                for half in range(2):
                    s = jnp.where(inside, scores[t][half] + bias_ref[g, 2 * p + half], NEG_BIG)
                    m = jnp.max(s, axis=-1, keepdims=True)
                    pexp = jnp.exp2(s - m)
                    per_head.append((pexp.astype(BF16), m, jnp.sum(pexp, axis=-1, keepdims=True)))
                soft.append(per_head)
            pv = [[_dot(pb, ld[t][3]) for pb, _, _ in soft[t]] for t in range(DL_BATCH)]
            for t in range(DL_BATCH):
                qrows = ld[t][0]
                outs = [pv[t][half] * (1.0 / soft[t][half][2]) for half in range(2)]
                lses = [soft[t][half][1] + jnp.log2(soft[t][half][2]) for half in range(2)]
                obuf[g, qrows, :] = jnp.where(lo, outs[0], outs[1])
                lbuf[g, qrows, :] = jnp.where(lo, lses[0], lses[1])
            return carry

        lax.fori_loop(0, dil * nsub // DL_BATCH, blocks, 0)

    la, lb, lc = lbuf[0], lbuf[1], lbuf[2]
    m = jnp.maximum(jnp.maximum(la, lb), lc)
    ea, eb, ec = jnp.exp2(la - m), jnp.exp2(lb - m), jnp.exp2(lc - m)
    o = (ea * obuf[0] + eb * obuf[1] + ec * obuf[2]) / (ea + eb + ec)
    out_ref[...] = (o * _silu(gate_ref[...])).astype(out_ref.dtype)


def dilated_attention(u, rel_bias, batch, seq, *, tok=2048):
    n = u.shape[0]
    ng = len(DL_PAIRS)
    hw = DL_HEADS * DL_HD
    npair = hw // LANES
    tok = min(tok, seq)
    nblk = seq // tok
    max_halo = DL_HALO * max(d for _, d in DL_PAIRS)
    assert tok % (DL_TQ * max(d for _, d in DL_PAIRS)) == 0 and tok % max_halo == 0
    buckets, present = _dilated_buckets()
    qg0 = npair * 2 * ng

    in_specs = [pl.BlockSpec(memory_space=pltpu.SMEM),
                pl.BlockSpec((ng, DL_TQ, DL_NK), lambda b, i, p: (0, 0, 0)),
                *[pl.BlockSpec((tok, LANES), lambda b, i, p, c=c: (b * nblk + i, qg0 + (ng + 1) * p + c))
                  for c in range(ng + 1)],
                pl.BlockSpec((tok, 2 * ng * LANES), lambda b, i, p: (b * nblk + i, p))]
    args = [rel_bias, jnp.asarray(buckets)] + [u] * (ng + 2)
    for gi, (_, dil) in enumerate(DL_PAIRS):
        halo = DL_HALO * dil
        hpb = tok // halo
        last = n // halo - 1
        in_specs += [pl.BlockSpec((halo, 2 * LANES), lambda b, i, p, gi=gi, hpb=hpb:
                                  (jnp.maximum((b * nblk + i) * hpb - 1, 0), ng * p + gi)),
                     pl.BlockSpec((halo, 2 * LANES), lambda b, i, p, gi=gi, hpb=hpb, last=last:
                                  (jnp.minimum((b * nblk + i + 1) * hpb, last), ng * p + gi))]
        args += [u, u]
    return pl.pallas_call(
        functools.partial(_dilated_kernel, seq=seq, tok=tok, present=present),
        grid=(batch, nblk, npair),
        in_specs=in_specs,
        out_specs=pl.BlockSpec((tok, LANES), lambda b, i, p: (b * nblk + i, p)),
        out_shape=jax.ShapeDtypeStruct((n, hw), BF16),
        scratch_shapes=[pltpu.VMEM((len(DL_PAIRS), DL_HEADS, DL_TQ, DL_NK), F32),
                        pltpu.VMEM((tok + 2 * max_halo, LANES), F32),
                        pltpu.VMEM((tok + 2 * max_halo, LANES), F32),
                        pltpu.VMEM((len(DL_PAIRS), tok, LANES), F32),
                        pltpu.VMEM((len(DL_PAIRS), tok, LANES), F32)],
        compiler_params=_params("arbitrary", "arbitrary", "arbitrary"),
        name="dilated_attention",
    )(*args)


def dilated_layer(x, g, rel_bias, w_in, w_out, batch, seq, final_g=None):
    d = w_in.shape[0]
    ng = len(DL_PAIRS)
    hw = DL_HEADS * DL_HD
    npair = hw // LANES
    w4 = w_in[:, :3 * ng * hw].reshape(d, ng, 3, npair, LANES)
    w_kv = jnp.transpose(w4[:, :, 1:3], (0, 3, 1, 2, 4)).reshape(d, npair * 2 * ng * LANES)
    w_gate = w_in[:, 3 * ng * hw:].reshape(d, npair, 1, LANES)
    w_qg = jnp.concatenate([jnp.transpose(w4[:, :, 0], (0, 2, 1, 3)), w_gate], axis=2).reshape(d, -1)
    w = jnp.concatenate([w_kv, w_qg], axis=1).astype(BF16)
    o = dilated_attention(proj_in(x, g, w), rel_bias, batch, seq)
    return proj_out(o, w_out.astype(BF16), x, final_g)


def kernel(x, norm_g, final_g, rel_bias, hgrn_lb, ssd_w_in, ssd_conv_w, ssd_conv_b, ssd_dt_bias, ssd_a_log, ssd_d,
           ssd_norm_g, ssd_w_out, hg_w_in, hg_norm_g, hg_w_out, at_w_in, at_q_norm_g, at_k_norm_g, at_w_out,
           dl_w_in, dl_w_out):
    batch, seq, d = x.shape
    depth = norm_g.shape[0]
    n_mixers = 4
    lb_sm = jax.nn.softmax(hgrn_lb.astype(F32), axis=0)
    lb_all = jnp.cumsum(lb_sm, axis=0) - lb_sm[0:1]
    h = x.reshape(batch * seq, d)
    for layer in range(depth):
        kind, slot = layer % n_mixers, layer // n_mixers
        fg = final_g if layer == depth - 1 else None
        if kind == 0:
            h = ssd_layer(h, norm_g[layer], ssd_w_in[slot], ssd_conv_w[slot], ssd_conv_b[slot], ssd_dt_bias[slot],
                          ssd_a_log[slot], ssd_d[slot], ssd_norm_g[slot], ssd_w_out[slot], batch, seq, fg)
        elif kind == 1:
            h = hgrn_layer(h, norm_g[layer], lb_all[layer], hg_w_in[slot], hg_norm_g[slot], hg_w_out[slot],
                           batch, seq, fg)
        elif kind == 2:
            h = gqa_layer(h, norm_g[layer], at_w_in[slot], at_q_norm_g[slot], at_k_norm_g[slot], at_w_out[slot],
                          batch, seq, fg)
        else:
            h = dilated_layer(h, norm_g[layer], rel_bias, dl_w_in[slot], dl_w_out[slot], batch, seq, fg)
    return h.reshape(batch, seq, d)
```

```python
import functools
import math

import jax
import jax.numpy as jnp
import numpy as np
from jax import lax
from jax.experimental import pallas as pl
from jax.experimental.pallas import tpu as pltpu

F32 = jnp.float32
BF16 = jnp.bfloat16

EPS = 1e-6
NEG_BIG = -1e30
GRID_W = 64
ROPE_THETA = 10000.0

SSD_HEADDIM = 64
SSD_HEADS = 32
SSD_GROUPS = 4
SSD_STATE = 128
SSD_CONV = 7
HG_HEADS = 8
HG_SUB = 32
AT_HEADS = 16
AT_KV = 8
AT_HD = 128
DL_PAIRS = ((128, 1), (512, 4), (2048, 16))
DL_HEADS = 16
DL_HD = 64
REL_BUCKETS = 32
REL_MAX_DIST = 1024

LANES = 128
SUBLANES = 8
CHUNK = 128
VMEM_LIMIT = 56 * 1024 * 1024


def _params(*sem):
    return pltpu.CompilerParams(dimension_semantics=sem, vmem_limit_bytes=VMEM_LIMIT)


def _sigmoid(x):
    return 0.5 * jnp.tanh(0.5 * x) + 0.5


def _silu(x):
    return x * _sigmoid(x)


def _softplus(x):
    return jnp.maximum(x, 0.0) + jnp.log(1.0 + jnp.exp(-jnp.abs(x)))


def _dot(a, b):
    return jnp.dot(a, b, preferred_element_type=F32)


def _dot_nt(a, b):
    return lax.dot_general(a, b, (((1,), (1,)), ((), ())), preferred_element_type=F32)


def _dot_tn(a, b):
    return lax.dot_general(a, b, (((0,), (0,)), ((), ())), preferred_element_type=F32)


def _prefix_sum(tri, x):
    x1 = x.astype(BF16)
    r1 = x - x1.astype(F32)
    x2 = r1.astype(BF16)
    x3 = (r1 - x2.astype(F32)).astype(BF16)
    return _dot(tri, x1) + _dot(tri, x2) + _dot(tri, x3)


def _proj_in_kernel(x_ref, g_ref, w_ref, o_ref, xn_ref):
    @pl.when(pl.program_id(1) == 0)
    def _():
        x = x_ref[...]
        ms = jnp.mean(x * x, axis=-1, keepdims=True)
        xn_ref[...] = (x * lax.rsqrt(ms + EPS) * g_ref[...]).astype(BF16)

    o_ref[...] = _dot(xn_ref[...], w_ref[...]).astype(o_ref.dtype)


def proj_in(x, g, w, *, tm=2048, tn=1024, out_dtype=F32):
    n, d = x.shape
    dout = w.shape[1]
    tm = min(tm, n)
    tn = min(tn, dout)
    assert n % tm == 0 and dout % tn == 0
    return pl.pallas_call(
        _proj_in_kernel,
        grid=(n // tm, dout // tn),
        in_specs=[pl.BlockSpec((tm, d), lambda i, j: (i, 0)),
                  pl.BlockSpec((1, d), lambda i, j: (0, 0)),
                  pl.BlockSpec((d, tn), lambda i, j: (0, j))],
        out_specs=pl.BlockSpec((tm, tn), lambda i, j: (i, j)),
        out_shape=jax.ShapeDtypeStruct((n, dout), out_dtype),
        scratch_shapes=[pltpu.VMEM((tm, d), BF16)],
        compiler_params=_params("parallel", "arbitrary"),
        name="proj_in",
    )(x, g.reshape(1, d), w)


def _proj_out_kernel(a_ref, w_ref, r_ref, g_ref, o_ref, *, final):
    y = r_ref[...] + _dot(a_ref[...], w_ref[...])
    if final:
        ms = jnp.mean(y * y, axis=-1, keepdims=True)
        y = y * lax.rsqrt(ms + EPS) * g_ref[...]
    o_ref[...] = y


def proj_out(a, w, res, final_g=None, *, tm=1024):
    n, k = a.shape
    d = w.shape[1]
    tm = min(tm, n)
    assert n % tm == 0
    g = jnp.ones((1, d), F32) if final_g is None else final_g.reshape(1, d)
    return pl.pallas_call(
        functools.partial(_proj_out_kernel, final=final_g is not None),
        grid=(n // tm,),
        in_specs=[pl.BlockSpec((tm, k), lambda i: (i, 0)),
                  pl.BlockSpec((k, d), lambda i: (0, 0)),
                  pl.BlockSpec((tm, d), lambda i: (i, 0)),
                  pl.BlockSpec((1, d), lambda i: (0, 0))],
        out_specs=pl.BlockSpec((tm, d), lambda i: (i, 0)),
        out_shape=jax.ShapeDtypeStruct((n, d), F32),
        compiler_params=_params("parallel"),
        name="proj_out",
    )(a, w, res, g)


def _ssd_conv_kernel(xp_ref, xc_ref, xn_ref, bp_ref, bc_ref, bn_ref, wx_ref, wb_ref, bx_ref, bb_ref,
                     ox_ref, ob_ref, *, nblk):
    i = pl.program_id(1)
    pad = SSD_CONV // 2

    def conv(prev_ref, cur_ref, next_ref, w_ref, b_ref, o_ref):
        rows = cur_ref.shape[0]
        prev = jnp.where(i > 0, prev_ref[...], 0.0)
        nxt = jnp.where(i < nblk - 1, next_ref[...], 0.0)
        ext = jnp.concatenate([prev, cur_ref[...], nxt], axis=0)
        acc = jnp.zeros(cur_ref.shape, F32) + b_ref[...]
        total = rows + 2 * SUBLANES
        for t in range(SSD_CONV):
            shifted = ext if t == pad else pltpu.roll(ext, (pad - t) % total, axis=0)
            acc = acc + shifted[SUBLANES:SUBLANES + rows, :] * w_ref[t:t + 1, :]
        o_ref[...] = _silu(acc)

    conv(xp_ref, xc_ref, xn_ref, wx_ref, bx_ref, ox_ref)
    conv(bp_ref, bc_ref, bn_ref, wb_ref, bb_ref, ob_ref)


def ssd_conv(u, conv_w, conv_b, batch, seq, *, tc=256):
    n = u.shape[0]
    di = SSD_HEADS * SSD_HEADDIM
    gn2 = 2 * SSD_GROUPS * SSD_STATE
    tc = min(tc, seq)
    nblk = seq // tc
    r8 = tc // SUBLANES
    last8 = n // SUBLANES - 1

    def cur(wblk):
        return lambda b, i: (b * nblk + i, wblk)

    def prev(wblk):
        return lambda b, i: (jnp.maximum((b * nblk + i) * r8 - 1, 0), wblk)

    def nxt(wblk):
        return lambda b, i: (jnp.minimum((b * nblk + i + 1) * r8, last8), wblk)

    wx, wb = conv_w[:, :di], conv_w[:, di:]
    bx, bb = conv_b[:di].reshape(1, di), conv_b[di:].reshape(1, gn2)
    const = lambda b, i: (0, 0)
    return pl.pallas_call(
        functools.partial(_ssd_conv_kernel, nblk=nblk),
        grid=(batch, nblk),
        in_specs=[pl.BlockSpec((SUBLANES, di), prev(1)), pl.BlockSpec((tc, di), cur(1)),
                  pl.BlockSpec((SUBLANES, di), nxt(1)),
                  pl.BlockSpec((SUBLANES, gn2), prev(4)), pl.BlockSpec((tc, gn2), cur(4)),
                  pl.BlockSpec((SUBLANES, gn2), nxt(4)),
                  pl.BlockSpec((SSD_CONV, di), const), pl.BlockSpec((SSD_CONV, gn2), const),
                  pl.BlockSpec((1, di), const), pl.BlockSpec((1, gn2), const)],
        out_specs=[pl.BlockSpec((tc, di), lambda b, i: (b * nblk + i, 0)),
                   pl.BlockSpec((tc, gn2), lambda b, i: (b * nblk + i, 0))],
        out_shape=[jax.ShapeDtypeStruct((n, di), F32), jax.ShapeDtypeStruct((n, gn2), F32)],
        compiler_params=_params("parallel", "parallel"),
        name="ssd_conv",
    )(u, u, u, u, u, u, wx, wb, bx, bb)


def _ssd_scan_kernel(x_ref, bc_ref, dtr_ref, dtb_ref, alog_ref, *rest, reverse, final):
    if final:
        z_ref, yo_ref, dskip_ref, ng_ref, o_ref, st_ref = rest
    else:
        o_ref, st_ref = rest
    c = pl.program_id(1)
    nb = x_ref.shape[0]
    q = CHUNK
    gn = SSD_GROUPS * SSD_STATE
    hpg = SSD_HEADS // SSD_GROUPS
    gw = hpg * SSD_HEADDIM
    hoff = SSD_HEADS if reverse else 0
    far = 0 if reverse else q - 1

    @pl.when(c == 0)
    def _():
        st_ref[...] = jnp.zeros(st_ref.shape, F32)

    row = lax.broadcasted_iota(jnp.int32, (q, q), 0)
    col = lax.broadcasted_iota(jnp.int32, (q, q), 1)
    valid = (col >= row) if reverse else (col <= row)
    tri = valid.astype(BF16)
    lane = lax.broadcasted_iota(jnp.int32, (q, LANES), 1)
    lo = lane < SSD_HEADDIM
    lo_row = lo[0:1, :]
    neg_a = -jnp.exp(alog_ref[...])

    prep = []
    for bb in range(nb):
        dt = _softplus(dtr_ref[bb] + dtb_ref[...])
        cs_col = _prefix_sum(tri, dt * neg_a)
        cs_row = cs_col.T
        dt_row = dt.T
        w_row = jnp.exp(cs_row[:, far:far + 1] - cs_row) * dt_row
        dec = jnp.exp(cs_col[far:far + 1, :])
        prep.append((cs_col, cs_row, dt_row, w_row, dec))

    for g in range(SSD_GROUPS):
        grp = []
        for bb in range(nb):
            b_f = bc_ref[bb, :, g * SSD_STATE:(g + 1) * SSD_STATE]
            c_f = bc_ref[bb, :, gn + g * SSD_STATE:gn + (g + 1) * SSD_STATE]
            c_b = c_f.astype(BF16)
            grp.append((_dot_nt(c_b, b_f.astype(BF16)).astype(BF16), c_b, b_f.T.astype(BF16)))
        y_parts = [[] for _ in range(nb)]
        for pp in range(hpg // 2):
            h0 = g * hpg + 2 * pp
            cols = slice(h0 * SSD_HEADDIM, (h0 + 2) * SSD_HEADDIM)
            scols = slice(2 * pp * SSD_HEADDIM, (2 * pp + 2) * SSD_HEADDIM)
            ops = []
            for bb in range(nb):
                cs_col, cs_row, dt_row, w_row, dec = prep[bb]
                cb, c_f, b_t = grp[bb]
                x2 = x_ref[bb, :, cols]
                x2b = x2.astype(BF16)
                s_old = st_ref[bb, g, :, scols]
                rhs = jnp.concatenate([x2b, s_old.astype(BF16)], axis=0)
                lhs, bws = [], []
                for h in (h0, h0 + 1):
                    hl = hoff + h
                    colb = jnp.broadcast_to(cs_col[:, hl:hl + 1], (q, q))
                    lmat = jnp.exp(jnp.where(valid, colb - cs_row[hl:hl + 1, :], NEG_BIG))
                    m = cb * (lmat.astype(BF16) * dt_row[hl:hl + 1, :].astype(BF16))
                    ce = c_f * jnp.exp(colb).astype(BF16)
                    lhs.append(jnp.concatenate([m, ce], axis=1))
                    bws.append(b_t * w_row[hl:hl + 1, :].astype(BF16))
                dec2 = jnp.where(lo_row, dec[:, hoff + h0:hoff + h0 + 1], dec[:, hoff + h0 + 1:hoff + h0 + 2])
                ops.append((x2, x2b, s_old, rhs, lhs, bws, dec2))
            prods = []
            for bb in range(nb):
                _, x2b, _, rhs, lhs, bws, _ = ops[bb]
                prods.append(([_dot(l, rhs) for l in lhs], [_dot(w, x2b) for w in bws]))
            for bb in range(nb):
                x2, _, s_old, _, _, _, dec2 = ops[bb]
                ys, sts = prods[bb]
                y2 = jnp.where(lo, ys[0], ys[1])
                st_ref[bb, g, :, scols] = s_old * dec2 + jnp.where(lo, sts[0], sts[1])
                if final:
                    y2 = y2 + yo_ref[bb, :, cols] + x2 * dskip_ref[:, cols]
                    y_parts[bb].append(y2 * _silu(z_ref[bb, :, cols]))
                else:
                    o_ref[bb, :, cols] = y2
        if final:
            gcols = slice(g * gw, (g + 1) * gw)
            for bb in range(nb):
                yg = jnp.concatenate(y_parts[bb], axis=1)
                ms = jnp.mean(yg * yg, axis=-1, keepdims=True)
                o_ref[bb, :, gcols] = (yg * lax.rsqrt(ms + EPS) * ng_ref[:, gcols]).astype(o_ref.dtype)


def ssd_scan(xc, bc, dtr, dt_bias, a_log, batch, seq, *, reverse, final_args=None, nb=4):
    n, di = xc.shape
    nc = seq // CHUNK
    gn2 = bc.shape[1]
    final = final_args is not None
    nb = min(nb, batch)
    assert batch % nb == 0
    v3 = lambda a: a.reshape(batch, seq, a.shape[1])

    def chunk(c):
        return nc - 1 - c if reverse else c

    blk = lambda w: pl.BlockSpec((nb, CHUNK, w), lambda b, c: (b, chunk(c), 0))
    const = lambda w: pl.BlockSpec((1, w), lambda b, c: (0, 0))
    in_specs = [blk(di), blk(gn2), blk(LANES), const(LANES), const(LANES)]
    args = [v3(xc), v3(bc), v3(dtr), dt_bias, a_log]
    if final:
        u, y_other, dskip, ng = final_args
        in_specs += [blk(di), blk(di), const(di), const(di)]
        args += [v3(u), v3(y_other), dskip, ng]
    out = pl.pallas_call(
        functools.partial(_ssd_scan_kernel, reverse=reverse, final=final),
        grid=(batch // nb, nc),
        in_specs=in_specs,
        out_specs=blk(di),
        out_shape=jax.ShapeDtypeStruct((batch, seq, di), BF16 if final else F32),
        scratch_shapes=[pltpu.VMEM((nb, SSD_GROUPS, SSD_STATE, (SSD_HEADS // SSD_GROUPS) * SSD_HEADDIM), F32)],
        compiler_params=_params("parallel", "arbitrary"),
        name="ssd_scan_rev" if reverse else "ssd_scan_fwd",
    )(*args)
    return out.reshape(n, di)


def ssd_layer(x, g, w_in, conv_w, conv_b, dt_bias, a_log, d_skip, norm_g, w_out, batch, seq, final_g=None):
    di = SSD_HEADS * SSD_HEADDIM
    main = 2 * di + 2 * SSD_GROUPS * SSD_STATE
    w_main = w_in[:, :main].astype(BF16)
    w_dt = jnp.pad(w_in[:, main:], ((0, 0), (0, LANES - 2 * SSD_HEADS))).astype(BF16)
    u = proj_in(x, g, w_main)
    dtr = proj_in(x, g, w_dt)
    xc, bc = ssd_conv(u, conv_w, conv_b, batch, seq)
    pad = lambda v: jnp.pad(v.reshape(1, 2 * SSD_HEADS), ((0, 0), (0, LANES - 2 * SSD_HEADS)))
    dtb, alog = pad(dt_bias), pad(a_log)
    y_rev = ssd_scan(xc, bc, dtr, dtb, alog, batch, seq, reverse=True)
    dskip = jnp.repeat(d_skip, SSD_HEADDIM).reshape(1, di)
    y = ssd_scan(xc, bc, dtr, dtb, alog, batch, seq, reverse=False,
                 final_args=(u, y_rev, dskip, norm_g.reshape(1, di)))
    return proj_out(y, w_out.astype(BF16), x, final_g)


def _hgrn_scan_kernel(q_ref, f_ref, v_ref, lb_ref, *rest, reverse, final):
    if final:
        gate_ref, oo_ref, ng_ref, o_ref, st_ref = rest
    else:
        o_ref, st_ref = rest
    c = pl.program_id(1)
    n = CHUNK
    nsub = n // HG_SUB
    dk = LANES

    @pl.when(c == 0)
    def _():
        st_ref[...] = jnp.zeros(st_ref.shape, F32)

    nb = q_ref.shape[0]
    lb = lb_ref[...]
    row = lax.broadcasted_iota(jnp.int32, (n, n), 0)
    col = lax.broadcasted_iota(jnp.int32, (n, n), 1)
    valid = (col >= row) if reverse else (col <= row)
    tri = valid.astype(BF16)
    ref_row = HG_SUB // 2 - 1 if reverse else HG_SUB // 2
    far = 0 if reverse else n - 1

    prep = []
    for bb in range(nb):
        f = lb + (1.0 - lb) * _sigmoid(f_ref[bb])
        prep.append((_silu(q_ref[bb]), 1.0 - f, _prefix_sum(tri, jnp.log(f))))

    for h in range(HG_HEADS):
        cols = slice(h * dk, (h + 1) * dk)
        staged = []
        for bb in range(nb):
            qa, ka, gsum = prep[bb]
            gh, qh, kh = gsum[:, cols], qa[:, cols], ka[:, cols]
            tot = gh[far:far + 1, :]
            qp, kn, anchors = [], [], []
            for s in range(nsub):
                rs = slice(s * HG_SUB, (s + 1) * HG_SUB)
                a = gh[s * HG_SUB + ref_row:s * HG_SUB + ref_row + 1, :]
                anchors.append(a)
                qp.append((qh[rs] * jnp.exp(gh[rs] - a)).astype(BF16))
                kn.append((kh[rs] * jnp.exp(a - gh[rs])).astype(BF16))
            lhs_cols = []
            for j in range(nsub):
                lhs = []
                for i in range(nsub):
                    live = (i <= j) if reverse else (i >= j)
                    if not live:
                        lhs.append(jnp.zeros((HG_SUB, dk), BF16))
                    elif i == j:
                        lhs.append(qp[i])
                    else:
                        lhs.append(qp[i] * jnp.exp(anchors[i] - anchors[j]).astype(BF16))
                lhs_cols.append(jnp.concatenate(lhs, axis=0))
            qe = jnp.concatenate([qp[s] * jnp.exp(anchors[s]).astype(BF16) for s in range(nsub)], axis=0)
            kd = jnp.concatenate([kn[s] * jnp.exp(tot - anchors[s]).astype(BF16) for s in range(nsub)], axis=0)
            staged.append((lhs_cols, kn, qe, kd, jnp.exp(tot)))
        atts = []
        for bb in range(nb):
            lhs_cols, kn_b = staged[bb][0], staged[bb][1]
            att = jnp.concatenate([_dot_nt(lhs_cols[j], kn_b[j]) for j in range(nsub)], axis=1)
            atts.append(jnp.where(valid, att, 0.0).astype(BF16))
        for bb in range(nb):
            _, _, qe, kd, decay = staged[bb]
            vh = v_ref[bb, :, cols].astype(BF16)
            s_old = st_ref[bb, h]
            o = _dot(atts[bb], vh) + _dot_nt(qe, s_old.astype(BF16))
            st_ref[bb, h] = s_old * decay + _dot_tn(vh, kd)
            if final:
                o = o + oo_ref[bb, :, cols]
                ms = jnp.mean(o * o, axis=-1, keepdims=True)
                o = o * lax.rsqrt(ms + EPS) * ng_ref[:, cols] * _silu(gate_ref[bb, :, cols])
            o_ref[bb, :, cols] = o.astype(o_ref.dtype)


def hgrn_scan(u, lb, batch, seq, *, reverse, final_args=None, nb=4):
    n = u.shape[0]
    w = HG_HEADS * LANES
    nc = seq // CHUNK
    final = final_args is not None
    nb = min(nb, batch)
    assert batch % nb == 0
    v3 = lambda a: a.reshape(batch, seq, a.shape[1])

    def chunk(c):
        return nc - 1 - c if reverse else c

    ublk = lambda j: pl.BlockSpec((nb, CHUNK, w), lambda b, c: (b, chunk(c), j))
    const = pl.BlockSpec((1, w), lambda b, c: (0, 0))
    u3 = v3(u)
    in_specs = [ublk(0), ublk(2 if reverse else 1), ublk(3), const]
    args = [u3, u3, u3, lb.reshape(1, w)]
    if final:
        o_other, ng = final_args
        in_specs += [ublk(4), ublk(0), const]
        args += [u3, v3(o_other), ng.reshape(1, w)]
    out = pl.pallas_call(
        functools.partial(_hgrn_scan_kernel, reverse=reverse, final=final),
        grid=(batch // nb, nc),
        in_specs=in_specs,
        out_specs=ublk(0),
        out_shape=jax.ShapeDtypeStruct((batch, seq, w), BF16 if final else F32),
        scratch_shapes=[pltpu.VMEM((nb, HG_HEADS, LANES, LANES), F32)],
        compiler_params=_params("parallel", "arbitrary"),
        name="hgrn_scan_rev" if reverse else "hgrn_scan_fwd",
    )(*args)
    return out.reshape(n, w)


def hgrn_layer(x, g, lb, w_in, norm_g, w_out, batch, seq, final_g=None):
    u = proj_in(x, g, w_in.astype(BF16))
    o_rev = hgrn_scan(u, lb, batch, seq, reverse=True)
    o = hgrn_scan(u, lb, batch, seq, reverse=False, final_args=(o_rev, norm_g))
    return proj_out(o, w_out.astype(BF16), x, final_g)


def _rope_tables(seq):
    pos = np.arange(seq)
    rowp = (pos // GRID_W).astype(np.float64)
    colp = (pos % GRID_W).astype(np.float64)
    half = AT_HD // 4
    inv = ROPE_THETA ** (-np.arange(0, 2 * half, 2, dtype=np.float64) / (2 * half))
    ar, ac = rowp[:, None] * inv, colp[:, None] * inv
    cos = np.concatenate([np.cos(ar), np.cos(ac), np.cos(ar), np.cos(ac)], axis=1)
    sin = np.concatenate([-np.sin(ar), -np.sin(ac), np.sin(ar), np.sin(ac)], axis=1)
    return jnp.asarray(cos, F32), jnp.asarray(sin, F32)


def _pair_major(a, nheads):
    lead = a.shape[:-1]
    a = a.reshape(*lead, nheads, 2, 2, AT_HD // 4)
    return jnp.swapaxes(a, -3, -2).reshape(*lead, nheads * AT_HD)


def _gqa_proj_kernel(x_ref, g_ref, w_ref, cos_ref, sin_ref, qg_ref, kg_ref, qk_ref, vo_ref, go_ref, xn_ref, acc_ref,
                     *, tn, ncol):
    j = pl.program_id(1)
    nqk = (AT_HEADS + AT_KV) * AT_HD // tn
    jq = AT_HEADS * AT_HD // tn

    @pl.when(j == 0)
    def _():
        x = x_ref[...]
        ms = jnp.mean(x * x, axis=-1, keepdims=True)
        xn_ref[...] = (x * lax.rsqrt(ms + EPS) * g_ref[...]).astype(BF16)

    def heads(blk):
        cos, sin = cos_ref[...], sin_ref[...]
        gain = qg_ref[...] * (AT_HD ** -0.5 * math.log2(math.e)) if blk < jq else kg_ref[...]
        for h in range(tn // AT_HD):
            cols = slice(h * AT_HD, (h + 1) * AT_HD)
            xh = acc_ref[blk % 2, :, cols]
            ms = jnp.mean(xh * xh, axis=-1, keepdims=True)
            xn = xh * lax.rsqrt(ms + EPS) * gain
            qk_ref[:, cols] = (xn * cos + pltpu.roll(xn, AT_HD // 2, axis=1) * sin).astype(qk_ref.dtype)

    for jj in range(ncol + 1):
        @pl.when(j == jj)
        def _(jj=jj):
            if jj < ncol:
                acc_ref[jj % 2] = _dot(xn_ref[...], w_ref[...])
            prev = jj - 1
            if 0 <= prev < nqk:
                heads(prev)
            elif prev == nqk:
                vo_ref[...] = acc_ref[prev % 2].astype(vo_ref.dtype)
            elif prev > nqk:
                go_ref[...] = acc_ref[prev % 2].astype(go_ref.dtype)


def gqa_proj(x, g, w, q_g, k_g, seq, *, tm=1024, tn=1024):
    n, d = x.shape
    qw, kw = AT_HEADS * AT_HD, AT_KV * AT_HD
    tm = min(tm, seq)
    assert w.shape[1] == 2 * qw + 2 * kw and kw == tn and qw % tn == 0 and seq % tm == 0
    nb = seq // tm
    ncol = w.shape[1] // tn
    nqk = (qw + kw) // tn
    cos, sin = _rope_tables(seq)
    pos = lambda i, j: (i % nb, 0)
    const = lambda i, j: (0, 0)
    blk = lambda f: pl.BlockSpec((tm, tn), f)
    return pl.pallas_call(
        functools.partial(_gqa_proj_kernel, tn=tn, ncol=ncol),
        grid=(n // tm, ncol + 1),
        in_specs=[pl.BlockSpec((tm, d), lambda i, j: (i, 0)), pl.BlockSpec((1, d), const),
                  pl.BlockSpec((d, tn), lambda i, j: (0, jnp.minimum(j, ncol - 1))),
                  pl.BlockSpec((tm, AT_HD), pos), pl.BlockSpec((tm, AT_HD), pos),
                  pl.BlockSpec((1, AT_HD), const), pl.BlockSpec((1, AT_HD), const)],
        out_specs=[blk(lambda i, j: (i, jnp.clip(j - 1, 0, nqk - 1))), blk(lambda i, j: (i, 0)),
                   blk(lambda i, j: (i, jnp.clip(j - 2 - nqk, 0, qw // tn - 1)))],
        out_shape=[jax.ShapeDtypeStruct((n, qw + kw), BF16), jax.ShapeDtypeStruct((n, kw), BF16),
                   jax.ShapeDtypeStruct((n, qw), BF16)],
        scratch_shapes=[pltpu.VMEM((tm, d), BF16), pltpu.VMEM((2, tm, tn), F32)],
        compiler_params=_params("parallel", "arbitrary"),
        name="gqa_proj",
    )(x, g.reshape(1, d), w, cos, sin, q_g.reshape(1, AT_HD), k_g.reshape(1, AT_HD))


AT_SAFE_LOG2_RANGE = 100.0


def _gqa_flash_kernel(q_ref, k_ref, v_ref, gate_ref, o_ref, kmax_ref, *, tk):
    tq = q_ref.shape[0]
    seq = k_ref.shape[0]
    grp = AT_HEADS // AT_KV
    rows = grp * tq
    nkv = seq // tk

    @pl.when(pl.program_id(2) == 0)
    def _():
        kf = k_ref[...].astype(F32)
        k2 = jnp.max(jnp.sum(kf * kf, axis=-1, keepdims=True), axis=0, keepdims=True)
        kmax_ref[...] = jnp.broadcast_to(jnp.sqrt(k2), kmax_ref.shape)

    qs = jnp.concatenate([q_ref[:, j * AT_HD:(j + 1) * AT_HD] for j in range(grp)], axis=0)
    qf = qs.astype(F32)
    c = jnp.sqrt(jnp.sum(qf * qf, axis=-1, keepdims=True)) * kmax_ref[0:1, 0:1] * (1.0 + 2.0 ** -8)
    cmax = jnp.max(c)

    def finish(acc, l):
        o = acc * (1.0 / l)
        for j in range(grp):
            cols = slice(j * AT_HD, (j + 1) * AT_HD)
            o_ref[:, cols] = (o[j * tq:(j + 1) * tq, :] * _silu(gate_ref[:, cols])).astype(o_ref.dtype)

    @pl.when(2.0 * cmax <= AT_SAFE_LOG2_RANGE)
    def _():
        lvec = jnp.zeros((rows, LANES), F32)
        acc = jnp.zeros((rows, AT_HD), F32)
        for t in range(nkv):
            ks = k_ref[t * tk:(t + 1) * tk, :]
            vs = v_ref[t * tk:(t + 1) * tk, :]
            p = jnp.exp2(_dot_nt(qs, ks) - c)
            for w in range(tk // LANES):
                lvec = lvec + p[:, w * LANES:(w + 1) * LANES]
            acc = acc + _dot(p.astype(BF16), vs)
        finish(acc, jnp.sum(lvec, axis=-1, keepdims=True))

    @pl.when(2.0 * cmax > AT_SAFE_LOG2_RANGE)
    def _():
        def body(t, carry):
            m, l, acc = carry
            ks = k_ref[pl.ds(t * tk, tk), :]
            vs = v_ref[pl.ds(t * tk, tk), :]
            s = _dot_nt(qs, ks)
            m_new = jnp.maximum(m, jnp.max(s, axis=-1, keepdims=True))
            alpha = jnp.exp2(m - m_new)
            p = jnp.exp2(s - m_new)
            l = alpha * l + jnp.sum(p, axis=-1, keepdims=True)
            acc = alpha * acc + _dot(p.astype(BF16), vs)
            return m_new, l, acc

        init = (jnp.full((rows, 1), -jnp.inf, F32), jnp.zeros((rows, 1), F32), jnp.zeros((rows, AT_HD), F32))
        _, l, acc = lax.fori_loop(0, nkv, body, init)
        finish(acc, l)


def gqa_flash(qk, v, gate, batch, seq, *, tq=1024, tk=512):
    n = qk.shape[0]
    grp = AT_HEADS // AT_KV
    gw = grp * AT_HD
    tq = min(tq, seq)
    nq = seq // tq
    return pl.pallas_call(
        functools.partial(_gqa_flash_kernel, tk=min(tk, seq)),
        grid=(batch, AT_KV, nq),
        in_specs=[pl.BlockSpec((tq, gw), lambda b, h, i: (b * nq + i, h)),
                  pl.BlockSpec((seq, AT_HD), lambda b, h, i: (b, AT_HEADS + h)),
                  pl.BlockSpec((seq, AT_HD), lambda b, h, i: (b, h)),
                  pl.BlockSpec((tq, gw), lambda b, h, i: (b * nq + i, h))],
        out_specs=pl.BlockSpec((tq, gw), lambda b, h, i: (b * nq + i, h)),
        out_shape=jax.ShapeDtypeStruct((n, AT_HEADS * AT_HD), BF16),
        scratch_shapes=[pltpu.VMEM((SUBLANES, LANES), F32)],
        compiler_params=_params("parallel", "parallel", "arbitrary"),
        name="gqa_flash",
    )(qk, qk, v, gate)


def gqa_layer(x, g, w_in, q_g, k_g, w_out, batch, seq, final_g=None):
    nqk = AT_HEADS + AT_KV
    w = jnp.concatenate([_pair_major(w_in[:, :nqk * AT_HD], nqk), w_in[:, nqk * AT_HD:]], axis=1).astype(BF16)
    qk, v, gate = gqa_proj(x, g, w, _pair_major(q_g, 1), _pair_major(k_g, 1), seq)
    o = gqa_flash(qk, v, gate, batch, seq)
    return proj_out(o, w_out.astype(BF16), x, final_g)


def _t5_bucket_np(rel):
    half = REL_BUCKETS // 2
    exact = half // 2
    nabs = np.abs(rel)
    large = exact + (np.log(np.maximum(nabs, 1).astype(np.float32) / exact)
                     / math.log(REL_MAX_DIST / exact) * (half - exact)).astype(np.int32)
    large = np.minimum(large, half - 1)
    return np.where(rel > 0, half, 0) + np.where(nabs < exact, nabs, large)


DL_TQ = 128
DL_HALO = 64
DL_NK = DL_TQ + 2 * DL_HALO
DL_BATCH = 4


def _dilated_buckets():
    qi = np.arange(DL_TQ)[:, None]
    kj = np.arange(DL_NK)[None, :]
    tabs = [_t5_bucket_np((kj - DL_HALO - qi) * dil).astype(np.int32) for _, dil in DL_PAIRS]
    band = np.abs(kj - DL_HALO - qi) <= DL_HALO
    present = [sorted(set(t[band].tolist())) for t in tabs]
    return np.stack(tabs), present


def _dilated_kernel(relb_ref, bucket_ref, *refs, seq, tok, present):
    ng = len(DL_PAIRS)
    io = refs[:7 * ng]
    gate_ref, out_ref, bias_ref, kbuf, vbuf, obuf, lbuf = refs[7 * ng:]
    i = pl.program_id(1)
    p = pl.program_id(2)
    qi = lax.broadcasted_iota(jnp.int32, (DL_TQ, DL_NK), 0)
    kj = lax.broadcasted_iota(jnp.int32, (DL_TQ, DL_NK), 1)
    band = jnp.abs(kj - DL_HALO - qi) <= DL_HALO
    lane = lax.broadcasted_iota(jnp.int32, (DL_TQ, LANES), 1)
    lo = lane < DL_HD
    log2e = math.log2(math.e)
    scale = DL_HD ** -0.5 * log2e

    @pl.when((pl.program_id(0) == 0) & (i == 0) & (p == 0))
    def _():
        for g in range(ng):
            bk = bucket_ref[g]

            def fill(h, carry, g=g, bk=bk):
                acc = jnp.zeros((DL_TQ, DL_NK), F32)
                for b in present[g]:
                    acc = jnp.where(bk == b, relb_ref[b, h], acc)
                bias_ref[g, h] = jnp.where(band, acc * log2e, NEG_BIG)
                return carry

            lax.fori_loop(0, DL_HEADS, fill, 0)

    for g, (_, dil) in enumerate(DL_PAIRS):
        q_ref, kp_ref, kc_ref, kn_ref, vp_ref, vc_ref, vn_ref = io[7 * g:7 * g + 7]
        halo = DL_HALO * dil
        ls = seq // dil
        ppb = tok // dil
        nsub = ppb // DL_TQ
        kbuf[0:halo, :] = kp_ref[...]
        kbuf[halo:halo + tok, :] = kc_ref[...]
        kbuf[halo + tok:2 * halo + tok, :] = kn_ref[...]
        vbuf[0:halo, :] = vp_ref[...]
        vbuf[halo:halo + tok, :] = vc_ref[...]
        vbuf[halo + tok:2 * halo + tok, :] = vn_ref[...]

        def blocks(it, carry, g=g, dil=dil, ls=ls, ppb=ppb, nsub=nsub, q_ref=q_ref):
            ld = []
            for t in range(DL_BATCH):
                idx = it * DL_BATCH + t
                r = idx // nsub
                j = idx % nsub
                start = r + j * (DL_TQ * dil)
                if dil == 1:
                    qrows, krows = pl.ds(start, DL_TQ), pl.ds(start, DL_NK)
                else:
                    qrows, krows = pl.ds(start, DL_TQ, stride=dil), pl.ds(start, DL_NK, stride=dil)
                q2 = q_ref[qrows, :] * scale
                qh = [jnp.where(lo if half == 0 else ~lo, q2, 0.0).astype(BF16) for half in range(2)]
                mk = i * ppb + j * DL_TQ - DL_HALO + kj
                ld.append((qrows, qh, kbuf[krows, :].astype(BF16), vbuf[krows, :].astype(BF16),
                           (mk >= 0) & (mk < ls)))
            scores = [[_dot_nt(qh, k2) for qh in qhs] for _, qhs, k2, _, _ in ld]
            soft = []
            for t in range(DL_BATCH):
                inside = ld[t][4]
                per_head = []
                for half in range(2):
                    s = jnp.where(inside, scores[t][half] + bias_ref[g, 2 * p + half], NEG_BIG)
                    m = jnp.max(s, axis=-1, keepdims=True)
                    pexp = jnp.exp2(s - m)
                    per_head.append((pexp.astype(BF16), m, jnp.sum(pexp, axis=-1, keepdims=True)))
                soft.append(per_head)
            pv = [[_dot(pb, ld[t][3]) for pb, _, _ in soft[t]] for t in range(DL_BATCH)]
            for t in range(DL_BATCH):
                qrows = ld[t][0]
                outs = [pv[t][half] * (1.0 / soft[t][half][2]) for half in range(2)]
                lses = [soft[t][half][1] + jnp.log2(soft[t][half][2]) for half in range(2)]
                obuf[g, qrows, :] = jnp.where(lo, outs[0], outs[1])
                lbuf[g, qrows, :] = jnp.where(lo, lses[0], lses[1])
            return carry

        lax.fori_loop(0, dil * nsub // DL_BATCH, blocks, 0)

    la, lb, lc = lbuf[0], lbuf[1], lbuf[2]
    m = jnp.maximum(jnp.maximum(la, lb), lc)
    ea, eb, ec = jnp.exp2(la - m), jnp.exp2(lb - m), jnp.exp2(lc - m)
    o = (ea * obuf[0] + eb * obuf[1] + ec * obuf[2]) / (ea + eb + ec)
    out_ref[...] = (o * _silu(gate_ref[...])).astype(out_ref.dtype)


def dilated_attention(u, rel_bias, batch, seq, *, tok=2048):
    n, win = u.shape
    hw = DL_HEADS * DL_HD
    npair = hw // LANES
    tok = min(tok, seq)
    nblk = seq // tok
    max_halo = DL_HALO * max(d for _, d in DL_PAIRS)
    assert tok % (DL_TQ * max(d for _, d in DL_PAIRS)) == 0 and tok % max_halo == 0
    buckets, present = _dilated_buckets()

    in_specs = [pl.BlockSpec(memory_space=pltpu.SMEM),
                pl.BlockSpec((len(DL_PAIRS), DL_TQ, DL_NK), lambda b, i, p: (0, 0, 0))]
    args = [rel_bias, jnp.asarray(buckets)]
    for gi, (_, dil) in enumerate(DL_PAIRS):
        halo = DL_HALO * dil
        hpb = tok // halo
        last = n // halo - 1

        def col(c, gi=gi):
            return lambda p: (gi * 3 + c) * npair + p

        def cur(c):
            cf = col(c)
            return pl.BlockSpec((tok, LANES), lambda b, i, p: (b * nblk + i, cf(p)))

        def prev(c, hpb=hpb):
            cf = col(c)
            return pl.BlockSpec((halo, LANES), lambda b, i, p: (jnp.maximum((b * nblk + i) * hpb - 1, 0), cf(p)))

        def nxt(c, hpb=hpb, last=last):
            cf = col(c)
            return pl.BlockSpec((halo, LANES),
                                lambda b, i, p: (jnp.minimum((b * nblk + i + 1) * hpb, last), cf(p)))

        in_specs += [cur(0), prev(1), cur(1), nxt(1), prev(2), cur(2), nxt(2)]
        args += [u] * 7
    gate_col0 = 3 * len(DL_PAIRS) * npair
    in_specs.append(pl.BlockSpec((tok, LANES), lambda b, i, p: (b * nblk + i, gate_col0 + p)))
    args.append(u)
    return pl.pallas_call(
        functools.partial(_dilated_kernel, seq=seq, tok=tok, present=present),
        grid=(batch, nblk, npair),
        in_specs=in_specs,
        out_specs=pl.BlockSpec((tok, LANES), lambda b, i, p: (b * nblk + i, p)),
        out_shape=jax.ShapeDtypeStruct((n, hw), BF16),
        scratch_shapes=[pltpu.VMEM((len(DL_PAIRS), DL_HEADS, DL_TQ, DL_NK), F32),
                        pltpu.VMEM((tok + 2 * max_halo, LANES), F32),
                        pltpu.VMEM((tok + 2 * max_halo, LANES), F32),
                        pltpu.VMEM((len(DL_PAIRS), tok, LANES), F32),
                        pltpu.VMEM((len(DL_PAIRS), tok, LANES), F32)],
        compiler_params=_params("arbitrary", "arbitrary", "arbitrary"),
        name="dilated_attention",
    )(*args)


def dilated_layer(x, g, rel_bias, w_in, w_out, batch, seq, final_g=None):
    u = proj_in(x, g, w_in.astype(BF16))
    o = dilated_attention(u, rel_bias, batch, seq)
    return proj_out(o, w_out.astype(BF16), x, final_g)


def kernel(x, norm_g, final_g, rel_bias, hgrn_lb, ssd_w_in, ssd_conv_w, ssd_conv_b, ssd_dt_bias, ssd_a_log, ssd_d,
           ssd_norm_g, ssd_w_out, hg_w_in, hg_norm_g, hg_w_out, at_w_in, at_q_norm_g, at_k_norm_g, at_w_out,
           dl_w_in, dl_w_out):
    batch, seq, d = x.shape
    depth = norm_g.shape[0]
    n_mixers = 4
    lb_sm = jax.nn.softmax(hgrn_lb.astype(F32), axis=0)
    lb_all = jnp.cumsum(lb_sm, axis=0) - lb_sm[0:1]
    h = x.reshape(batch * seq, d)
    for layer in range(depth):
        kind, slot = layer % n_mixers, layer // n_mixers
        fg = final_g if layer == depth - 1 else None
        if kind == 0:
            h = ssd_layer(h, norm_g[layer], ssd_w_in[slot], ssd_conv_w[slot], ssd_conv_b[slot], ssd_dt_bias[slot],
                          ssd_a_log[slot], ssd_d[slot], ssd_norm_g[slot], ssd_w_out[slot], batch, seq, fg)
        elif kind == 1:
            h = hgrn_layer(h, norm_g[layer], lb_all[layer], hg_w_in[slot], hg_norm_g[slot], hg_w_out[slot],
                           batch, seq, fg)
        elif kind == 2:
            h = gqa_layer(h, norm_g[layer], at_w_in[slot], at_q_norm_g[slot], at_k_norm_g[slot], at_w_out[slot],
                          batch, seq, fg)
        else:
            h = dilated_layer(h, norm_g[layer], rel_bias, dl_w_in[slot], dl_w_out[slot], batch, seq, fg)
    return h.reshape(batch, seq, d)
```

```python
import functools
import math

import jax
import jax.numpy as jnp
import numpy as np
from jax import lax
from jax.experimental import pallas as pl
from jax.experimental.pallas import tpu as pltpu

F32 = jnp.float32
BF16 = jnp.bfloat16

EPS = 1e-6
NEG_BIG = -1e30
GRID_W = 64
ROPE_THETA = 10000.0

SSD_HEADDIM = 64
SSD_HEADS = 32
SSD_GROUPS = 4
SSD_STATE = 128
SSD_CONV = 7
HG_HEADS = 8
HG_SUB = 32
AT_HEADS = 16
AT_KV = 8
AT_HD = 128
DL_PAIRS = ((128, 1), (512, 4), (2048, 16))
DL_HEADS = 16
DL_HD = 64
REL_BUCKETS = 32
REL_MAX_DIST = 1024

LANES = 128
SUBLANES = 8
CHUNK = 128
VMEM_LIMIT = 56 * 1024 * 1024


def _params(*sem):
    return pltpu.CompilerParams(dimension_semantics=sem, vmem_limit_bytes=VMEM_LIMIT)


def _sigmoid(x):
    return 0.5 * jnp.tanh(0.5 * x) + 0.5


def _silu(x):
    return x * _sigmoid(x)


def _softplus(x):
    return jnp.maximum(x, 0.0) + jnp.log(1.0 + jnp.exp(-jnp.abs(x)))


def _dot(a, b):
    return jnp.dot(a, b, preferred_element_type=F32)


def _dot_nt(a, b):
    return lax.dot_general(a, b, (((1,), (1,)), ((), ())), preferred_element_type=F32)


def _dot_tn(a, b):
    return lax.dot_general(a, b, (((0,), (0,)), ((), ())), preferred_element_type=F32)


def _prefix_sum(tri, x):
    x1 = x.astype(BF16)
    r1 = x - x1.astype(F32)
    x2 = r1.astype(BF16)
    x3 = (r1 - x2.astype(F32)).astype(BF16)
    return _dot(tri, x1) + _dot(tri, x2) + _dot(tri, x3)


def _proj_in_kernel(x_ref, g_ref, w_ref, o_ref, xn_ref):
    @pl.when(pl.program_id(1) == 0)
    def _():
        x = x_ref[...]
        ms = jnp.mean(x * x, axis=-1, keepdims=True)
        xn_ref[...] = (x * lax.rsqrt(ms + EPS) * g_ref[...]).astype(BF16)

    o_ref[...] = _dot(xn_ref[...], w_ref[...]).astype(o_ref.dtype)


def proj_in(x, g, w, *, tm=2048, tn=1024, out_dtype=F32):
    n, d = x.shape
    dout = w.shape[1]
    tm = min(tm, n)
    tn = min(tn, dout)
    assert n % tm == 0 and dout % tn == 0
    return pl.pallas_call(
        _proj_in_kernel,
        grid=(n // tm, dout // tn),
        in_specs=[pl.BlockSpec((tm, d), lambda i, j: (i, 0)),
                  pl.BlockSpec((1, d), lambda i, j: (0, 0)),
                  pl.BlockSpec((d, tn), lambda i, j: (0, j))],
        out_specs=pl.BlockSpec((tm, tn), lambda i, j: (i, j)),
        out_shape=jax.ShapeDtypeStruct((n, dout), out_dtype),
        scratch_shapes=[pltpu.VMEM((tm, d), BF16)],
        compiler_params=_params("parallel", "arbitrary"),
        name="proj_in",
    )(x, g.reshape(1, d), w)


def _proj_out_kernel(a_ref, w_ref, r_ref, g_ref, o_ref, *, final):
    y = r_ref[...] + _dot(a_ref[...], w_ref[...])
    if final:
        ms = jnp.mean(y * y, axis=-1, keepdims=True)
        y = y * lax.rsqrt(ms + EPS) * g_ref[...]
    o_ref[...] = y


def proj_out(a, w, res, final_g=None, *, tm=1024):
    n, k = a.shape
    d = w.shape[1]
    tm = min(tm, n)
    assert n % tm == 0
    g = jnp.ones((1, d), F32) if final_g is None else final_g.reshape(1, d)
    return pl.pallas_call(
        functools.partial(_proj_out_kernel, final=final_g is not None),
        grid=(n // tm,),
        in_specs=[pl.BlockSpec((tm, k), lambda i: (i, 0)),
                  pl.BlockSpec((k, d), lambda i: (0, 0)),
                  pl.BlockSpec((tm, d), lambda i: (i, 0)),
                  pl.BlockSpec((1, d), lambda i: (0, 0))],
        out_specs=pl.BlockSpec((tm, d), lambda i: (i, 0)),
        out_shape=jax.ShapeDtypeStruct((n, d), F32),
        compiler_params=_params("parallel"),
        name="proj_out",
    )(a, w, res, g)


def _ssd_conv_kernel(xp_ref, xc_ref, xn_ref, bp_ref, bc_ref, bn_ref, wx_ref, wb_ref, bx_ref, bb_ref,
                     ox_ref, ob_ref, *, nblk):
    i = pl.program_id(1)
    pad = SSD_CONV // 2

    def conv(prev_ref, cur_ref, next_ref, w_ref, b_ref, o_ref):
        rows = cur_ref.shape[0]
        prev = jnp.where(i > 0, prev_ref[...], 0.0)
        nxt = jnp.where(i < nblk - 1, next_ref[...], 0.0)
        ext = jnp.concatenate([prev, cur_ref[...], nxt], axis=0)
        acc = jnp.zeros(cur_ref.shape, F32) + b_ref[...]
        total = rows + 2 * SUBLANES
        for t in range(SSD_CONV):
            shifted = ext if t == pad else pltpu.roll(ext, (pad - t) % total, axis=0)
            acc = acc + shifted[SUBLANES:SUBLANES + rows, :] * w_ref[t:t + 1, :]
        o_ref[...] = _silu(acc)

    conv(xp_ref, xc_ref, xn_ref, wx_ref, bx_ref, ox_ref)
    conv(bp_ref, bc_ref, bn_ref, wb_ref, bb_ref, ob_ref)


def ssd_conv(u, conv_w, conv_b, batch, seq, *, tc=256):
    n = u.shape[0]
    di = SSD_HEADS * SSD_HEADDIM
    gn2 = 2 * SSD_GROUPS * SSD_STATE
    tc = min(tc, seq)
    nblk = seq // tc
    r8 = tc // SUBLANES
    last8 = n // SUBLANES - 1

    def cur(wblk):
        return lambda b, i: (b * nblk + i, wblk)

    def prev(wblk):
        return lambda b, i: (jnp.maximum((b * nblk + i) * r8 - 1, 0), wblk)

    def nxt(wblk):
        return lambda b, i: (jnp.minimum((b * nblk + i + 1) * r8, last8), wblk)

    wx, wb = conv_w[:, :di], conv_w[:, di:]
    bx, bb = conv_b[:di].reshape(1, di), conv_b[di:].reshape(1, gn2)
    const = lambda b, i: (0, 0)
    return pl.pallas_call(
        functools.partial(_ssd_conv_kernel, nblk=nblk),
        grid=(batch, nblk),
        in_specs=[pl.BlockSpec((SUBLANES, di), prev(1)), pl.BlockSpec((tc, di), cur(1)),
                  pl.BlockSpec((SUBLANES, di), nxt(1)),
                  pl.BlockSpec((SUBLANES, gn2), prev(4)), pl.BlockSpec((tc, gn2), cur(4)),
                  pl.BlockSpec((SUBLANES, gn2), nxt(4)),
                  pl.BlockSpec((SSD_CONV, di), const), pl.BlockSpec((SSD_CONV, gn2), const),
                  pl.BlockSpec((1, di), const), pl.BlockSpec((1, gn2), const)],
        out_specs=[pl.BlockSpec((tc, di), lambda b, i: (b * nblk + i, 0)),
                   pl.BlockSpec((tc, gn2), lambda b, i: (b * nblk + i, 0))],
        out_shape=[jax.ShapeDtypeStruct((n, di), F32), jax.ShapeDtypeStruct((n, gn2), F32)],
        compiler_params=_params("parallel", "parallel"),
        name="ssd_conv",
    )(u, u, u, u, u, u, wx, wb, bx, bb)


def _chunk_views(refs, is_block, reverse, body):
    rows = next(r.shape[1] for r, blk in zip(refs, is_block) if blk)
    order = range(rows // CHUNK)
    order = list(reversed(order)) if reverse else list(order)
    for cc in order:
        views = [r.at[:, pl.ds(cc * CHUNK, CHUNK)] if blk else r for r, blk in zip(refs, is_block)]
        body(views, cc == order[0])


def _ssd_scan_kernel(*refs, reverse, final):
    is_block = [True, True, True, False, False] + ([True, True, False, False] if final else []) + [True, False]
    _chunk_views(refs, is_block, reverse,
                 lambda views, first: _ssd_scan_chunk(*views, reverse=reverse, final=final, first=first))


def _ssd_scan_chunk(x_ref, bc_ref, dtr_ref, dtb_ref, alog_ref, *rest, reverse, final, first):
    if final:
        z_ref, yo_ref, dskip_ref, ng_ref, o_ref, st_ref = rest
    else:
        o_ref, st_ref = rest
    c = pl.program_id(1)
    nb = x_ref.shape[0]
    q = CHUNK
    gn = SSD_GROUPS * SSD_STATE
    hpg = SSD_HEADS // SSD_GROUPS
    gw = hpg * SSD_HEADDIM
    hoff = SSD_HEADS if reverse else 0
    far = 0 if reverse else q - 1

    if first:
        @pl.when(c == 0)
        def _():
            st_ref[...] = jnp.zeros(st_ref.shape, F32)

    row = lax.broadcasted_iota(jnp.int32, (q, q), 0)
    col = lax.broadcasted_iota(jnp.int32, (q, q), 1)
    valid = (col >= row) if reverse else (col <= row)
    tri = valid.astype(BF16)
    lane = lax.broadcasted_iota(jnp.int32, (q, LANES), 1)
    lo = lane < SSD_HEADDIM
    lo_row = lo[0:1, :]
    neg_a = -jnp.exp(alog_ref[...])

    prep = []
    for bb in range(nb):
        dt = _softplus(dtr_ref[bb] + dtb_ref[...])
        cs_col = _prefix_sum(tri, dt * neg_a)
        cs_row = cs_col.T
        dt_row = dt.T
        w_row = jnp.exp(cs_row[:, far:far + 1] - cs_row) * dt_row
        dec = jnp.exp(cs_col[far:far + 1, :])
        prep.append((cs_col, cs_row, dt_row, w_row, dec))

    for g in range(SSD_GROUPS):
        grp = []
        for bb in range(nb):
            b_f = bc_ref[bb, :, g * SSD_STATE:(g + 1) * SSD_STATE]
            c_f = bc_ref[bb, :, gn + g * SSD_STATE:gn + (g + 1) * SSD_STATE]
            c_b = c_f.astype(BF16)
            grp.append((_dot_nt(c_b, b_f.astype(BF16)).astype(BF16), c_b, b_f.T.astype(BF16)))
        y_parts = [[] for _ in range(nb)]
        for pp in range(hpg // 2):
            h0 = g * hpg + 2 * pp
            cols = slice(h0 * SSD_HEADDIM, (h0 + 2) * SSD_HEADDIM)
            scols = slice(2 * pp * SSD_HEADDIM, (2 * pp + 2) * SSD_HEADDIM)
            ops = []
            for bb in range(nb):
                cs_col, cs_row, dt_row, w_row, dec = prep[bb]
                cb, c_f, b_t = grp[bb]
                x2 = x_ref[bb, :, cols]
                x2b = x2.astype(BF16)
                s_old = st_ref[bb, g, :, scols]
                rhs = jnp.concatenate([x2b, s_old.astype(BF16)], axis=0)
                lhs, bws = [], []
                for h in (h0, h0 + 1):
                    hl = hoff + h
                    colb = jnp.broadcast_to(cs_col[:, hl:hl + 1], (q, q))
                    lmat = jnp.exp(jnp.where(valid, colb - cs_row[hl:hl + 1, :], NEG_BIG))
                    m = cb * (lmat.astype(BF16) * dt_row[hl:hl + 1, :].astype(BF16))
                    ce = c_f * jnp.exp(colb).astype(BF16)
                    lhs.append(jnp.concatenate([m, ce], axis=1))
                    bws.append(b_t * w_row[hl:hl + 1, :].astype(BF16))
                dec2 = jnp.where(lo_row, dec[:, hoff + h0:hoff + h0 + 1], dec[:, hoff + h0 + 1:hoff + h0 + 2])
                ops.append((x2, x2b, s_old, rhs, lhs, bws, dec2))
            prods = []
            for bb in range(nb):
                _, x2b, _, rhs, lhs, bws, _ = ops[bb]
                prods.append(([_dot(l, rhs) for l in lhs], [_dot(w, x2b) for w in bws]))
            for bb in range(nb):
                x2, _, s_old, _, _, _, dec2 = ops[bb]
                ys, sts = prods[bb]
                y2 = jnp.where(lo, ys[0], ys[1])
                st_ref[bb, g, :, scols] = s_old * dec2 + jnp.where(lo, sts[0], sts[1])
                if final:
                    y2 = y2 + yo_ref[bb, :, cols] + x2 * dskip_ref[:, cols]
                    y_parts[bb].append(y2 * _silu(z_ref[bb, :, cols]))
                else:
                    o_ref[bb, :, cols] = y2
        if final:
            gcols = slice(g * gw, (g + 1) * gw)
            for bb in range(nb):
                yg = jnp.concatenate(y_parts[bb], axis=1)
                ms = jnp.mean(yg * yg, axis=-1, keepdims=True)
                o_ref[bb, :, gcols] = (yg * lax.rsqrt(ms + EPS) * ng_ref[:, gcols]).astype(o_ref.dtype)


def ssd_scan(xc, bc, dtr, dt_bias, a_log, batch, seq, *, reverse, final_args=None, nb=4, chunks_per_step=1):
    n, di = xc.shape
    rows = CHUNK * min(chunks_per_step, seq // CHUNK)
    nc = seq // rows
    gn2 = bc.shape[1]
    final = final_args is not None
    nb = min(nb, batch)
    assert batch % nb == 0 and seq % rows == 0
    v3 = lambda a: a.reshape(batch, seq, a.shape[1])

    def chunk(c):
        return nc - 1 - c if reverse else c

    blk = lambda w: pl.BlockSpec((nb, rows, w), lambda b, c: (b, chunk(c), 0))
    const = lambda w: pl.BlockSpec((1, w), lambda b, c: (0, 0))
    in_specs = [blk(di), blk(gn2), blk(LANES), const(LANES), const(LANES)]
    args = [v3(xc), v3(bc), v3(dtr), dt_bias, a_log]
    if final:
        u, y_other, dskip, ng = final_args
        in_specs += [blk(di), blk(di), const(di), const(di)]
        args += [v3(u), v3(y_other), dskip, ng]
    out = pl.pallas_call(
        functools.partial(_ssd_scan_kernel, reverse=reverse, final=final),
        grid=(batch // nb, nc),
        in_specs=in_specs,
        out_specs=blk(di),
        out_shape=jax.ShapeDtypeStruct((batch, seq, di), BF16 if final else F32),
        scratch_shapes=[pltpu.VMEM((nb, SSD_GROUPS, SSD_STATE, (SSD_HEADS // SSD_GROUPS) * SSD_HEADDIM), F32)],
        compiler_params=_params("parallel", "arbitrary"),
        name="ssd_scan_rev" if reverse else "ssd_scan_fwd",
    )(*args)
    return out.reshape(n, di)


def ssd_layer(x, g, w_in, conv_w, conv_b, dt_bias, a_log, d_skip, norm_g, w_out, batch, seq, final_g=None):
    di = SSD_HEADS * SSD_HEADDIM
    main = 2 * di + 2 * SSD_GROUPS * SSD_STATE
    w_main = w_in[:, :main].astype(BF16)
    w_dt = jnp.pad(w_in[:, main:], ((0, 0), (0, LANES - 2 * SSD_HEADS))).astype(BF16)
    u = proj_in(x, g, w_main)
    dtr = proj_in(x, g, w_dt)
    xc, bc = ssd_conv(u, conv_w, conv_b, batch, seq)
    pad = lambda v: jnp.pad(v.reshape(1, 2 * SSD_HEADS), ((0, 0), (0, LANES - 2 * SSD_HEADS)))
    dtb, alog = pad(dt_bias), pad(a_log)
    y_rev = ssd_scan(xc, bc, dtr, dtb, alog, batch, seq, reverse=True)
    dskip = jnp.repeat(d_skip, SSD_HEADDIM).reshape(1, di)
    y = ssd_scan(xc, bc, dtr, dtb, alog, batch, seq, reverse=False,
                 final_args=(u, y_rev, dskip, norm_g.reshape(1, di)))
    return proj_out(y, w_out.astype(BF16), x, final_g)


def _hgrn_scan_kernel(*refs, reverse, final):
    is_block = [True, True, True, False] + ([True, True, False] if final else []) + [True, False]
    _chunk_views(refs, is_block, reverse,
                 lambda views, first: _hgrn_scan_chunk(*views, reverse=reverse, final=final, first=first))


def _hgrn_scan_chunk(q_ref, f_ref, v_ref, lb_ref, *rest, reverse, final, first):
    if final:
        gate_ref, oo_ref, ng_ref, o_ref, st_ref = rest
    else:
        o_ref, st_ref = rest
    c = pl.program_id(1)
    n = CHUNK
    nsub = n // HG_SUB
    dk = LANES

    if first:
        @pl.when(c == 0)
        def _():
            st_ref[...] = jnp.zeros(st_ref.shape, F32)

    nb = q_ref.shape[0]
    lb = lb_ref[...]
    row = lax.broadcasted_iota(jnp.int32, (n, n), 0)
    col = lax.broadcasted_iota(jnp.int32, (n, n), 1)
    valid = (col >= row) if reverse else (col <= row)
    tri = valid.astype(BF16)
    ref_row = HG_SUB // 2 - 1 if reverse else HG_SUB // 2
    far = 0 if reverse else n - 1

    prep = []
    for bb in range(nb):
        f = lb + (1.0 - lb) * _sigmoid(f_ref[bb])
        prep.append((_silu(q_ref[bb]), 1.0 - f, _prefix_sum(tri, jnp.log(f))))

    for h in range(HG_HEADS):
        cols = slice(h * dk, (h + 1) * dk)
        staged = []
        for bb in range(nb):
            qa, ka, gsum = prep[bb]
            gh, qh, kh = gsum[:, cols], qa[:, cols], ka[:, cols]
            tot = gh[far:far + 1, :]
            qp, kn, anchors = [], [], []
            for s in range(nsub):
                rs = slice(s * HG_SUB, (s + 1) * HG_SUB)
                a = gh[s * HG_SUB + ref_row:s * HG_SUB + ref_row + 1, :]
                anchors.append(a)
                qp.append((qh[rs] * jnp.exp(gh[rs] - a)).astype(BF16))
                kn.append((kh[rs] * jnp.exp(a - gh[rs])).astype(BF16))
            lhs_cols = []
            for j in range(nsub):
                lhs = []
                for i in range(nsub):
                    live = (i <= j) if reverse else (i >= j)
                    if not live:
                        lhs.append(jnp.zeros((HG_SUB, dk), BF16))
                    elif i == j:
                        lhs.append(qp[i])
                    else:
                        lhs.append(qp[i] * jnp.exp(anchors[i] - anchors[j]).astype(BF16))
                lhs_cols.append(jnp.concatenate(lhs, axis=0))
            qe = jnp.concatenate([qp[s] * jnp.exp(anchors[s]).astype(BF16) for s in range(nsub)], axis=0)
            kd = jnp.concatenate([kn[s] * jnp.exp(tot - anchors[s]).astype(BF16) for s in range(nsub)], axis=0)
            staged.append((lhs_cols, kn, qe, kd, jnp.exp(tot)))
        atts = []
        for bb in range(nb):
            lhs_cols, kn_b = staged[bb][0], staged[bb][1]
            att = jnp.concatenate([_dot_nt(lhs_cols[j], kn_b[j]) for j in range(nsub)], axis=1)
            atts.append(jnp.where(valid, att, 0.0).astype(BF16))
        for bb in range(nb):
            _, _, qe, kd, decay = staged[bb]
            vh = v_ref[bb, :, cols].astype(BF16)
            s_old = st_ref[bb, h]
            o = _dot(atts[bb], vh) + _dot_nt(qe, s_old.astype(BF16))
            st_ref[bb, h] = s_old * decay + _dot_tn(vh, kd)
            if final:
                o = o + oo_ref[bb, :, cols]
                ms = jnp.mean(o * o, axis=-1, keepdims=True)
                o = o * lax.rsqrt(ms + EPS) * ng_ref[:, cols] * _silu(gate_ref[bb, :, cols])
            o_ref[bb, :, cols] = o.astype(o_ref.dtype)


def hgrn_scan(u, lb, batch, seq, *, reverse, final_args=None, nb=4, chunks_per_step=2):
    n = u.shape[0]
    w = HG_HEADS * LANES
    rows = CHUNK * min(chunks_per_step, seq // CHUNK)
    nc = seq // rows
    final = final_args is not None
    nb = min(nb, batch)
    assert batch % nb == 0 and seq % rows == 0
    v3 = lambda a: a.reshape(batch, seq, a.shape[1])

    def chunk(c):
        return nc - 1 - c if reverse else c

    ublk = lambda j: pl.BlockSpec((nb, rows, w), lambda b, c: (b, chunk(c), j))
    const = pl.BlockSpec((1, w), lambda b, c: (0, 0))
    u3 = v3(u)
    in_specs = [ublk(0), ublk(2 if reverse else 1), ublk(3), const]
    args = [u3, u3, u3, lb.reshape(1, w)]
    if final:
        o_other, ng = final_args
        in_specs += [ublk(4), ublk(0), const]
        args += [u3, v3(o_other), ng.reshape(1, w)]
    out = pl.pallas_call(
        functools.partial(_hgrn_scan_kernel, reverse=reverse, final=final),
        grid=(batch // nb, nc),
        in_specs=in_specs,
        out_specs=ublk(0),
        out_shape=jax.ShapeDtypeStruct((batch, seq, w), BF16 if final else F32),
        scratch_shapes=[pltpu.VMEM((nb, HG_HEADS, LANES, LANES), F32)],
        compiler_params=_params("parallel", "arbitrary"),
        name="hgrn_scan_rev" if reverse else "hgrn_scan_fwd",
    )(*args)
    return out.reshape(n, w)


def hgrn_layer(x, g, lb, w_in, norm_g, w_out, batch, seq, final_g=None):
    u = proj_in(x, g, w_in.astype(BF16))
    o_rev = hgrn_scan(u, lb, batch, seq, reverse=True)
    o = hgrn_scan(u, lb, batch, seq, reverse=False, final_args=(o_rev, norm_g))
    return proj_out(o, w_out.astype(BF16), x, final_g)


def _rope_tables(seq):
    pos = np.arange(seq)
    rowp = (pos // GRID_W).astype(np.float64)
    colp = (pos % GRID_W).astype(np.float64)
    half = AT_HD // 4
    inv = ROPE_THETA ** (-np.arange(0, 2 * half, 2, dtype=np.float64) / (2 * half))
    ar, ac = rowp[:, None] * inv, colp[:, None] * inv
    cos = np.concatenate([np.cos(ar), np.cos(ac), np.cos(ar), np.cos(ac)], axis=1)
    sin = np.concatenate([-np.sin(ar), -np.sin(ac), np.sin(ar), np.sin(ac)], axis=1)
    return jnp.asarray(cos, F32), jnp.asarray(sin, F32)


def _pair_major(a, nheads):
    lead = a.shape[:-1]
    a = a.reshape(*lead, nheads, 2, 2, AT_HD // 4)
    return jnp.swapaxes(a, -3, -2).reshape(*lead, nheads * AT_HD)


def _gqa_proj_kernel(x_ref, g_ref, w_ref, cos_ref, sin_ref, qg_ref, kg_ref, qk_ref, vo_ref, go_ref, xn_ref, acc_ref,
                     *, tn, ncol):
    j = pl.program_id(1)
    nqk = (AT_HEADS + AT_KV) * AT_HD // tn
    jq = AT_HEADS * AT_HD // tn

    @pl.when(j == 0)
    def _():
        x = x_ref[...]
        ms = jnp.mean(x * x, axis=-1, keepdims=True)
        xn_ref[...] = (x * lax.rsqrt(ms + EPS) * g_ref[...]).astype(BF16)

    def heads(blk):
        cos, sin = cos_ref[...], sin_ref[...]
        gain = qg_ref[...] * (AT_HD ** -0.5 * math.log2(math.e)) if blk < jq else kg_ref[...]
        for h in range(tn // AT_HD):
            cols = slice(h * AT_HD, (h + 1) * AT_HD)
            xh = acc_ref[blk % 2, :, cols]
            ms = jnp.mean(xh * xh, axis=-1, keepdims=True)
            xn = xh * lax.rsqrt(ms + EPS) * gain
            qk_ref[:, cols] = (xn * cos + pltpu.roll(xn, AT_HD // 2, axis=1) * sin).astype(qk_ref.dtype)

    for jj in range(ncol + 1):
        @pl.when(j == jj)
        def _(jj=jj):
            if jj < ncol:
                acc_ref[jj % 2] = _dot(xn_ref[...], w_ref[...])
            prev = jj - 1
            if 0 <= prev < nqk:
                heads(prev)
            elif prev == nqk:
                vo_ref[...] = acc_ref[prev % 2].astype(vo_ref.dtype)
            elif prev > nqk:
                go_ref[...] = acc_ref[prev % 2].astype(go_ref.dtype)


def gqa_proj(x, g, w, q_g, k_g, seq, *, tm=1024, tn=1024):
    n, d = x.shape
    qw, kw = AT_HEADS * AT_HD, AT_KV * AT_HD
    tm = min(tm, seq)
    assert w.shape[1] == 2 * qw + 2 * kw and kw == tn and qw % tn == 0 and seq % tm == 0
    nb = seq // tm
    ncol = w.shape[1] // tn
    nqk = (qw + kw) // tn
    cos, sin = _rope_tables(seq)
    pos = lambda i, j: (i % nb, 0)
    const = lambda i, j: (0, 0)
    blk = lambda f: pl.BlockSpec((tm, tn), f)
    return pl.pallas_call(
        functools.partial(_gqa_proj_kernel, tn=tn, ncol=ncol),
        grid=(n // tm, ncol + 1),
        in_specs=[pl.BlockSpec((tm, d), lambda i, j: (i, 0)), pl.BlockSpec((1, d), const),
                  pl.BlockSpec((d, tn), lambda i, j: (0, jnp.minimum(j, ncol - 1))),
                  pl.BlockSpec((tm, AT_HD), pos), pl.BlockSpec((tm, AT_HD), pos),
                  pl.BlockSpec((1, AT_HD), const), pl.BlockSpec((1, AT_HD), const)],
        out_specs=[blk(lambda i, j: (i, jnp.clip(j - 1, 0, nqk - 1))), blk(lambda i, j: (i, 0)),
                   blk(lambda i, j: (i, jnp.clip(j - 2 - nqk, 0, qw // tn - 1)))],
        out_shape=[jax.ShapeDtypeStruct((n, qw + kw), BF16), jax.ShapeDtypeStruct((n, kw), BF16),
                   jax.ShapeDtypeStruct((n, qw), BF16)],
        scratch_shapes=[pltpu.VMEM((tm, d), BF16), pltpu.VMEM((2, tm, tn), F32)],
        compiler_params=_params("parallel", "arbitrary"),
        name="gqa_proj",
    )(x, g.reshape(1, d), w, cos, sin, q_g.reshape(1, AT_HD), k_g.reshape(1, AT_HD))


AT_SAFE_LOG2_RANGE = 100.0


def _gqa_flash_kernel(q_ref, k_ref, v_ref, gate_ref, o_ref, kmax_ref, *, tk):
    tq = q_ref.shape[0]
    seq = k_ref.shape[0]
    grp = AT_HEADS // AT_KV
    rows = grp * tq
    nkv = seq // tk

    @pl.when(pl.program_id(2) == 0)
    def _():
        kf = k_ref[...].astype(F32)
        k2 = jnp.max(jnp.sum(kf * kf, axis=-1, keepdims=True), axis=0, keepdims=True)
        kmax_ref[...] = jnp.broadcast_to(jnp.sqrt(k2), kmax_ref.shape)

    qs = jnp.concatenate([q_ref[:, j * AT_HD:(j + 1) * AT_HD] for j in range(grp)], axis=0)
    qf = qs.astype(F32)
    c = jnp.sqrt(jnp.sum(qf * qf, axis=-1, keepdims=True)) * kmax_ref[0:1, 0:1] * (1.0 + 2.0 ** -8)
    cmax = jnp.max(c)

    def finish(acc, l):
        o = acc * (1.0 / l)
        for j in range(grp):
            cols = slice(j * AT_HD, (j + 1) * AT_HD)
            o_ref[:, cols] = (o[j * tq:(j + 1) * tq, :] * _silu(gate_ref[:, cols])).astype(o_ref.dtype)

    bounded = 2.0 * cmax <= AT_SAFE_LOG2_RANGE

    @pl.when(bounded)
    def _():
        lvec = jnp.zeros((rows, LANES), F32)
        acc = jnp.zeros((rows, AT_HD), F32)
        for t in range(nkv):
            ks = k_ref[t * tk:(t + 1) * tk, :]
            vs = v_ref[t * tk:(t + 1) * tk, :]
            p = jnp.exp2(_dot_nt(qs, ks) - c)
            for w in range(tk // LANES):
                lvec = lvec + p[:, w * LANES:(w + 1) * LANES]
            acc = acc + _dot(p.astype(BF16), vs)
        finish(acc, jnp.sum(lvec, axis=-1, keepdims=True))

    @pl.when(jnp.logical_not(bounded))
    def _():
        def body(t, carry):
            m, l, acc = carry
            ks = k_ref[pl.ds(t * tk, tk), :]
            vs = v_ref[pl.ds(t * tk, tk), :]
            s = _dot_nt(qs, ks)
            m_new = jnp.maximum(m, jnp.max(s, axis=-1, keepdims=True))
            alpha = jnp.exp2(m - m_new)
            p = jnp.exp2(s - m_new)
            l = alpha * l + jnp.sum(p, axis=-1, keepdims=True)
            acc = alpha * acc + _dot(p.astype(BF16), vs)
            return m_new, l, acc

        init = (jnp.full((rows, 1), -jnp.inf, F32), jnp.zeros((rows, 1), F32), jnp.zeros((rows, AT_HD), F32))
        _, l, acc = lax.fori_loop(0, nkv, body, init)
        finish(acc, l)


def gqa_flash(qk, v, gate, batch, seq, *, tq=1024, tk=512):
    n = qk.shape[0]
    grp = AT_HEADS // AT_KV
    gw = grp * AT_HD
    tq = min(tq, seq)
    nq = seq // tq
    return pl.pallas_call(
        functools.partial(_gqa_flash_kernel, tk=min(tk, seq)),
        grid=(batch, AT_KV, nq),
        in_specs=[pl.BlockSpec((tq, gw), lambda b, h, i: (b * nq + i, h)),
                  pl.BlockSpec((seq, AT_HD), lambda b, h, i: (b, AT_HEADS + h)),
                  pl.BlockSpec((seq, AT_HD), lambda b, h, i: (b, h)),
                  pl.BlockSpec((tq, gw), lambda b, h, i: (b * nq + i, h))],
        out_specs=pl.BlockSpec((tq, gw), lambda b, h, i: (b * nq + i, h)),
        out_shape=jax.ShapeDtypeStruct((n, AT_HEADS * AT_HD), BF16),
        scratch_shapes=[pltpu.VMEM((SUBLANES, LANES), F32)],
        compiler_params=_params("parallel", "parallel", "arbitrary"),
        name="gqa_flash",
    )(qk, qk, v, gate)


def gqa_layer(x, g, w_in, q_g, k_g, w_out, batch, seq, final_g=None):
    nqk = AT_HEADS + AT_KV
    w = jnp.concatenate([_pair_major(w_in[:, :nqk * AT_HD], nqk), w_in[:, nqk * AT_HD:]], axis=1).astype(BF16)
    qk, v, gate = gqa_proj(x, g, w, _pair_major(q_g, 1), _pair_major(k_g, 1), seq)
    o = gqa_flash(qk, v, gate, batch, seq)
    return proj_out(o, w_out.astype(BF16), x, final_g)


def _t5_bucket_np(rel):
    half = REL_BUCKETS // 2
    exact = half // 2
    nabs = np.abs(rel)
    large = exact + (np.log(np.maximum(nabs, 1).astype(np.float32) / exact)
                     / math.log(REL_MAX_DIST / exact) * (half - exact)).astype(np.int32)
    large = np.minimum(large, half - 1)
    return np.where(rel > 0, half, 0) + np.where(nabs < exact, nabs, large)


DL_TQ = 128
DL_HALO = 64
DL_NK = DL_TQ + 2 * DL_HALO
DL_BATCH = 4


def _dilated_buckets():
    qi = np.arange(DL_TQ)[:, None]
    kj = np.arange(DL_NK)[None, :]
    tabs = [_t5_bucket_np((kj - DL_HALO - qi) * dil).astype(np.int32) for _, dil in DL_PAIRS]
    band = np.abs(kj - DL_HALO - qi) <= DL_HALO
    present = [sorted(set(t[band].tolist())) for t in tabs]
    return np.stack(tabs), present


def _dilated_kernel(relb_ref, bucket_ref, *refs, seq, tok, present):
    ng = len(DL_PAIRS)
    io = refs[:7 * ng]
    gate_ref, out_ref, bias_ref, kbuf, vbuf, obuf, lbuf = refs[7 * ng:]
    i = pl.program_id(1)
    p = pl.program_id(2)
    qi = lax.broadcasted_iota(jnp.int32, (DL_TQ, DL_NK), 0)
    kj = lax.broadcasted_iota(jnp.int32, (DL_TQ, DL_NK), 1)
    band = jnp.abs(kj - DL_HALO - qi) <= DL_HALO
    lane = lax.broadcasted_iota(jnp.int32, (DL_TQ, LANES), 1)
    lo = lane < DL_HD
    log2e = math.log2(math.e)
    scale = DL_HD ** -0.5 * log2e

    @pl.when((pl.program_id(0) == 0) & (i == 0) & (p == 0))
    def _():
        for g in range(ng):
            bk = bucket_ref[g]

            def fill(h, carry, g=g, bk=bk):
                acc = jnp.zeros((DL_TQ, DL_NK), F32)
                for b in present[g]:
                    acc = jnp.where(bk == b, relb_ref[b, h], acc)
                bias_ref[g, h] = jnp.where(band, acc * log2e, NEG_BIG)
                return carry

            lax.fori_loop(0, DL_HEADS, fill, 0)

    for g, (_, dil) in enumerate(DL_PAIRS):
        q_ref, kp_ref, kc_ref, kn_ref, vp_ref, vc_ref, vn_ref = io[7 * g:7 * g + 7]
        halo = DL_HALO * dil
        ls = seq // dil
        ppb = tok // dil
        nsub = ppb // DL_TQ
        kbuf[0:halo, :] = kp_ref[...]
        kbuf[halo:halo + tok, :] = kc_ref[...]
        kbuf[halo + tok:2 * halo + tok, :] = kn_ref[...]
        vbuf[0:halo, :] = vp_ref[...]
        vbuf[halo:halo + tok, :] = vc_ref[...]
        vbuf[halo + tok:2 * halo + tok, :] = vn_ref[...]

        def blocks(it, carry, g=g, dil=dil, ls=ls, ppb=ppb, nsub=nsub, q_ref=q_ref):
            ld = []
            for t in range(DL_BATCH):
                idx = it * DL_BATCH + t
                r = idx // nsub
                j = idx % nsub
                start = r + j * (DL_TQ * dil)
                if dil == 1:
                    qrows, krows = pl.ds(start, DL_TQ), pl.ds(start, DL_NK)
                else:
                    qrows, krows = pl.ds(start, DL_TQ, stride=dil), pl.ds(start, DL_NK, stride=dil)
                q2 = q_ref[qrows, :] * scale
                qh = [jnp.where(lo if half == 0 else ~lo, q2, 0.0).astype(BF16) for half in range(2)]
                mk = i * ppb + j * DL_TQ - DL_HALO + kj
                ld.append((qrows, qh, kbuf[krows, :].astype(BF16), vbuf[krows, :].astype(BF16),
                           (mk >= 0) & (mk < ls)))
            scores = [[_dot_nt(qh, k2) for qh in qhs] for _, qhs, k2, _, _ in ld]
            soft = []
            for t in range(DL_BATCH):
                inside = ld[t][4]
                per_head = []
                for half in range(2):
                    s = jnp.where(inside, scores[t][half] + bias_ref[g, 2 * p + half], NEG_BIG)
                    m = jnp.max(s, axis=-1, keepdims=True)
                    pexp = jnp.exp2(s - m)
                    per_head.append((pexp.astype(BF16), m, jnp.sum(pexp, axis=-1, keepdims=True)))
                soft.append(per_head)
            pv = [[_dot(pb, ld[t][3]) for pb, _, _ in soft[t]] for t in range(DL_BATCH)]
            for t in range(DL_BATCH):
                qrows = ld[t][0]
                outs = [pv[t][half] * (1.0 / soft[t][half][2]) for half in range(2)]
                lses = [soft[t][half][1] + jnp.log2(soft[t][half][2]) for half in range(2)]
                obuf[g, qrows, :] = jnp.where(lo, outs[0], outs[1])
                lbuf[g, qrows, :] = jnp.where(lo, lses[0], lses[1])
            return carry

        lax.fori_loop(0, dil * nsub // DL_BATCH, blocks, 0)

    la, lb, lc = lbuf[0], lbuf[1], lbuf[2]
    m = jnp.maximum(jnp.maximum(la, lb), lc)
    ea, eb, ec = jnp.exp2(la - m), jnp.exp2(lb - m), jnp.exp2(lc - m)
    o = (ea * obuf[0] + eb * obuf[1] + ec * obuf[2]) / (ea + eb + ec)
    out_ref[...] = (o * _silu(gate_ref[...])).astype(out_ref.dtype)


def dilated_attention(u, rel_bias, batch, seq, *, tok=2048):
    n, win = u.shape
    hw = DL_HEADS * DL_HD
    npair = hw // LANES
    tok = min(tok, seq)
    nblk = seq // tok
    max_halo = DL_HALO * max(d for _, d in DL_PAIRS)
    assert tok % (DL_TQ * max(d for _, d in DL_PAIRS)) == 0 and tok % max_halo == 0
    buckets, present = _dilated_buckets()

    in_specs = [pl.BlockSpec(memory_space=pltpu.SMEM),
                pl.BlockSpec((len(DL_PAIRS), DL_TQ, DL_NK), lambda b, i, p: (0, 0, 0))]
    args = [rel_bias, jnp.asarray(buckets)]
    for gi, (_, dil) in enumerate(DL_PAIRS):
        halo = DL_HALO * dil
        hpb = tok // halo
        last = n // halo - 1

        def col(c, gi=gi):
            return lambda p: (gi * 3 + c) * npair + p

        def cur(c):
            cf = col(c)
            return pl.BlockSpec((tok, LANES), lambda b, i, p: (b * nblk + i, cf(p)))

        def prev(c, hpb=hpb):
            cf = col(c)
            return pl.BlockSpec((halo, LANES), lambda b, i, p: (jnp.maximum((b * nblk + i) * hpb - 1, 0), cf(p)))

        def nxt(c, hpb=hpb, last=last):
            cf = col(c)
            return pl.BlockSpec((halo, LANES),
                                lambda b, i, p: (jnp.minimum((b * nblk + i + 1) * hpb, last), cf(p)))

        in_specs += [cur(0), prev(1), cur(1), nxt(1), prev(2), cur(2), nxt(2)]
        args += [u] * 7
    gate_col0 = 3 * len(DL_PAIRS) * npair
    in_specs.append(pl.BlockSpec((tok, LANES), lambda b, i, p: (b * nblk + i, gate_col0 + p)))
    args.append(u)
    return pl.pallas_call(
        functools.partial(_dilated_kernel, seq=seq, tok=tok, present=present),
        grid=(batch, nblk, npair),
        in_specs=in_specs,
        out_specs=pl.BlockSpec((tok, LANES), lambda b, i, p: (b * nblk + i, p)),
        out_shape=jax.ShapeDtypeStruct((n, hw), BF16),
        scratch_shapes=[pltpu.VMEM((len(DL_PAIRS), DL_HEADS, DL_TQ, DL_NK), F32),
                        pltpu.VMEM((tok + 2 * max_halo, LANES), F32),
                        pltpu.VMEM((tok + 2 * max_halo, LANES), F32),
                        pltpu.VMEM((len(DL_PAIRS), tok, LANES), F32),
                        pltpu.VMEM((len(DL_PAIRS), tok, LANES), F32)],
        compiler_params=_params("arbitrary", "arbitrary", "arbitrary"),
        name="dilated_attention",
    )(*args)


def dilated_layer(x, g, rel_bias, w_in, w_out, batch, seq, final_g=None):
    u = proj_in(x, g, w_in.astype(BF16))
    o = dilated_attention(u, rel_bias, batch, seq)
    return proj_out(o, w_out.astype(BF16), x, final_g)


def kernel(x, norm_g, final_g, rel_bias, hgrn_lb, ssd_w_in, ssd_conv_w, ssd_conv_b, ssd_dt_bias, ssd_a_log, ssd_d,
           ssd_norm_g, ssd_w_out, hg_w_in, hg_norm_g, hg_w_out, at_w_in, at_q_norm_g, at_k_norm_g, at_w_out,
           dl_w_in, dl_w_out):
    batch, seq, d = x.shape
    depth = norm_g.shape[0]
    n_mixers = 4
    lb_sm = jax.nn.softmax(hgrn_lb.astype(F32), axis=0)
    lb_all = jnp.cumsum(lb_sm, axis=0) - lb_sm[0:1]
    h = x.reshape(batch * seq, d)
    for layer in range(depth):
        kind, slot = layer % n_mixers, layer // n_mixers
        fg = final_g if layer == depth - 1 else None
        if kind == 0:
            h = ssd_layer(h, norm_g[layer], ssd_w_in[slot], ssd_conv_w[slot], ssd_conv_b[slot], ssd_dt_bias[slot],
                          ssd_a_log[slot], ssd_d[slot], ssd_norm_g[slot], ssd_w_out[slot], batch, seq, fg)
        elif kind == 1:
            h = hgrn_layer(h, norm_g[layer], lb_all[layer], hg_w_in[slot], hg_norm_g[slot], hg_w_out[slot],
                           batch, seq, fg)
        elif kind == 2:
            h = gqa_layer(h, norm_g[layer], at_w_in[slot], at_q_norm_g[slot], at_k_norm_g[slot], at_w_out[slot],
                          batch, seq, fg)
        else:
            h = dilated_layer(h, norm_g[layer], rel_bias, dl_w_in[slot], dl_w_out[slot], batch, seq, fg)
    return h.reshape(batch, seq, d)
```

```python
import functools
import math

import jax
import jax.numpy as jnp
import numpy as np
from jax import lax
from jax.experimental import pallas as pl
from jax.experimental.pallas import tpu as pltpu

F32 = jnp.float32
BF16 = jnp.bfloat16

EPS = 1e-6
NEG_BIG = -1e30
GRID_W = 64
ROPE_THETA = 10000.0

SSD_HEADDIM = 64
SSD_HEADS = 32
SSD_GROUPS = 4
SSD_STATE = 128
SSD_CONV = 7
HG_HEADS = 8
HG_SUB = 32
AT_HEADS = 16
AT_KV = 8
AT_HD = 128
DL_PAIRS = ((128, 1), (512, 4), (2048, 16))
DL_HEADS = 16
DL_HD = 64
REL_BUCKETS = 32
REL_MAX_DIST = 1024

LANES = 128
SUBLANES = 8
CHUNK = 128
VMEM_LIMIT = 56 * 1024 * 1024


def _params(*sem):
    return pltpu.CompilerParams(dimension_semantics=sem, vmem_limit_bytes=VMEM_LIMIT)


def _sigmoid(x):
    return 0.5 * jnp.tanh(0.5 * x) + 0.5


def _silu(x):
    return x * _sigmoid(x)


def _softplus(x):
    return jnp.maximum(x, 0.0) + jnp.log(1.0 + jnp.exp(-jnp.abs(x)))


def _dot(a, b):
    return jnp.dot(a, b, preferred_element_type=F32)


def _dot_nt(a, b):
    return lax.dot_general(a, b, (((1,), (1,)), ((), ())), preferred_element_type=F32)


def _dot_tn(a, b):
    return lax.dot_general(a, b, (((0,), (0,)), ((), ())), preferred_element_type=F32)


def _prefix_sum(tri, x):
    x1 = x.astype(BF16)
    r1 = x - x1.astype(F32)
    x2 = r1.astype(BF16)
    x3 = (r1 - x2.astype(F32)).astype(BF16)
    return _dot(tri, x1) + _dot(tri, x2) + _dot(tri, x3)


def _proj_in_kernel(x_ref, g_ref, w_ref, o_ref, xn_ref):
    @pl.when(pl.program_id(1) == 0)
    def _():
        x = x_ref[...]
        ms = jnp.mean(x * x, axis=-1, keepdims=True)
        xn_ref[...] = (x * lax.rsqrt(ms + EPS) * g_ref[...]).astype(BF16)

    o_ref[...] = _dot(xn_ref[...], w_ref[...]).astype(o_ref.dtype)


def proj_in(x, g, w, *, tm=2048, tn=1024, out_dtype=F32):
    n, d = x.shape
    dout = w.shape[1]
    tm = min(tm, n)
    tn = min(tn, dout)
    assert n % tm == 0 and dout % tn == 0
    return pl.pallas_call(
        _proj_in_kernel,
        grid=(n // tm, dout // tn),
        in_specs=[pl.BlockSpec((tm, d), lambda i, j: (i, 0)),
                  pl.BlockSpec((1, d), lambda i, j: (0, 0)),
                  pl.BlockSpec((d, tn), lambda i, j: (0, j))],
        out_specs=pl.BlockSpec((tm, tn), lambda i, j: (i, j)),
        out_shape=jax.ShapeDtypeStruct((n, dout), out_dtype),
        scratch_shapes=[pltpu.VMEM((tm, d), BF16)],
        compiler_params=_params("parallel", "arbitrary"),
        name="proj_in",
    )(x, g.reshape(1, d), w)


def _proj_out_kernel(a_ref, w_ref, r_ref, g_ref, o_ref, *, final):
    y = r_ref[...] + _dot(a_ref[...], w_ref[...])
    if final:
        ms = jnp.mean(y * y, axis=-1, keepdims=True)
        y = y * lax.rsqrt(ms + EPS) * g_ref[...]
    o_ref[...] = y


def proj_out(a, w, res, final_g=None, *, tm=1024):
    n, k = a.shape
    d = w.shape[1]
    tm = min(tm, n)
    assert n % tm == 0
    g = jnp.ones((1, d), F32) if final_g is None else final_g.reshape(1, d)
    return pl.pallas_call(
        functools.partial(_proj_out_kernel, final=final_g is not None),
        grid=(n // tm,),
        in_specs=[pl.BlockSpec((tm, k), lambda i: (i, 0)),
                  pl.BlockSpec((k, d), lambda i: (0, 0)),
                  pl.BlockSpec((tm, d), lambda i: (i, 0)),
                  pl.BlockSpec((1, d), lambda i: (0, 0))],
        out_specs=pl.BlockSpec((tm, d), lambda i: (i, 0)),
        out_shape=jax.ShapeDtypeStruct((n, d), F32),
        compiler_params=_params("parallel"),
        name="proj_out",
    )(a, w, res, g)


def _ssd_conv_kernel(xp_ref, xc_ref, xn_ref, bp_ref, bc_ref, bn_ref, wx_ref, wb_ref, bx_ref, bb_ref,
                     ox_ref, ob_ref, *, nblk):
    i = pl.program_id(1)
    pad = SSD_CONV // 2

    def conv(prev_ref, cur_ref, next_ref, w_ref, b_ref, o_ref):
        rows = cur_ref.shape[0]
        prev = jnp.where(i > 0, prev_ref[...], 0.0)
        nxt = jnp.where(i < nblk - 1, next_ref[...], 0.0)
        ext = jnp.concatenate([prev, cur_ref[...], nxt], axis=0)
        acc = jnp.zeros(cur_ref.shape, F32) + b_ref[...]
        total = rows + 2 * SUBLANES
        for t in range(SSD_CONV):
            shifted = ext if t == pad else pltpu.roll(ext, (pad - t) % total, axis=0)
            acc = acc + shifted[SUBLANES:SUBLANES + rows, :] * w_ref[t:t + 1, :]
        o_ref[...] = _silu(acc)

    conv(xp_ref, xc_ref, xn_ref, wx_ref, bx_ref, ox_ref)
    conv(bp_ref, bc_ref, bn_ref, wb_ref, bb_ref, ob_ref)


def ssd_conv(u, conv_w, conv_b, batch, seq, *, tc=256):
    n = u.shape[0]
    di = SSD_HEADS * SSD_HEADDIM
    gn2 = 2 * SSD_GROUPS * SSD_STATE
    tc = min(tc, seq)
    nblk = seq // tc
    r8 = tc // SUBLANES
    last8 = n // SUBLANES - 1

    def cur(wblk):
        return lambda b, i: (b * nblk + i, wblk)

    def prev(wblk):
        return lambda b, i: (jnp.maximum((b * nblk + i) * r8 - 1, 0), wblk)

    def nxt(wblk):
        return lambda b, i: (jnp.minimum((b * nblk + i + 1) * r8, last8), wblk)

    wx, wb = conv_w[:, :di], conv_w[:, di:]
    bx, bb = conv_b[:di].reshape(1, di), conv_b[di:].reshape(1, gn2)
    const = lambda b, i: (0, 0)
    return pl.pallas_call(
        functools.partial(_ssd_conv_kernel, nblk=nblk),
        grid=(batch, nblk),
        in_specs=[pl.BlockSpec((SUBLANES, di), prev(1)), pl.BlockSpec((tc, di), cur(1)),
                  pl.BlockSpec((SUBLANES, di), nxt(1)),
                  pl.BlockSpec((SUBLANES, gn2), prev(4)), pl.BlockSpec((tc, gn2), cur(4)),
                  pl.BlockSpec((SUBLANES, gn2), nxt(4)),
                  pl.BlockSpec((SSD_CONV, di), const), pl.BlockSpec((SSD_CONV, gn2), const),
                  pl.BlockSpec((1, di), const), pl.BlockSpec((1, gn2), const)],
        out_specs=[pl.BlockSpec((tc, di), lambda b, i: (b * nblk + i, 0)),
                   pl.BlockSpec((tc, gn2), lambda b, i: (b * nblk + i, 0))],
        out_shape=[jax.ShapeDtypeStruct((n, di), F32), jax.ShapeDtypeStruct((n, gn2), F32)],
        compiler_params=_params("parallel", "parallel"),
        name="ssd_conv",
    )(u, u, u, u, u, u, wx, wb, bx, bb)


def _ssd_scan_kernel(x_ref, bc_ref, dtr_ref, dtb_ref, alog_ref, *rest, reverse, final):
    if final:
        z_ref, yo_ref, dskip_ref, ng_ref, o_ref, st_ref = rest
    else:
        o_ref, st_ref = rest
    c = pl.program_id(1)
    nb = x_ref.shape[0]
    q = CHUNK
    gn = SSD_GROUPS * SSD_STATE
    hpg = SSD_HEADS // SSD_GROUPS
    gw = hpg * SSD_HEADDIM
    hoff = SSD_HEADS if reverse else 0
    far = 0 if reverse else q - 1

    @pl.when(c == 0)
    def _():
        st_ref[...] = jnp.zeros(st_ref.shape, F32)

    row = lax.broadcasted_iota(jnp.int32, (q, q), 0)
    col = lax.broadcasted_iota(jnp.int32, (q, q), 1)
    valid = (col >= row) if reverse else (col <= row)
    tri = valid.astype(BF16)
    lane = lax.broadcasted_iota(jnp.int32, (q, LANES), 1)
    lo = lane < SSD_HEADDIM
    lo_row = lo[0:1, :]
    neg_a = -jnp.exp(alog_ref[...])

    prep = []
    for bb in range(nb):
        dt = _softplus(dtr_ref[bb] + dtb_ref[...])
        cs_col = _prefix_sum(tri, dt * neg_a)
        cs_row = cs_col.T
        dt_row = dt.T
        w_row = jnp.exp(cs_row[:, far:far + 1] - cs_row) * dt_row
        dec = jnp.exp(cs_col[far:far + 1, :])
        prep.append((cs_col, cs_row, dt_row, w_row, dec))

    for g in range(SSD_GROUPS):
        grp = []
        for bb in range(nb):
            b_f = bc_ref[bb, :, g * SSD_STATE:(g + 1) * SSD_STATE]
            c_f = bc_ref[bb, :, gn + g * SSD_STATE:gn + (g + 1) * SSD_STATE]
            c_b = c_f.astype(BF16)
            grp.append((_dot_nt(c_b, b_f.astype(BF16)).astype(BF16), c_b, b_f.T.astype(BF16)))
        y_parts = [[] for _ in range(nb)]
        for pp in range(hpg // 2):
            h0 = g * hpg + 2 * pp
            cols = slice(h0 * SSD_HEADDIM, (h0 + 2) * SSD_HEADDIM)
            scols = slice(2 * pp * SSD_HEADDIM, (2 * pp + 2) * SSD_HEADDIM)
            ops = []
            for bb in range(nb):
                cs_col, cs_row, dt_row, w_row, dec = prep[bb]
                cb, c_f, b_t = grp[bb]
                x2 = x_ref[bb, :, cols]
                x2b = x2.astype(BF16)
                s_old = st_ref[bb, g, :, scols]
                rhs = jnp.concatenate([x2b, s_old.astype(BF16)], axis=0)
                lhs, bws = [], []
                for h in (h0, h0 + 1):
                    hl = hoff + h
                    colb = jnp.broadcast_to(cs_col[:, hl:hl + 1], (q, q))
                    lmat = jnp.exp(jnp.where(valid, colb - cs_row[hl:hl + 1, :], NEG_BIG))
                    m = cb * (lmat.astype(BF16) * dt_row[hl:hl + 1, :].astype(BF16))
                    ce = c_f * jnp.exp(colb).astype(BF16)
                    lhs.append(jnp.concatenate([m, ce], axis=1))
                    bws.append(b_t * w_row[hl:hl + 1, :].astype(BF16))
                dec2 = jnp.where(lo_row, dec[:, hoff + h0:hoff + h0 + 1], dec[:, hoff + h0 + 1:hoff + h0 + 2])
                ops.append((x2, x2b, s_old, rhs, lhs, bws, dec2))
            prods = []
            for bb in range(nb):
                _, x2b, _, rhs, lhs, bws, _ = ops[bb]
                prods.append(([_dot(l, rhs) for l in lhs], [_dot(w, x2b) for w in bws]))
            for bb in range(nb):
                x2, _, s_old, _, _, _, dec2 = ops[bb]
                ys, sts = prods[bb]
                y2 = jnp.where(lo, ys[0], ys[1])
                st_ref[bb, g, :, scols] = s_old * dec2 + jnp.where(lo, sts[0], sts[1])
                if final:
                    y2 = y2 + yo_ref[bb, :, cols] + x2 * dskip_ref[:, cols]
                    y_parts[bb].append(y2 * _silu(z_ref[bb, :, cols]))
                else:
                    o_ref[bb, :, cols] = y2
        if final:
            gcols = slice(g * gw, (g + 1) * gw)
            for bb in range(nb):
                yg = jnp.concatenate(y_parts[bb], axis=1)
                ms = jnp.mean(yg * yg, axis=-1, keepdims=True)
                o_ref[bb, :, gcols] = (yg * lax.rsqrt(ms + EPS) * ng_ref[:, gcols]).astype(o_ref.dtype)


def ssd_scan(xc, bc, dtr, dt_bias, a_log, batch, seq, *, reverse, final_args=None, nb=4):
    n, di = xc.shape
    nc = seq // CHUNK
    gn2 = bc.shape[1]
    final = final_args is not None
    nb = min(nb, batch)
    assert batch % nb == 0
    v3 = lambda a: a.reshape(batch, seq, a.shape[1])

    def chunk(c):
        return nc - 1 - c if reverse else c

    blk = lambda w: pl.BlockSpec((nb, CHUNK, w), lambda b, c: (b, chunk(c), 0))
    const = lambda w: pl.BlockSpec((1, w), lambda b, c: (0, 0))
    in_specs = [blk(di), blk(gn2), blk(LANES), const(LANES), const(LANES)]
    args = [v3(xc), v3(bc), v3(dtr), dt_bias, a_log]
    if final:
        u, y_other, dskip, ng = final_args
        in_specs += [blk(di), blk(di), const(di), const(di)]
        args += [v3(u), v3(y_other), dskip, ng]
    out = pl.pallas_call(
        functools.partial(_ssd_scan_kernel, reverse=reverse, final=final),
        grid=(batch // nb, nc),
        in_specs=in_specs,
        out_specs=blk(di),
        out_shape=jax.ShapeDtypeStruct((batch, seq, di), BF16 if final else F32),
        scratch_shapes=[pltpu.VMEM((nb, SSD_GROUPS, SSD_STATE, (SSD_HEADS // SSD_GROUPS) * SSD_HEADDIM), F32)],
        compiler_params=_params("parallel", "arbitrary"),
        name="ssd_scan_rev" if reverse else "ssd_scan_fwd",
    )(*args)
    return out.reshape(n, di)


def ssd_layer(x, g, w_in, conv_w, conv_b, dt_bias, a_log, d_skip, norm_g, w_out, batch, seq, final_g=None):
    di = SSD_HEADS * SSD_HEADDIM
    main = 2 * di + 2 * SSD_GROUPS * SSD_STATE
    w_main = w_in[:, :main].astype(BF16)
    w_dt = jnp.pad(w_in[:, main:], ((0, 0), (0, LANES - 2 * SSD_HEADS))).astype(BF16)
    u = proj_in(x, g, w_main)
    dtr = proj_in(x, g, w_dt)
    xc, bc = ssd_conv(u, conv_w, conv_b, batch, seq)
    pad = lambda v: jnp.pad(v.reshape(1, 2 * SSD_HEADS), ((0, 0), (0, LANES - 2 * SSD_HEADS)))
    dtb, alog = pad(dt_bias), pad(a_log)
    y_rev = ssd_scan(xc, bc, dtr, dtb, alog, batch, seq, reverse=True)
    dskip = jnp.repeat(d_skip, SSD_HEADDIM).reshape(1, di)
    y = ssd_scan(xc, bc, dtr, dtb, alog, batch, seq, reverse=False,
                 final_args=(u, y_rev, dskip, norm_g.reshape(1, di)))
    return proj_out(y, w_out.astype(BF16), x, final_g)


def _hgrn_scan_kernel(q_ref, f_ref, v_ref, lb_ref, *rest, reverse, final):
    if final:
        gate_ref, oo_ref, ng_ref, o_ref, st_ref = rest
    else:
        o_ref, st_ref = rest
    c = pl.program_id(1)
    n = CHUNK
    nsub = n // HG_SUB
    dk = LANES

    @pl.when(c == 0)
    def _():
        st_ref[...] = jnp.zeros(st_ref.shape, F32)

    nb = q_ref.shape[0]
    lb = lb_ref[...]
    row = lax.broadcasted_iota(jnp.int32, (n, n), 0)
    col = lax.broadcasted_iota(jnp.int32, (n, n), 1)
    valid = (col >= row) if reverse else (col <= row)
    tri = valid.astype(BF16)
    ref_row = HG_SUB // 2 - 1 if reverse else HG_SUB // 2
    far = 0 if reverse else n - 1

    prep = []
    for bb in range(nb):
        f = lb + (1.0 - lb) * _sigmoid(f_ref[bb])
        prep.append((_silu(q_ref[bb]), 1.0 - f, _prefix_sum(tri, jnp.log(f))))

    for h in range(HG_HEADS):
        cols = slice(h * dk, (h + 1) * dk)
        staged = []
        for bb in range(nb):
            qa, ka, gsum = prep[bb]
            gh, qh, kh = gsum[:, cols], qa[:, cols], ka[:, cols]
            tot = gh[far:far + 1, :]
            qp, kn, anchors = [], [], []
            for s in range(nsub):
                rs = slice(s * HG_SUB, (s + 1) * HG_SUB)
                a = gh[s * HG_SUB + ref_row:s * HG_SUB + ref_row + 1, :]
                anchors.append(a)
                qp.append((qh[rs] * jnp.exp(gh[rs] - a)).astype(BF16))
                kn.append((kh[rs] * jnp.exp(a - gh[rs])).astype(BF16))
            lhs_cols = []
            for j in range(nsub):
                lhs = []
                for i in range(nsub):
                    live = (i <= j) if reverse else (i >= j)
                    if not live:
                        lhs.append(jnp.zeros((HG_SUB, dk), BF16))
                    elif i == j:
                        lhs.append(qp[i])
                    else:
                        lhs.append(qp[i] * jnp.exp(anchors[i] - anchors[j]).astype(BF16))
                lhs_cols.append(jnp.concatenate(lhs, axis=0))
            qe = jnp.concatenate([qp[s] * jnp.exp(anchors[s]).astype(BF16) for s in range(nsub)], axis=0)
            kd = jnp.concatenate([kn[s] * jnp.exp(tot - anchors[s]).astype(BF16) for s in range(nsub)], axis=0)
            staged.append((lhs_cols, kn, qe, kd, jnp.exp(tot)))
        atts = []
        for bb in range(nb):
            lhs_cols, kn_b = staged[bb][0], staged[bb][1]
            att = jnp.concatenate([_dot_nt(lhs_cols[j], kn_b[j]) for j in range(nsub)], axis=1)
            atts.append(jnp.where(valid, att, 0.0).astype(BF16))
        for bb in range(nb):
            _, _, qe, kd, decay = staged[bb]
            vh = v_ref[bb, :, cols].astype(BF16)
            s_old = st_ref[bb, h]
            o = _dot(atts[bb], vh) + _dot_nt(qe, s_old.astype(BF16))
            st_ref[bb, h] = s_old * decay + _dot_tn(vh, kd)
            if final:
                o = o + oo_ref[bb, :, cols]
                ms = jnp.mean(o * o, axis=-1, keepdims=True)
                o = o * lax.rsqrt(ms + EPS) * ng_ref[:, cols] * _silu(gate_ref[bb, :, cols])
            o_ref[bb, :, cols] = o.astype(o_ref.dtype)


def hgrn_scan(u, lb, batch, seq, *, reverse, final_args=None, nb=4):
    n = u.shape[0]
    w = HG_HEADS * LANES
    nc = seq // CHUNK
    final = final_args is not None
    nb = min(nb, batch)
    assert batch % nb == 0
    v3 = lambda a: a.reshape(batch, seq, a.shape[1])

    def chunk(c):
        return nc - 1 - c if reverse else c

    ublk = lambda j: pl.BlockSpec((nb, CHUNK, w), lambda b, c: (b, chunk(c), j))
    const = pl.BlockSpec((1, w), lambda b, c: (0, 0))
    u3 = v3(u)
    in_specs = [ublk(0), ublk(2 if reverse else 1), ublk(3), const]
    args = [u3, u3, u3, lb.reshape(1, w)]
    if final:
        o_other, ng = final_args
        in_specs += [ublk(4), ublk(0), const]
        args += [u3, v3(o_other), ng.reshape(1, w)]
    out = pl.pallas_call(
        functools.partial(_hgrn_scan_kernel, reverse=reverse, final=final),
        grid=(batch // nb, nc),
        in_specs=in_specs,
        out_specs=ublk(0),
        out_shape=jax.ShapeDtypeStruct((batch, seq, w), BF16 if final else F32),
        scratch_shapes=[pltpu.VMEM((nb, HG_HEADS, LANES, LANES), F32)],
        compiler_params=_params("parallel", "arbitrary"),
        name="hgrn_scan_rev" if reverse else "hgrn_scan_fwd",
    )(*args)
    return out.reshape(n, w)


def hgrn_layer(x, g, lb, w_in, norm_g, w_out, batch, seq, final_g=None):
    u = proj_in(x, g, w_in.astype(BF16))
    o_rev = hgrn_scan(u, lb, batch, seq, reverse=True)
    o = hgrn_scan(u, lb, batch, seq, reverse=False, final_args=(o_rev, norm_g))
    return proj_out(o, w_out.astype(BF16), x, final_g)


def _rope_tables(seq):
    pos = np.arange(seq)
    rowp = (pos // GRID_W).astype(np.float64)
    colp = (pos % GRID_W).astype(np.float64)
    half = AT_HD // 4
    inv = ROPE_THETA ** (-np.arange(0, 2 * half, 2, dtype=np.float64) / (2 * half))
    ar, ac = rowp[:, None] * inv, colp[:, None] * inv
    cos = np.concatenate([np.cos(ar), np.cos(ac), np.cos(ar), np.cos(ac)], axis=1)
    sin = np.concatenate([-np.sin(ar), -np.sin(ac), np.sin(ar), np.sin(ac)], axis=1)
    return jnp.asarray(cos, F32), jnp.asarray(sin, F32)


def _pair_major(a, nheads):
    lead = a.shape[:-1]
    a = a.reshape(*lead, nheads, 2, 2, AT_HD // 4)
    return jnp.swapaxes(a, -3, -2).reshape(*lead, nheads * AT_HD)


def _gqa_proj_kernel(x_ref, g_ref, w_ref, cos_ref, sin_ref, qg_ref, kg_ref, qk_ref, vo_ref, go_ref, xn_ref, acc_ref,
                     *, tn, ncol):
    j = pl.program_id(1)
    nqk = (AT_HEADS + AT_KV) * AT_HD // tn
    jq = AT_HEADS * AT_HD // tn

    @pl.when(j == 0)
    def _():
        x = x_ref[...]
        ms = jnp.mean(x * x, axis=-1, keepdims=True)
        xn_ref[...] = (x * lax.rsqrt(ms + EPS) * g_ref[...]).astype(BF16)

    def heads(blk):
        cos, sin = cos_ref[...], sin_ref[...]
        gain = qg_ref[...] * (AT_HD ** -0.5 * math.log2(math.e)) if blk < jq else kg_ref[...]
        for h in range(tn // AT_HD):
            cols = slice(h * AT_HD, (h + 1) * AT_HD)
            xh = acc_ref[blk % 2, :, cols]
            ms = jnp.mean(xh * xh, axis=-1, keepdims=True)
            xn = xh * lax.rsqrt(ms + EPS) * gain
            qk_ref[:, cols] = (xn * cos + pltpu.roll(xn, AT_HD // 2, axis=1) * sin).astype(qk_ref.dtype)

    for jj in range(ncol + 1):
        @pl.when(j == jj)
        def _(jj=jj):
            if jj < ncol:
                acc_ref[jj % 2] = _dot(xn_ref[...], w_ref[...])
            prev = jj - 1
            if 0 <= prev < nqk:
                heads(prev)
            elif prev == nqk:
                vo_ref[...] = acc_ref[prev % 2].astype(vo_ref.dtype)
            elif prev > nqk:
                go_ref[...] = acc_ref[prev % 2].astype(go_ref.dtype)


def gqa_proj(x, g, w, q_g, k_g, seq, *, tm=1024, tn=1024):
    n, d = x.shape
    qw, kw = AT_HEADS * AT_HD, AT_KV * AT_HD
    tm = min(tm, seq)
    assert w.shape[1] == 2 * qw + 2 * kw and kw == tn and qw % tn == 0 and seq % tm == 0
    nb = seq // tm
    ncol = w.shape[1] // tn
    nqk = (qw + kw) // tn
    cos, sin = _rope_tables(seq)
    pos = lambda i, j: (i % nb, 0)
    const = lambda i, j: (0, 0)
    blk = lambda f: pl.BlockSpec((tm, tn), f)
    return pl.pallas_call(
        functools.partial(_gqa_proj_kernel, tn=tn, ncol=ncol),
        grid=(n // tm, ncol + 1),
        in_specs=[pl.BlockSpec((tm, d), lambda i, j: (i, 0)), pl.BlockSpec((1, d), const),
                  pl.BlockSpec((d, tn), lambda i, j: (0, jnp.minimum(j, ncol - 1))),
                  pl.BlockSpec((tm, AT_HD), pos), pl.BlockSpec((tm, AT_HD), pos),
                  pl.BlockSpec((1, AT_HD), const), pl.BlockSpec((1, AT_HD), const)],
        out_specs=[blk(lambda i, j: (i, jnp.clip(j - 1, 0, nqk - 1))), blk(lambda i, j: (i, 0)),
                   blk(lambda i, j: (i, jnp.clip(j - 2 - nqk, 0, qw // tn - 1)))],
        out_shape=[jax.ShapeDtypeStruct((n, qw + kw), BF16), jax.ShapeDtypeStruct((n, kw), BF16),
                   jax.ShapeDtypeStruct((n, qw), BF16)],
        scratch_shapes=[pltpu.VMEM((tm, d), BF16), pltpu.VMEM((2, tm, tn), F32)],
        compiler_params=_params("parallel", "arbitrary"),
        name="gqa_proj",
    )(x, g.reshape(1, d), w, cos, sin, q_g.reshape(1, AT_HD), k_g.reshape(1, AT_HD))


AT_SAFE_LOG2_RANGE = 100.0


def _gqa_flash_kernel(bound_ref, q_ref, k_ref, v_ref, gate_ref, o_ref, *, tk):
    tq = q_ref.shape[0]
    seq = k_ref.shape[0]
    grp = AT_HEADS // AT_KV
    rows = grp * tq
    nkv = seq // tk
    qs = jnp.concatenate([q_ref[:, j * AT_HD:(j + 1) * AT_HD] for j in range(grp)], axis=0)
    cmax = bound_ref[0, 0]
    c = cmax

    def finish(acc, l):
        o = acc * (1.0 / l)
        for j in range(grp):
            cols = slice(j * AT_HD, (j + 1) * AT_HD)
            o_ref[:, cols] = (o[j * tq:(j + 1) * tq, :] * _silu(gate_ref[:, cols])).astype(o_ref.dtype)

    bounded = 2.0 * cmax <= AT_SAFE_LOG2_RANGE

    @pl.when(bounded)
    def _():
        lvec = jnp.zeros((rows, LANES), F32)
        acc = jnp.zeros((rows, AT_HD), F32)
        for t in range(nkv):
            ks = k_ref[t * tk:(t + 1) * tk, :]
            vs = v_ref[t * tk:(t + 1) * tk, :]
            p = jnp.exp2(_dot_nt(qs, ks) - c)
            for w in range(tk // LANES):
                lvec = lvec + p[:, w * LANES:(w + 1) * LANES]
            acc = acc + _dot(p.astype(BF16), vs)
        finish(acc, jnp.sum(lvec, axis=-1, keepdims=True))

    @pl.when(jnp.logical_not(bounded))
    def _():
        def body(t, carry):
            m, l, acc = carry
            ks = k_ref[pl.ds(t * tk, tk), :]
            vs = v_ref[pl.ds(t * tk, tk), :]
            s = _dot_nt(qs, ks)
            m_new = jnp.maximum(m, jnp.max(s, axis=-1, keepdims=True))
            alpha = jnp.exp2(m - m_new)
            p = jnp.exp2(s - m_new)
            l = alpha * l + jnp.sum(p, axis=-1, keepdims=True)
            acc = alpha * acc + _dot(p.astype(BF16), vs)
            return m_new, l, acc

        init = (jnp.full((rows, 1), -jnp.inf, F32), jnp.zeros((rows, 1), F32), jnp.zeros((rows, AT_HD), F32))
        _, l, acc = lax.fori_loop(0, nkv, body, init)
        finish(acc, l)


def gqa_flash(qk, v, gate, q_g, k_g, batch, seq, *, tq=1024, tk=512):
    n = qk.shape[0]
    grp = AT_HEADS // AT_KV
    gw = grp * AT_HD
    tq = min(tq, seq)
    nq = seq // tq
    bound = (AT_HD * AT_HD ** -0.5 * math.log2(math.e) * (1.0 + 2.0 ** -7)
             * jnp.max(jnp.abs(q_g)) * jnp.max(jnp.abs(k_g))).astype(F32).reshape(1, 1)
    return pl.pallas_call(
        functools.partial(_gqa_flash_kernel, tk=min(tk, seq)),
        grid=(batch, AT_KV, nq),
        in_specs=[pl.BlockSpec(memory_space=pltpu.SMEM),
                  pl.BlockSpec((tq, gw), lambda b, h, i: (b * nq + i, h)),
                  pl.BlockSpec((seq, AT_HD), lambda b, h, i: (b, AT_HEADS + h)),
                  pl.BlockSpec((seq, AT_HD), lambda b, h, i: (b, h)),
                  pl.BlockSpec((tq, gw), lambda b, h, i: (b * nq + i, h))],
        out_specs=pl.BlockSpec((tq, gw), lambda b, h, i: (b * nq + i, h)),
        out_shape=jax.ShapeDtypeStruct((n, AT_HEADS * AT_HD), BF16),
        compiler_params=_params("parallel", "parallel", "parallel"),
        name="gqa_flash",
    )(bound, qk, qk, v, gate)


def gqa_layer(x, g, w_in, q_g, k_g, w_out, batch, seq, final_g=None):
    nqk = AT_HEADS + AT_KV
    w = jnp.concatenate([_pair_major(w_in[:, :nqk * AT_HD], nqk), w_in[:, nqk * AT_HD:]], axis=1).astype(BF16)
    qk, v, gate = gqa_proj(x, g, w, _pair_major(q_g, 1), _pair_major(k_g, 1), seq)
    o = gqa_flash(qk, v, gate, q_g, k_g, batch, seq)
    return proj_out(o, w_out.astype(BF16), x, final_g)


def _t5_bucket_np(rel):
    half = REL_BUCKETS // 2
    exact = half // 2
    nabs = np.abs(rel)
    large = exact + (np.log(np.maximum(nabs, 1).astype(np.float32) / exact)
                     / math.log(REL_MAX_DIST / exact) * (half - exact)).astype(np.int32)
    large = np.minimum(large, half - 1)
    return np.where(rel > 0, half, 0) + np.where(nabs < exact, nabs, large)


DL_TQ = 128
DL_HALO = 64
DL_NK = DL_TQ + 2 * DL_HALO
DL_BATCH = 4


def _dilated_buckets():
    qi = np.arange(DL_TQ)[:, None]
    kj = np.arange(DL_NK)[None, :]
    tabs = [_t5_bucket_np((kj - DL_HALO - qi) * dil).astype(np.int32) for _, dil in DL_PAIRS]
    band = np.abs(kj - DL_HALO - qi) <= DL_HALO
    present = [sorted(set(t[band].tolist())) for t in tabs]
    return np.stack(tabs), present


def _dilated_kernel(relb_ref, bucket_ref, *refs, seq, tok, present):
    ng = len(DL_PAIRS)
    io = refs[:7 * ng]
    gate_ref, out_ref, bias_ref, kbuf, vbuf, obuf, lbuf = refs[7 * ng:]
    i = pl.program_id(1)
    p = pl.program_id(2)
    qi = lax.broadcasted_iota(jnp.int32, (DL_TQ, DL_NK), 0)
    kj = lax.broadcasted_iota(jnp.int32, (DL_TQ, DL_NK), 1)
    band = jnp.abs(kj - DL_HALO - qi) <= DL_HALO
    lane = lax.broadcasted_iota(jnp.int32, (DL_TQ, LANES), 1)
    lo = lane < DL_HD
    log2e = math.log2(math.e)
    scale = DL_HD ** -0.5 * log2e

    @pl.when((pl.program_id(0) == 0) & (i == 0) & (p == 0))
    def _():
        for g in range(ng):
            bk = bucket_ref[g]

            def fill(h, carry, g=g, bk=bk):
                acc = jnp.zeros((DL_TQ, DL_NK), F32)
                for b in present[g]:
                    acc = jnp.where(bk == b, relb_ref[b, h], acc)
                bias_ref[g, h] = jnp.where(band, acc * log2e, NEG_BIG)
                return carry

            lax.fori_loop(0, DL_HEADS, fill, 0)

    for g, (_, dil) in enumerate(DL_PAIRS):
        q_ref, kp_ref, kc_ref, kn_ref, vp_ref, vc_ref, vn_ref = io[7 * g:7 * g + 7]
        halo = DL_HALO * dil
        ls = seq // dil
        ppb = tok // dil
        nsub = ppb // DL_TQ
        kbuf[0:halo, :] = kp_ref[...]
        kbuf[halo:halo + tok, :] = kc_ref[...]
        kbuf[halo + tok:2 * halo + tok, :] = kn_ref[...]
        vbuf[0:halo, :] = vp_ref[...]
        vbuf[halo:halo + tok, :] = vc_ref[...]
        vbuf[halo + tok:2 * halo + tok, :] = vn_ref[...]

        def blocks(it, carry, g=g, dil=dil, ls=ls, ppb=ppb, nsub=nsub, q_ref=q_ref):
            ld = []
            for t in range(DL_BATCH):
                idx = it * DL_BATCH + t
                r = idx // nsub
                j = idx % nsub
                start = r + j * (DL_TQ * dil)
                if dil == 1:
                    qrows, krows = pl.ds(start, DL_TQ), pl.ds(start, DL_NK)
                else:
                    qrows, krows = pl.ds(start, DL_TQ, stride=dil), pl.ds(start, DL_NK, stride=dil)
                q2 = q_ref[qrows, :] * scale
                qh = [jnp.where(lo if half == 0 else ~lo, q2, 0.0).astype(BF16) for half in range(2)]
                mk = i * ppb + j * DL_TQ - DL_HALO + kj
                ld.append((qrows, qh, kbuf[krows, :].astype(BF16), vbuf[krows, :].astype(BF16),
                           (mk >= 0) & (mk < ls)))
            scores = [[_dot_nt(qh, k2) for qh in qhs] for _, qhs, k2, _, _ in ld]
            soft = []
            for t in range(DL_BATCH):
                inside = ld[t][4]
                per_head = []
                for half in range(2):
                    s = jnp.where(inside, scores[t][half] + bias_ref[g, 2 * p + half], NEG_BIG)
                    m = jnp.max(s, axis=-1, keepdims=True)
                    pexp = jnp.exp2(s - m)
                    per_head.append((pexp.astype(BF16), m, jnp.sum(pexp, axis=-1, keepdims=True)))
                soft.append(per_head)
            pv = [[_dot(pb, ld[t][3]) for pb, _, _ in soft[t]] for t in range(DL_BATCH)]
            for t in range(DL_BATCH):
                qrows = ld[t][0]
                outs = [pv[t][half] * (1.0 / soft[t][half][2]) for half in range(2)]
                lses = [soft[t][half][1] + jnp.log2(soft[t][half][2]) for half in range(2)]
                obuf[g, qrows, :] = jnp.where(lo, outs[0], outs[1])
                lbuf[g, qrows, :] = jnp.where(lo, lses[0], lses[1])
            return carry

        lax.fori_loop(0, dil * nsub // DL_BATCH, blocks, 0)

    la, lb, lc = lbuf[0], lbuf[1], lbuf[2]
    m = jnp.maximum(jnp.maximum(la, lb), lc)
    ea, eb, ec = jnp.exp2(la - m), jnp.exp2(lb - m), jnp.exp2(lc - m)
    o = (ea * obuf[0] + eb * obuf[1] + ec * obuf[2]) / (ea + eb + ec)
    out_ref[...] = (o * _silu(gate_ref[...])).astype(out_ref.dtype)


def dilated_attention(u, rel_bias, batch, seq, *, tok=2048):
    n, win = u.shape
    hw = DL_HEADS * DL_HD
    npair = hw // LANES
    tok = min(tok, seq)
    nblk = seq // tok
    max_halo = DL_HALO * max(d for _, d in DL_PAIRS)
    assert tok % (DL_TQ * max(d for _, d in DL_PAIRS)) == 0 and tok % max_halo == 0
    buckets, present = _dilated_buckets()

    in_specs = [pl.BlockSpec(memory_space=pltpu.SMEM),
                pl.BlockSpec((len(DL_PAIRS), DL_TQ, DL_NK), lambda b, i, p: (0, 0, 0))]
    args = [rel_bias, jnp.asarray(buckets)]
    for gi, (_, dil) in enumerate(DL_PAIRS):
        halo = DL_HALO * dil
        hpb = tok // halo
        last = n // halo - 1

        def col(c, gi=gi):
            return lambda p: (gi * 3 + c) * npair + p

        def cur(c):
            cf = col(c)
            return pl.BlockSpec((tok, LANES), lambda b, i, p: (b * nblk + i, cf(p)))

        def prev(c, hpb=hpb):
            cf = col(c)
            return pl.BlockSpec((halo, LANES), lambda b, i, p: (jnp.maximum((b * nblk + i) * hpb - 1, 0), cf(p)))

        def nxt(c, hpb=hpb, last=last):
            cf = col(c)
            return pl.BlockSpec((halo, LANES),
                                lambda b, i, p: (jnp.minimum((b * nblk + i + 1) * hpb, last), cf(p)))

        in_specs += [cur(0), prev(1), cur(1), nxt(1), prev(2), cur(2), nxt(2)]
        args += [u] * 7
    gate_col0 = 3 * len(DL_PAIRS) * npair
    in_specs.append(pl.BlockSpec((tok, LANES), lambda b, i, p: (b * nblk + i, gate_col0 + p)))
    args.append(u)
    return pl.pallas_call(
        functools.partial(_dilated_kernel, seq=seq, tok=tok, present=present),
        grid=(batch, nblk, npair),
        in_specs=in_specs,
        out_specs=pl.BlockSpec((tok, LANES), lambda b, i, p: (b * nblk + i, p)),
        out_shape=jax.ShapeDtypeStruct((n, hw), BF16),
        scratch_shapes=[pltpu.VMEM((len(DL_PAIRS), DL_HEADS, DL_TQ, DL_NK), F32),
                        pltpu.VMEM((tok + 2 * max_halo, LANES), F32),
                        pltpu.VMEM((tok + 2 * max_halo, LANES), F32),
                        pltpu.VMEM((len(DL_PAIRS), tok, LANES), F32),
                        pltpu.VMEM((len(DL_PAIRS), tok, LANES), F32)],
        compiler_params=_params("arbitrary", "arbitrary", "arbitrary"),
        name="dilated_attention",
    )(*args)


def dilated_layer(x, g, rel_bias, w_in, w_out, batch, seq, final_g=None):
    u = proj_in(x, g, w_in.astype(BF16))
    o = dilated_attention(u, rel_bias, batch, seq)
    return proj_out(o, w_out.astype(BF16), x, final_g)


def kernel(x, norm_g, final_g, rel_bias, hgrn_lb, ssd_w_in, ssd_conv_w, ssd_conv_b, ssd_dt_bias, ssd_a_log, ssd_d,
           ssd_norm_g, ssd_w_out, hg_w_in, hg_norm_g, hg_w_out, at_w_in, at_q_norm_g, at_k_norm_g, at_w_out,
           dl_w_in, dl_w_out):
    batch, seq, d = x.shape
    depth = norm_g.shape[0]
    n_mixers = 4
    lb_sm = jax.nn.softmax(hgrn_lb.astype(F32), axis=0)
    lb_all = jnp.cumsum(lb_sm, axis=0) - lb_sm[0:1]
    h = x.reshape(batch * seq, d)
    for layer in range(depth):
        kind, slot = layer % n_mixers, layer // n_mixers
        fg = final_g if layer == depth - 1 else None
        if kind == 0:
            h = ssd_layer(h, norm_g[layer], ssd_w_in[slot], ssd_conv_w[slot], ssd_conv_b[slot], ssd_dt_bias[slot],
                          ssd_a_log[slot], ssd_d[slot], ssd_norm_g[slot], ssd_w_out[slot], batch, seq, fg)
        elif kind == 1:
            h = hgrn_layer(h, norm_g[layer], lb_all[layer], hg_w_in[slot], hg_norm_g[slot], hg_w_out[slot],
                           batch, seq, fg)
        elif kind == 2:
            h = gqa_layer(h, norm_g[layer], at_w_in[slot], at_q_norm_g[slot], at_k_norm_g[slot], at_w_out[slot],
                          batch, seq, fg)
        else:
            h = dilated_layer(h, norm_g[layer], rel_bias, dl_w_in[slot], dl_w_out[slot], batch, seq, fg)
    return h.reshape(batch, seq, d)
```

```python
import functools
import math

import jax
import jax.numpy as jnp
import numpy as np
from jax import lax
from jax.experimental import pallas as pl
from jax.experimental.pallas import tpu as pltpu

F32 = jnp.float32
BF16 = jnp.bfloat16

EPS = 1e-6
NEG_BIG = -1e30
GRID_W = 64
ROPE_THETA = 10000.0

SSD_HEADDIM = 64
SSD_HEADS = 32
SSD_GROUPS = 4
SSD_STATE = 128
SSD_CONV = 7
HG_HEADS = 8
HG_SUB = 32
AT_HEADS = 16
AT_KV = 8
AT_HD = 128
DL_PAIRS = ((128, 1), (512, 4), (2048, 16))
DL_HEADS = 16
DL_HD = 64
REL_BUCKETS = 32
REL_MAX_DIST = 1024

LANES = 128
SUBLANES = 8
CHUNK = 128
VMEM_LIMIT = 56 * 1024 * 1024


def _params(*sem):
    return pltpu.CompilerParams(dimension_semantics=sem, vmem_limit_bytes=VMEM_LIMIT)


def _sigmoid(x):
    return 0.5 * jnp.tanh(0.5 * x) + 0.5


def _silu(x):
    return x * _sigmoid(x)


def _softplus(x):
    return jnp.maximum(x, 0.0) + jnp.log(1.0 + jnp.exp(-jnp.abs(x)))


def _dot(a, b):
    return jnp.dot(a, b, preferred_element_type=F32)


def _dot_nt(a, b):
    return lax.dot_general(a, b, (((1,), (1,)), ((), ())), preferred_element_type=F32)


def _dot_tn(a, b):
    return lax.dot_general(a, b, (((0,), (0,)), ((), ())), preferred_element_type=F32)


def _prefix_sum(tri, x):
    x1 = x.astype(BF16)
    r1 = x - x1.astype(F32)
    x2 = r1.astype(BF16)
    x3 = (r1 - x2.astype(F32)).astype(BF16)
    return _dot(tri, x1) + _dot(tri, x2) + _dot(tri, x3)


def _proj_in_kernel(x_ref, g_ref, w_ref, o_ref, xn_ref):
    @pl.when(pl.program_id(1) == 0)
    def _():
        x = x_ref[...]
        ms = jnp.mean(x * x, axis=-1, keepdims=True)
        xn_ref[...] = (x * lax.rsqrt(ms + EPS) * g_ref[...]).astype(BF16)

    o_ref[...] = _dot(xn_ref[...], w_ref[...]).astype(o_ref.dtype)


def proj_in(x, g, w, *, tm=2048, tn=1024, out_dtype=F32):
    n, d = x.shape
    dout = w.shape[1]
    tm = min(tm, n)
    tn = min(tn, dout)
    assert n % tm == 0 and dout % tn == 0
    return pl.pallas_call(
        _proj_in_kernel,
        grid=(n // tm, dout // tn),
        in_specs=[pl.BlockSpec((tm, d), lambda i, j: (i, 0)),
                  pl.BlockSpec((1, d), lambda i, j: (0, 0)),
                  pl.BlockSpec((d, tn), lambda i, j: (0, j))],
        out_specs=pl.BlockSpec((tm, tn), lambda i, j: (i, j)),
        out_shape=jax.ShapeDtypeStruct((n, dout), out_dtype),
        scratch_shapes=[pltpu.VMEM((tm, d), BF16)],
        compiler_params=_params("parallel", "arbitrary"),
        name="proj_in",
    )(x, g.reshape(1, d), w)


def _proj_out_kernel(a_ref, w_ref, r_ref, g_ref, o_ref, *, final):
    y = r_ref[...] + _dot(a_ref[...], w_ref[...])
    if final:
        ms = jnp.mean(y * y, axis=-1, keepdims=True)
        y = y * lax.rsqrt(ms + EPS) * g_ref[...]
    o_ref[...] = y


def proj_out(a, w, res, final_g=None, *, tm=1024):
    n, k = a.shape
    d = w.shape[1]
    tm = min(tm, n)
    assert n % tm == 0
    g = jnp.ones((1, d), F32) if final_g is None else final_g.reshape(1, d)
    return pl.pallas_call(
        functools.partial(_proj_out_kernel, final=final_g is not None),
        grid=(n // tm,),
        in_specs=[pl.BlockSpec((tm, k), lambda i: (i, 0)),
                  pl.BlockSpec((k, d), lambda i: (0, 0)),
                  pl.BlockSpec((tm, d), lambda i: (i, 0)),
                  pl.BlockSpec((1, d), lambda i: (0, 0))],
        out_specs=pl.BlockSpec((tm, d), lambda i: (i, 0)),
        out_shape=jax.ShapeDtypeStruct((n, d), F32),
        compiler_params=_params("parallel"),
        name="proj_out",
    )(a, w, res, g)


def _ssd_conv_kernel(xp_ref, xc_ref, xn_ref, bp_ref, bc_ref, bn_ref, wx_ref, wb_ref, bx_ref, bb_ref,
                     ox_ref, ob_ref, *, nblk):
    i = pl.program_id(1)
    pad = SSD_CONV // 2

    def conv(prev_ref, cur_ref, next_ref, w_ref, b_ref, o_ref):
        rows = cur_ref.shape[0]
        prev = jnp.where(i > 0, prev_ref[...], 0.0)
        nxt = jnp.where(i < nblk - 1, next_ref[...], 0.0)
        ext = jnp.concatenate([prev, cur_ref[...], nxt], axis=0)
        acc = jnp.zeros(cur_ref.shape, F32) + b_ref[...]
        total = rows + 2 * SUBLANES
        for t in range(SSD_CONV):
            shifted = ext if t == pad else pltpu.roll(ext, (pad - t) % total, axis=0)
            acc = acc + shifted[SUBLANES:SUBLANES + rows, :] * w_ref[t:t + 1, :]
        o_ref[...] = _silu(acc)

    conv(xp_ref, xc_ref, xn_ref, wx_ref, bx_ref, ox_ref)
    conv(bp_ref, bc_ref, bn_ref, wb_ref, bb_ref, ob_ref)


def ssd_conv(u, conv_w, conv_b, batch, seq, *, tc=512):
    n = u.shape[0]
    di = SSD_HEADS * SSD_HEADDIM
    gn2 = 2 * SSD_GROUPS * SSD_STATE
    tc = min(tc, seq)
    nblk = seq // tc
    r8 = tc // SUBLANES
    last8 = n // SUBLANES - 1

    def cur(wblk):
        return lambda b, i: (b * nblk + i, wblk)

    def prev(wblk):
        return lambda b, i: (jnp.maximum((b * nblk + i) * r8 - 1, 0), wblk)

    def nxt(wblk):
        return lambda b, i: (jnp.minimum((b * nblk + i + 1) * r8, last8), wblk)

    wx, wb = conv_w[:, :di], conv_w[:, di:]
    bx, bb = conv_b[:di].reshape(1, di), conv_b[di:].reshape(1, gn2)
    const = lambda b, i: (0, 0)
    return pl.pallas_call(
        functools.partial(_ssd_conv_kernel, nblk=nblk),
        grid=(batch, nblk),
        in_specs=[pl.BlockSpec((SUBLANES, di), prev(1)), pl.BlockSpec((tc, di), cur(1)),
                  pl.BlockSpec((SUBLANES, di), nxt(1)),
                  pl.BlockSpec((SUBLANES, gn2), prev(4)), pl.BlockSpec((tc, gn2), cur(4)),
                  pl.BlockSpec((SUBLANES, gn2), nxt(4)),
                  pl.BlockSpec((SSD_CONV, di), const), pl.BlockSpec((SSD_CONV, gn2), const),
                  pl.BlockSpec((1, di), const), pl.BlockSpec((1, gn2), const)],
        out_specs=[pl.BlockSpec((tc, di), lambda b, i: (b * nblk + i, 0)),
                   pl.BlockSpec((tc, gn2), lambda b, i: (b * nblk + i, 0))],
        out_shape=[jax.ShapeDtypeStruct((n, di), F32), jax.ShapeDtypeStruct((n, gn2), F32)],
        compiler_params=_params("parallel", "parallel"),
        name="ssd_conv",
    )(u, u, u, u, u, u, wx, wb, bx, bb)


def _ssd_scan_kernel(x_ref, bc_ref, dtr_ref, dtb_ref, alog_ref, *rest, reverse, final):
    if final:
        z_ref, yo_ref, dskip_ref, ng_ref, o_ref, st_ref = rest
    else:
        o_ref, st_ref = rest
    c = pl.program_id(1)
    nb = x_ref.shape[0]
    q = CHUNK
    gn = SSD_GROUPS * SSD_STATE
    hpg = SSD_HEADS // SSD_GROUPS
    gw = hpg * SSD_HEADDIM
    hoff = SSD_HEADS if reverse else 0
    far = 0 if reverse else q - 1

    @pl.when(c == 0)
    def _():
        st_ref[...] = jnp.zeros(st_ref.shape, F32)

    row = lax.broadcasted_iota(jnp.int32, (q, q), 0)
    col = lax.broadcasted_iota(jnp.int32, (q, q), 1)
    valid = (col >= row) if reverse else (col <= row)
    tri = valid.astype(BF16)
    lane = lax.broadcasted_iota(jnp.int32, (q, LANES), 1)
    lo = lane < SSD_HEADDIM
    lo_row = lo[0:1, :]
    neg_a = -jnp.exp(alog_ref[...])

    prep = []
    for bb in range(nb):
        dt = _softplus(dtr_ref[bb] + dtb_ref[...])
        cs_col = _prefix_sum(tri, dt * neg_a)
        cs_row = cs_col.T
        dt_row = dt.T
        w_row = jnp.exp(cs_row[:, far:far + 1] - cs_row) * dt_row
        dec = jnp.exp(cs_col[far:far + 1, :])
        prep.append((cs_col, cs_row, dt_row, w_row, dec))

    for g in range(SSD_GROUPS):
        grp = []
        for bb in range(nb):
            b_f = bc_ref[bb, :, g * SSD_STATE:(g + 1) * SSD_STATE]
            c_f = bc_ref[bb, :, gn + g * SSD_STATE:gn + (g + 1) * SSD_STATE]
            c_b = c_f.astype(BF16)
            grp.append((_dot_nt(c_b, b_f.astype(BF16)).astype(BF16), c_b, b_f.T.astype(BF16)))
        y_parts = [[] for _ in range(nb)]
        for pp in range(hpg // 2):
            h0 = g * hpg + 2 * pp
            cols = slice(h0 * SSD_HEADDIM, (h0 + 2) * SSD_HEADDIM)
            scols = slice(2 * pp * SSD_HEADDIM, (2 * pp + 2) * SSD_HEADDIM)
            ops = []
            for bb in range(nb):
                cs_col, cs_row, dt_row, w_row, dec = prep[bb]
                cb, c_f, b_t = grp[bb]
                x2 = x_ref[bb, :, cols]
                x2b = x2.astype(BF16)
                s_old = st_ref[bb, g, :, scols]
                rhs = jnp.concatenate([x2b, s_old.astype(BF16)], axis=0)
                lhs, bws = [], []
                for h in (h0, h0 + 1):
                    hl = hoff + h
                    colb = jnp.broadcast_to(cs_col[:, hl:hl + 1], (q, q))
                    lmat = jnp.exp(jnp.where(valid, colb - cs_row[hl:hl + 1, :], NEG_BIG))
                    m = cb * (lmat.astype(BF16) * dt_row[hl:hl + 1, :].astype(BF16))
                    ce = c_f * jnp.exp(colb).astype(BF16)
                    lhs.append(jnp.concatenate([m, ce], axis=1))
                    bws.append(b_t * w_row[hl:hl + 1, :].astype(BF16))
                dec2 = jnp.where(lo_row, dec[:, hoff + h0:hoff + h0 + 1], dec[:, hoff + h0 + 1:hoff + h0 + 2])
                ops.append((x2, x2b, s_old, rhs, lhs, bws, dec2))
            prods = []
            for bb in range(nb):
                _, x2b, _, rhs, lhs, bws, _ = ops[bb]
                prods.append(([_dot(l, rhs) for l in lhs], [_dot(w, x2b) for w in bws]))
            for bb in range(nb):
                x2, _, s_old, _, _, _, dec2 = ops[bb]
                ys, sts = prods[bb]
                y2 = jnp.where(lo, ys[0], ys[1])
                st_ref[bb, g, :, scols] = s_old * dec2 + jnp.where(lo, sts[0], sts[1])
                if final:
                    y2 = y2 + yo_ref[bb, :, cols] + x2 * dskip_ref[:, cols]
                    y_parts[bb].append(y2 * _silu(z_ref[bb, :, cols]))
                else:
                    o_ref[bb, :, cols] = y2
        if final:
            gcols = slice(g * gw, (g + 1) * gw)
            for bb in range(nb):
                yg = jnp.concatenate(y_parts[bb], axis=1)
                ms = jnp.mean(yg * yg, axis=-1, keepdims=True)
                o_ref[bb, :, gcols] = (yg * lax.rsqrt(ms + EPS) * ng_ref[:, gcols]).astype(o_ref.dtype)


def ssd_scan(xc, bc, dtr, dt_bias, a_log, batch, seq, *, reverse, final_args=None, nb=4):
    n, di = xc.shape
    nc = seq // CHUNK
    gn2 = bc.shape[1]
    final = final_args is not None
    nb = min(nb, batch)
    assert batch % nb == 0
    v3 = lambda a: a.reshape(batch, seq, a.shape[1])

    def chunk(c):
        return nc - 1 - c if reverse else c

    blk = lambda w: pl.BlockSpec((nb, CHUNK, w), lambda b, c: (b, chunk(c), 0))
    const = lambda w: pl.BlockSpec((1, w), lambda b, c: (0, 0))
    in_specs = [blk(di), blk(gn2), blk(LANES), const(LANES), const(LANES)]
    args = [v3(xc), v3(bc), v3(dtr), dt_bias, a_log]
    if final:
        u, y_other, dskip, ng = final_args
        in_specs += [blk(di), blk(di), const(di), const(di)]
        args += [v3(u), v3(y_other), dskip, ng]
    out = pl.pallas_call(
        functools.partial(_ssd_scan_kernel, reverse=reverse, final=final),
        grid=(batch // nb, nc),
        in_specs=in_specs,
        out_specs=blk(di),
        out_shape=jax.ShapeDtypeStruct((batch, seq, di), BF16 if final else F32),
        scratch_shapes=[pltpu.VMEM((nb, SSD_GROUPS, SSD_STATE, (SSD_HEADS // SSD_GROUPS) * SSD_HEADDIM), F32)],
        compiler_params=_params("parallel", "arbitrary"),
        name="ssd_scan_rev" if reverse else "ssd_scan_fwd",
    )(*args)
    return out.reshape(n, di)


def ssd_layer(x, g, w_in, conv_w, conv_b, dt_bias, a_log, d_skip, norm_g, w_out, batch, seq, final_g=None):
    di = SSD_HEADS * SSD_HEADDIM
    main = 2 * di + 2 * SSD_GROUPS * SSD_STATE
    w_main = w_in[:, :main].astype(BF16)
    w_dt = jnp.pad(w_in[:, main:], ((0, 0), (0, LANES - 2 * SSD_HEADS))).astype(BF16)
    u = proj_in(x, g, w_main)
    dtr = proj_in(x, g, w_dt)
    xc, bc = ssd_conv(u, conv_w, conv_b, batch, seq)
    pad = lambda v: jnp.pad(v.reshape(1, 2 * SSD_HEADS), ((0, 0), (0, LANES - 2 * SSD_HEADS)))
    dtb, alog = pad(dt_bias), pad(a_log)
    y_rev = ssd_scan(xc, bc, dtr, dtb, alog, batch, seq, reverse=True)
    dskip = jnp.repeat(d_skip, SSD_HEADDIM).reshape(1, di)
    y = ssd_scan(xc, bc, dtr, dtb, alog, batch, seq, reverse=False,
                 final_args=(u, y_rev, dskip, norm_g.reshape(1, di)))
    return proj_out(y, w_out.astype(BF16), x, final_g)


def _hgrn_scan_kernel(q_ref, f_ref, v_ref, lb_ref, *rest, reverse, final):
    if final:
        gate_ref, oo_ref, ng_ref, o_ref, st_ref = rest
    else:
        o_ref, st_ref = rest
    c = pl.program_id(1)
    n = CHUNK
    nsub = n // HG_SUB
    dk = LANES

    @pl.when(c == 0)
    def _():
        st_ref[...] = jnp.zeros(st_ref.shape, F32)

    nb = q_ref.shape[0]
    lb = lb_ref[...]
    row = lax.broadcasted_iota(jnp.int32, (n, n), 0)
    col = lax.broadcasted_iota(jnp.int32, (n, n), 1)
    valid = (col >= row) if reverse else (col <= row)
    tri = valid.astype(BF16)
    ref_row = HG_SUB // 2 - 1 if reverse else HG_SUB // 2
    far = 0 if reverse else n - 1

    prep = []
    for bb in range(nb):
        f = lb + (1.0 - lb) * _sigmoid(f_ref[bb])
        prep.append((_silu(q_ref[bb]), 1.0 - f, _prefix_sum(tri, jnp.log(f))))

    for h in range(HG_HEADS):
        cols = slice(h * dk, (h + 1) * dk)
        staged = []
        for bb in range(nb):
            qa, ka, gsum = prep[bb]
            gh, qh, kh = gsum[:, cols], qa[:, cols], ka[:, cols]
            tot = gh[far:far + 1, :]
            qp, kn, anchors = [], [], []
            for s in range(nsub):
                rs = slice(s * HG_SUB, (s + 1) * HG_SUB)
                a = gh[s * HG_SUB + ref_row:s * HG_SUB + ref_row + 1, :]
                anchors.append(a)
                qp.append((qh[rs] * jnp.exp(gh[rs] - a)).astype(BF16))
                kn.append((kh[rs] * jnp.exp(a - gh[rs])).astype(BF16))
            lhs_cols = []
            for j in range(nsub):
                lhs = []
                for i in range(nsub):
                    live = (i <= j) if reverse else (i >= j)
                    if not live:
                        lhs.append(jnp.zeros((HG_SUB, dk), BF16))
                    elif i == j:
                        lhs.append(qp[i])
                    else:
                        lhs.append(qp[i] * jnp.exp(anchors[i] - anchors[j]).astype(BF16))
                lhs_cols.append(jnp.concatenate(lhs, axis=0))
            qe = jnp.concatenate([qp[s] * jnp.exp(anchors[s]).astype(BF16) for s in range(nsub)], axis=0)
            kd = jnp.concatenate([kn[s] * jnp.exp(tot - anchors[s]).astype(BF16) for s in range(nsub)], axis=0)
            staged.append((lhs_cols, kn, qe, kd, jnp.exp(tot)))
        atts = []
        for bb in range(nb):
            lhs_cols, kn_b = staged[bb][0], staged[bb][1]
            att = jnp.concatenate([_dot_nt(lhs_cols[j], kn_b[j]) for j in range(nsub)], axis=1)
            atts.append(jnp.where(valid, att, 0.0).astype(BF16))
        for bb in range(nb):
            _, _, qe, kd, decay = staged[bb]
            vh = v_ref[bb, :, cols].astype(BF16)
            s_old = st_ref[bb, h]
            o = _dot(atts[bb], vh) + _dot_nt(qe, s_old.astype(BF16))
            st_ref[bb, h] = s_old * decay + _dot_tn(vh, kd)
            if final:
                o = o + oo_ref[bb, :, cols]
                ms = jnp.mean(o * o, axis=-1, keepdims=True)
                o = o * lax.rsqrt(ms + EPS) * ng_ref[:, cols] * _silu(gate_ref[bb, :, cols])
            o_ref[bb, :, cols] = o.astype(o_ref.dtype)


def hgrn_scan(u, lb, batch, seq, *, reverse, final_args=None, nb=4):
    n = u.shape[0]
    w = HG_HEADS * LANES
    nc = seq // CHUNK
    final = final_args is not None
    nb = min(nb, batch)
    assert batch % nb == 0
    v3 = lambda a: a.reshape(batch, seq, a.shape[1])

    def chunk(c):
        return nc - 1 - c if reverse else c

    ublk = lambda j: pl.BlockSpec((nb, CHUNK, w), lambda b, c: (b, chunk(c), j))
    const = pl.BlockSpec((1, w), lambda b, c: (0, 0))
    u3 = v3(u)
    in_specs = [ublk(0), ublk(2 if reverse else 1), ublk(3), const]
    args = [u3, u3, u3, lb.reshape(1, w)]
    if final:
        o_other, ng = final_args
        in_specs += [ublk(4), ublk(0), const]
        args += [u3, v3(o_other), ng.reshape(1, w)]
    out = pl.pallas_call(
        functools.partial(_hgrn_scan_kernel, reverse=reverse, final=final),
        grid=(batch // nb, nc),
        in_specs=in_specs,
        out_specs=ublk(0),
        out_shape=jax.ShapeDtypeStruct((batch, seq, w), BF16 if final else F32),
        scratch_shapes=[pltpu.VMEM((nb, HG_HEADS, LANES, LANES), F32)],
        compiler_params=_params("parallel", "arbitrary"),
        name="hgrn_scan_rev" if reverse else "hgrn_scan_fwd",
    )(*args)
    return out.reshape(n, w)


def hgrn_layer(x, g, lb, w_in, norm_g, w_out, batch, seq, final_g=None):
    u = proj_in(x, g, w_in.astype(BF16))
    o_rev = hgrn_scan(u, lb, batch, seq, reverse=True)
    o = hgrn_scan(u, lb, batch, seq, reverse=False, final_args=(o_rev, norm_g))
    return proj_out(o, w_out.astype(BF16), x, final_g)


def _rope_tables(seq):
    pos = np.arange(seq)
    rowp = (pos // GRID_W).astype(np.float64)
    colp = (pos % GRID_W).astype(np.float64)
    half = AT_HD // 4
    inv = ROPE_THETA ** (-np.arange(0, 2 * half, 2, dtype=np.float64) / (2 * half))
    ar, ac = rowp[:, None] * inv, colp[:, None] * inv
    cos = np.concatenate([np.cos(ar), np.cos(ac), np.cos(ar), np.cos(ac)], axis=1)
    sin = np.concatenate([-np.sin(ar), -np.sin(ac), np.sin(ar), np.sin(ac)], axis=1)
    return jnp.asarray(cos, F32), jnp.asarray(sin, F32)


def _pair_major(a, nheads):
    lead = a.shape[:-1]
    a = a.reshape(*lead, nheads, 2, 2, AT_HD // 4)
    return jnp.swapaxes(a, -3, -2).reshape(*lead, nheads * AT_HD)


def _gqa_proj_kernel(x_ref, g_ref, w_ref, cos_ref, sin_ref, qg_ref, kg_ref, qk_ref, vo_ref, go_ref, xn_ref, acc_ref,
                     *, tn, ncol):
    j = pl.program_id(1)
    nqk = (AT_HEADS + AT_KV) * AT_HD // tn
    jq = AT_HEADS * AT_HD // tn
    nv = AT_KV * AT_HD // tn

    @pl.when(j == 0)
    def _():
        x = x_ref[...]
        ms = jnp.mean(x * x, axis=-1, keepdims=True)
        xn_ref[...] = (x * lax.rsqrt(ms + EPS) * g_ref[...]).astype(BF16)

    def heads(blk):
        cos, sin = cos_ref[...], sin_ref[...]
        gain = qg_ref[...] * (AT_HD ** -0.5 * math.log2(math.e)) if blk < jq else kg_ref[...]
        for h in range(tn // AT_HD):
            cols = slice(h * AT_HD, (h + 1) * AT_HD)
            xh = acc_ref[blk % 2, :, cols]
            ms = jnp.mean(xh * xh, axis=-1, keepdims=True)
            xn = xh * lax.rsqrt(ms + EPS) * gain
            qk_ref[:, cols] = (xn * cos + pltpu.roll(xn, AT_HD // 2, axis=1) * sin).astype(qk_ref.dtype)

    for jj in range(ncol + 1):
        @pl.when(j == jj)
        def _(jj=jj):
            if jj < ncol:
                acc_ref[jj % 2] = _dot(xn_ref[...], w_ref[...])
            prev = jj - 1
            if 0 <= prev < nqk:
                heads(prev)
            elif nqk <= prev < nqk + nv:
                vo_ref[...] = acc_ref[prev % 2].astype(vo_ref.dtype)
            elif prev >= nqk + nv:
                go_ref[...] = acc_ref[prev % 2].astype(go_ref.dtype)


def gqa_proj(x, g, w, q_g, k_g, seq, *, tm=2048, tn=512):
    n, d = x.shape
    qw, kw = AT_HEADS * AT_HD, AT_KV * AT_HD
    tm = min(tm, seq)
    assert w.shape[1] == 2 * qw + 2 * kw and kw % tn == 0 and qw % tn == 0 and seq % tm == 0
    nb = seq // tm
    ncol = w.shape[1] // tn
    nqk = (qw + kw) // tn
    nv = kw // tn
    cos, sin = _rope_tables(seq)
    pos = lambda i, j: (i % nb, 0)
    const = lambda i, j: (0, 0)
    blk = lambda f: pl.BlockSpec((tm, tn), f)
    return pl.pallas_call(
        functools.partial(_gqa_proj_kernel, tn=tn, ncol=ncol),
        grid=(n // tm, ncol + 1),
        in_specs=[pl.BlockSpec((tm, d), lambda i, j: (i, 0)), pl.BlockSpec((1, d), const),
                  pl.BlockSpec((d, tn), lambda i, j: (0, jnp.minimum(j, ncol - 1))),
                  pl.BlockSpec((tm, AT_HD), pos), pl.BlockSpec((tm, AT_HD), pos),
                  pl.BlockSpec((1, AT_HD), const), pl.BlockSpec((1, AT_HD), const)],
        out_specs=[blk(lambda i, j: (i, jnp.clip(j - 1, 0, nqk - 1))),
                   blk(lambda i, j: (i, jnp.clip(j - 1 - nqk, 0, nv - 1))),
                   blk(lambda i, j: (i, jnp.clip(j - 1 - nqk - nv, 0, qw // tn - 1)))],
        out_shape=[jax.ShapeDtypeStruct((n, qw + kw), BF16), jax.ShapeDtypeStruct((n, kw), BF16),
                   jax.ShapeDtypeStruct((n, qw), BF16)],
        scratch_shapes=[pltpu.VMEM((tm, d), BF16), pltpu.VMEM((2, tm, tn), F32)],
        compiler_params=_params("parallel", "arbitrary"),
        name="gqa_proj",
    )(x, g.reshape(1, d), w, cos, sin, q_g.reshape(1, AT_HD), k_g.reshape(1, AT_HD))


AT_SAFE_LOG2_RANGE = 100.0


def _gqa_flash_kernel(bound_ref, q_ref, k_ref, v_ref, gate_ref, o_ref, *, tk):
    tq = q_ref.shape[0]
    seq = k_ref.shape[0]
    grp = AT_HEADS // AT_KV
    rows = grp * tq
    nkv = seq // tk
    qs = jnp.concatenate([q_ref[:, j * AT_HD:(j + 1) * AT_HD] for j in range(grp)], axis=0)
    cmax = bound_ref[0, 0]
    c = cmax

    def finish(acc, l):
        o = acc * (1.0 / l)
        for j in range(grp):
            cols = slice(j * AT_HD, (j + 1) * AT_HD)
            o_ref[:, cols] = (o[j * tq:(j + 1) * tq, :] * _silu(gate_ref[:, cols])).astype(o_ref.dtype)

    bounded = 2.0 * cmax <= AT_SAFE_LOG2_RANGE

    @pl.when(bounded)
    def _():
        lvec = jnp.zeros((rows, LANES), F32)
        acc = jnp.zeros((rows, AT_HD), F32)
        for t in range(nkv):
            ks = k_ref[t * tk:(t + 1) * tk, :]
            vs = v_ref[t * tk:(t + 1) * tk, :]
            p = jnp.exp2(_dot_nt(qs, ks) - c)
            for w in range(tk // LANES):
                lvec = lvec + p[:, w * LANES:(w + 1) * LANES]
            acc = acc + _dot(p.astype(BF16), vs)
        finish(acc, jnp.sum(lvec, axis=-1, keepdims=True))

    @pl.when(jnp.logical_not(bounded))
    def _():
        def body(t, carry):
            m, l, acc = carry
            ks = k_ref[pl.ds(t * tk, tk), :]
            vs = v_ref[pl.ds(t * tk, tk), :]
            s = _dot_nt(qs, ks)
            m_new = jnp.maximum(m, jnp.max(s, axis=-1, keepdims=True))
            alpha = jnp.exp2(m - m_new)
            p = jnp.exp2(s - m_new)
            l = alpha * l + jnp.sum(p, axis=-1, keepdims=True)
            acc = alpha * acc + _dot(p.astype(BF16), vs)
            return m_new, l, acc

        init = (jnp.full((rows, 1), -jnp.inf, F32), jnp.zeros((rows, 1), F32), jnp.zeros((rows, AT_HD), F32))
        _, l, acc = lax.fori_loop(0, nkv, body, init)
        finish(acc, l)


def gqa_flash(qk, v, gate, q_g, k_g, batch, seq, *, tq=1024, tk=512):
    n = qk.shape[0]
    grp = AT_HEADS // AT_KV
    gw = grp * AT_HD
    tq = min(tq, seq)
    nq = seq // tq
    bound = (AT_HD * AT_HD ** -0.5 * math.log2(math.e) * (1.0 + 2.0 ** -7)
             * jnp.max(jnp.abs(q_g)) * jnp.max(jnp.abs(k_g))).astype(F32).reshape(1, 1)
    return pl.pallas_call(
        functools.partial(_gqa_flash_kernel, tk=min(tk, seq)),
        grid=(batch, AT_KV, nq),
        in_specs=[pl.BlockSpec(memory_space=pltpu.SMEM),
                  pl.BlockSpec((tq, gw), lambda b, h, i: (b * nq + i, h)),
                  pl.BlockSpec((seq, AT_HD), lambda b, h, i: (b, AT_HEADS + h)),
                  pl.BlockSpec((seq, AT_HD), lambda b, h, i: (b, h)),
                  pl.BlockSpec((tq, gw), lambda b, h, i: (b * nq + i, h))],
        out_specs=pl.BlockSpec((tq, gw), lambda b, h, i: (b * nq + i, h)),
        out_shape=jax.ShapeDtypeStruct((n, AT_HEADS * AT_HD), BF16),
        compiler_params=_params("parallel", "parallel", "parallel"),
        name="gqa_flash",
    )(bound, qk, qk, v, gate)


def gqa_layer(x, g, w_in, q_g, k_g, w_out, batch, seq, final_g=None):
    nqk = AT_HEADS + AT_KV
    w = jnp.concatenate([_pair_major(w_in[:, :nqk * AT_HD], nqk), w_in[:, nqk * AT_HD:]], axis=1).astype(BF16)
    qk, v, gate = gqa_proj(x, g, w, _pair_major(q_g, 1), _pair_major(k_g, 1), seq)
    o = gqa_flash(qk, v, gate, q_g, k_g, batch, seq)
    return proj_out(o, w_out.astype(BF16), x, final_g)


def _t5_bucket_np(rel):
    half = REL_BUCKETS // 2
    exact = half // 2
    nabs = np.abs(rel)
    large = exact + (np.log(np.maximum(nabs, 1).astype(np.float32) / exact)
                     / math.log(REL_MAX_DIST / exact) * (half - exact)).astype(np.int32)
    large = np.minimum(large, half - 1)
    return np.where(rel > 0, half, 0) + np.where(nabs < exact, nabs, large)


DL_TQ = 128
DL_HALO = 64
DL_NK = DL_TQ + 2 * DL_HALO
DL_BATCH = 4


def _dilated_buckets():
    qi = np.arange(DL_TQ)[:, None]
    kj = np.arange(DL_NK)[None, :]
    tabs = [_t5_bucket_np((kj - DL_HALO - qi) * dil).astype(np.int32) for _, dil in DL_PAIRS]
    band = np.abs(kj - DL_HALO - qi) <= DL_HALO
    present = [sorted(set(t[band].tolist())) for t in tabs]
    return np.stack(tabs), present


def _dilated_kernel(relb_ref, bucket_ref, *refs, seq, tok, present):
    ng = len(DL_PAIRS)
    io = refs[:7 * ng]
    gate_ref, out_ref, bias_ref, kbuf, vbuf, obuf, lbuf = refs[7 * ng:]
    i = pl.program_id(1)
    p = pl.program_id(2)
    qi = lax.broadcasted_iota(jnp.int32, (DL_TQ, DL_NK), 0)
    kj = lax.broadcasted_iota(jnp.int32, (DL_TQ, DL_NK), 1)
    band = jnp.abs(kj - DL_HALO - qi) <= DL_HALO
    lane = lax.broadcasted_iota(jnp.int32, (DL_TQ, LANES), 1)
    lo = lane < DL_HD
    log2e = math.log2(math.e)
    scale = DL_HD ** -0.5 * log2e

    @pl.when((pl.program_id(0) == 0) & (i == 0) & (p == 0))
    def _():
        for g in range(ng):
            bk = bucket_ref[g]

            def fill(h, carry, g=g, bk=bk):
                acc = jnp.zeros((DL_TQ, DL_NK), F32)
                for b in present[g]:
                    acc = jnp.where(bk == b, relb_ref[b, h], acc)
                bias_ref[g, h] = jnp.where(band, acc * log2e, NEG_BIG)
                return carry

            lax.fori_loop(0, DL_HEADS, fill, 0)

    for g, (_, dil) in enumerate(DL_PAIRS):
        q_ref, kp_ref, kc_ref, kn_ref, vp_ref, vc_ref, vn_ref = io[7 * g:7 * g + 7]
        halo = DL_HALO * dil
        ls = seq // dil
        ppb = tok // dil
        nsub = ppb // DL_TQ
        kbuf[0:halo, :] = kp_ref[...]
        kbuf[halo:halo + tok, :] = kc_ref[...]
        kbuf[halo + tok:2 * halo + tok, :] = kn_ref[...]
        vbuf[0:halo, :] = vp_ref[...]
        vbuf[halo:halo + tok, :] = vc_ref[...]
        vbuf[halo + tok:2 * halo + tok, :] = vn_ref[...]

        def blocks(it, carry, g=g, dil=dil, ls=ls, ppb=ppb, nsub=nsub, q_ref=q_ref):
            ld = []
            for t in range(DL_BATCH):
                idx = it * DL_BATCH + t
                r = idx // nsub
                j = idx % nsub
                start = r + j * (DL_TQ * dil)
                if dil == 1:
                    qrows, krows = pl.ds(start, DL_TQ), pl.ds(start, DL_NK)
                else:
                    qrows, krows = pl.ds(start, DL_TQ, stride=dil), pl.ds(start, DL_NK, stride=dil)
                q2 = q_ref[qrows, :] * scale
                qh = [jnp.where(lo if half == 0 else ~lo, q2, 0.0).astype(BF16) for half in range(2)]
                mk = i * ppb + j * DL_TQ - DL_HALO + kj
                ld.append((qrows, qh, kbuf[krows, :].astype(BF16), vbuf[krows, :].astype(BF16),
                           (mk >= 0) & (mk < ls)))
            scores = [[_dot_nt(qh, k2) for qh in qhs] for _, qhs, k2, _, _ in ld]
            soft = []
            for t in range(DL_BATCH):
                inside = ld[t][4]
                per_head = []
                for half in range(2):
                    s = jnp.where(inside, scores[t][half] + bias_ref[g, 2 * p + half], NEG_BIG)
                    m = jnp.max(s, axis=-1, keepdims=True)
                    pexp = jnp.exp2(s - m)
                    per_head.append((pexp.astype(BF16), m, jnp.sum(pexp, axis=-1, keepdims=True)))
                soft.append(per_head)
            pv = [[_dot(pb, ld[t][3]) for pb, _, _ in soft[t]] for t in range(DL_BATCH)]
            for t in range(DL_BATCH):
                qrows = ld[t][0]
                outs = [pv[t][half] * (1.0 / soft[t][half][2]) for half in range(2)]
                lses = [soft[t][half][1] + jnp.log2(soft[t][half][2]) for half in range(2)]
                obuf[g, qrows, :] = jnp.where(lo, outs[0], outs[1])
                lbuf[g, qrows, :] = jnp.where(lo, lses[0], lses[1])
            return carry

        lax.fori_loop(0, dil * nsub // DL_BATCH, blocks, 0)

    la, lb, lc = lbuf[0], lbuf[1], lbuf[2]
    m = jnp.maximum(jnp.maximum(la, lb), lc)
    ea, eb, ec = jnp.exp2(la - m), jnp.exp2(lb - m), jnp.exp2(lc - m)
    o = (ea * obuf[0] + eb * obuf[1] + ec * obuf[2]) / (ea + eb + ec)
    out_ref[...] = (o * _silu(gate_ref[...])).astype(out_ref.dtype)


def dilated_attention(u, rel_bias, batch, seq, *, tok=2048):
    n, win = u.shape
    hw = DL_HEADS * DL_HD
    npair = hw // LANES
    tok = min(tok, seq)
    nblk = seq // tok
    max_halo = DL_HALO * max(d for _, d in DL_PAIRS)
    assert tok % (DL_TQ * max(d for _, d in DL_PAIRS)) == 0 and tok % max_halo == 0
    buckets, present = _dilated_buckets()

    in_specs = [pl.BlockSpec(memory_space=pltpu.SMEM),
                pl.BlockSpec((len(DL_PAIRS), DL_TQ, DL_NK), lambda b, i, p: (0, 0, 0))]
    args = [rel_bias, jnp.asarray(buckets)]
    for gi, (_, dil) in enumerate(DL_PAIRS):
        halo = DL_HALO * dil
        hpb = tok // halo
        last = n // halo - 1

        def col(c, gi=gi):
            return lambda p: (gi * 3 + c) * npair + p

        def cur(c):
            cf = col(c)
            return pl.BlockSpec((tok, LANES), lambda b, i, p: (b * nblk + i, cf(p)))

        def prev(c, hpb=hpb):
            cf = col(c)
            return pl.BlockSpec((halo, LANES), lambda b, i, p: (jnp.maximum((b * nblk + i) * hpb - 1, 0), cf(p)))

        def nxt(c, hpb=hpb, last=last):
            cf = col(c)
            return pl.BlockSpec((halo, LANES),
                                lambda b, i, p: (jnp.minimum((b * nblk + i + 1) * hpb, last), cf(p)))

        in_specs += [cur(0), prev(1), cur(1), nxt(1), prev(2), cur(2), nxt(2)]
        args += [u] * 7
    gate_col0 = 3 * len(DL_PAIRS) * npair
    in_specs.append(pl.BlockSpec((tok, LANES), lambda b, i, p: (b * nblk + i, gate_col0 + p)))
    args.append(u)
    return pl.pallas_call(
        functools.partial(_dilated_kernel, seq=seq, tok=tok, present=present),
        grid=(batch, nblk, npair),
        in_specs=in_specs,
        out_specs=pl.BlockSpec((tok, LANES), lambda b, i, p: (b * nblk + i, p)),
        out_shape=jax.ShapeDtypeStruct((n, hw), BF16),
        scratch_shapes=[pltpu.VMEM((len(DL_PAIRS), DL_HEADS, DL_TQ, DL_NK), F32),
                        pltpu.VMEM((tok + 2 * max_halo, LANES), F32),
                        pltpu.VMEM((tok + 2 * max_halo, LANES), F32),
                        pltpu.VMEM((len(DL_PAIRS), tok, LANES), F32),
                        pltpu.VMEM((len(DL_PAIRS), tok, LANES), F32)],
        compiler_params=_params("arbitrary", "arbitrary", "arbitrary"),
        name="dilated_attention",
    )(*args)


def dilated_layer(x, g, rel_bias, w_in, w_out, batch, seq, final_g=None):
    u = proj_in(x, g, w_in.astype(BF16))
    o = dilated_attention(u, rel_bias, batch, seq)
    return proj_out(o, w_out.astype(BF16), x, final_g)


def kernel(x, norm_g, final_g, rel_bias, hgrn_lb, ssd_w_in, ssd_conv_w, ssd_conv_b, ssd_dt_bias, ssd_a_log, ssd_d,
           ssd_norm_g, ssd_w_out, hg_w_in, hg_norm_g, hg_w_out, at_w_in, at_q_norm_g, at_k_norm_g, at_w_out,
           dl_w_in, dl_w_out):
    batch, seq, d = x.shape
    depth = norm_g.shape[0]
    n_mixers = 4
    lb_sm = jax.nn.softmax(hgrn_lb.astype(F32), axis=0)
    lb_all = jnp.cumsum(lb_sm, axis=0) - lb_sm[0:1]
    h = x.reshape(batch * seq, d)
    for layer in range(depth):
        kind, slot = layer % n_mixers, layer // n_mixers
        fg = final_g if layer == depth - 1 else None
        if kind == 0:
            h = ssd_layer(h, norm_g[layer], ssd_w_in[slot], ssd_conv_w[slot], ssd_conv_b[slot], ssd_dt_bias[slot],
                          ssd_a_log[slot], ssd_d[slot], ssd_norm_g[slot], ssd_w_out[slot], batch, seq, fg)
        elif kind == 1:
            h = hgrn_layer(h, norm_g[layer], lb_all[layer], hg_w_in[slot], hg_norm_g[slot], hg_w_out[slot],
                           batch, seq, fg)
        elif kind == 2:
            h = gqa_layer(h, norm_g[layer], at_w_in[slot], at_q_norm_g[slot], at_k_norm_g[slot], at_w_out[slot],
                          batch, seq, fg)
        else:
            h = dilated_layer(h, norm_g[layer], rel_bias, dl_w_in[slot], dl_w_out[slot], batch, seq, fg)
    return h.reshape(batch, seq, d)
```

```python
import functools
import math

import jax
import jax.numpy as jnp
import numpy as np
from jax import lax
from jax.experimental import pallas as pl
from jax.experimental.pallas import tpu as pltpu

F32 = jnp.float32
BF16 = jnp.bfloat16

EPS = 1e-6
NEG_BIG = -1e30
GRID_W = 64
ROPE_THETA = 10000.0

SSD_HEADDIM = 64
SSD_HEADS = 32
SSD_GROUPS = 4
SSD_STATE = 128
SSD_CONV = 7
HG_HEADS = 8
HG_SUB = 32
AT_HEADS = 16
AT_KV = 8
AT_HD = 128
DL_PAIRS = ((128, 1), (512, 4), (2048, 16))
DL_HEADS = 16
DL_HD = 64
REL_BUCKETS = 32
REL_MAX_DIST = 1024

LANES = 128
SUBLANES = 8
CHUNK = 128
VMEM_LIMIT = 56 * 1024 * 1024


def _params(*sem):
    return pltpu.CompilerParams(dimension_semantics=sem, vmem_limit_bytes=VMEM_LIMIT)


def _sigmoid(x):
    return 0.5 * jnp.tanh(0.5 * x) + 0.5


def _silu(x):
    return x * _sigmoid(x)


def _softplus(x):
    return jnp.maximum(x, 0.0) + jnp.log(1.0 + jnp.exp(-jnp.abs(x)))


def _dot(a, b):
    return jnp.dot(a, b, preferred_element_type=F32)


def _dot_nt(a, b):
    return lax.dot_general(a, b, (((1,), (1,)), ((), ())), preferred_element_type=F32)


def _dot_tn(a, b):
    return lax.dot_general(a, b, (((0,), (0,)), ((), ())), preferred_element_type=F32)


def _prefix_sum(tri, x):
    x1 = x.astype(BF16)
    r1 = x - x1.astype(F32)
    x2 = r1.astype(BF16)
    x3 = (r1 - x2.astype(F32)).astype(BF16)
    return _dot(tri, x1) + _dot(tri, x2) + _dot(tri, x3)


def _proj_in_kernel(x_ref, g_ref, w_ref, o_ref, xn_ref):
    @pl.when(pl.program_id(1) == 0)
    def _():
        x = x_ref[...]
        ms = jnp.mean(x * x, axis=-1, keepdims=True)
        xn_ref[...] = (x * lax.rsqrt(ms + EPS) * g_ref[...]).astype(BF16)

    o_ref[...] = _dot(xn_ref[...], w_ref[...]).astype(o_ref.dtype)


def proj_in(x, g, w, *, tm=2048, tn=1024, out_dtype=F32):
    n, d = x.shape
    dout = w.shape[1]
    tm = min(tm, n)
    tn = min(tn, dout)
    assert n % tm == 0 and dout % tn == 0
    return pl.pallas_call(
        _proj_in_kernel,
        grid=(n // tm, dout // tn),
        in_specs=[pl.BlockSpec((tm, d), lambda i, j: (i, 0)),
                  pl.BlockSpec((1, d), lambda i, j: (0, 0)),
                  pl.BlockSpec((d, tn), lambda i, j: (0, j))],
        out_specs=pl.BlockSpec((tm, tn), lambda i, j: (i, j)),
        out_shape=jax.ShapeDtypeStruct((n, dout), out_dtype),
        scratch_shapes=[pltpu.VMEM((tm, d), BF16)],
        compiler_params=_params("parallel", "arbitrary"),
        name="proj_in",
    )(x, g.reshape(1, d), w)


def _proj_out_kernel(a_ref, w_ref, r_ref, g_ref, o_ref, *, final):
    y = r_ref[...] + _dot(a_ref[...], w_ref[...])
    if final:
        ms = jnp.mean(y * y, axis=-1, keepdims=True)
        y = y * lax.rsqrt(ms + EPS) * g_ref[...]
    o_ref[...] = y


def proj_out(a, w, res, final_g=None, *, tm=1024):
    n, k = a.shape
    d = w.shape[1]
    tm = min(tm, n)
    assert n % tm == 0
    g = jnp.ones((1, d), F32) if final_g is None else final_g.reshape(1, d)
    return pl.pallas_call(
        functools.partial(_proj_out_kernel, final=final_g is not None),
        grid=(n // tm,),
        in_specs=[pl.BlockSpec((tm, k), lambda i: (i, 0)),
                  pl.BlockSpec((k, d), lambda i: (0, 0)),
                  pl.BlockSpec((tm, d), lambda i: (i, 0)),
                  pl.BlockSpec((1, d), lambda i: (0, 0))],
        out_specs=pl.BlockSpec((tm, d), lambda i: (i, 0)),
        out_shape=jax.ShapeDtypeStruct((n, d), F32),
        compiler_params=_params("parallel"),
        name="proj_out",
    )(a, w, res, g)


def _ssd_conv_kernel(xp_ref, xc_ref, xn_ref, bp_ref, bc_ref, bn_ref, wx_ref, wb_ref, bx_ref, bb_ref,
                     ox_ref, ob_ref, *, nblk):
    i = pl.program_id(1)
    pad = SSD_CONV // 2

    def conv(prev_ref, cur_ref, next_ref, w_ref, b_ref, o_ref):
        rows = cur_ref.shape[0]
        prev = jnp.where(i > 0, prev_ref[...], 0.0)
        nxt = jnp.where(i < nblk - 1, next_ref[...], 0.0)
        ext = jnp.concatenate([prev, cur_ref[...], nxt], axis=0)
        acc = jnp.zeros(cur_ref.shape, F32) + b_ref[...]
        total = rows + 2 * SUBLANES
        for t in range(SSD_CONV):
            shifted = ext if t == pad else pltpu.roll(ext, (pad - t) % total, axis=0)
            acc = acc + shifted[SUBLANES:SUBLANES + rows, :] * w_ref[t:t + 1, :]
        o_ref[...] = _silu(acc)

    conv(xp_ref, xc_ref, xn_ref, wx_ref, bx_ref, ox_ref)
    conv(bp_ref, bc_ref, bn_ref, wb_ref, bb_ref, ob_ref)


def ssd_conv(u, conv_w, conv_b, batch, seq, *, tc=512):
    n = u.shape[0]
    di = SSD_HEADS * SSD_HEADDIM
    gn2 = 2 * SSD_GROUPS * SSD_STATE
    tc = min(tc, seq)
    nblk = seq // tc
    r8 = tc // SUBLANES
    last8 = n // SUBLANES - 1

    def cur(wblk):
        return lambda b, i: (b * nblk + i, wblk)

    def prev(wblk):
        return lambda b, i: (jnp.maximum((b * nblk + i) * r8 - 1, 0), wblk)

    def nxt(wblk):
        return lambda b, i: (jnp.minimum((b * nblk + i + 1) * r8, last8), wblk)

    wx, wb = conv_w[:, :di], conv_w[:, di:]
    bx, bb = conv_b[:di].reshape(1, di), conv_b[di:].reshape(1, gn2)
    const = lambda b, i: (0, 0)
    return pl.pallas_call(
        functools.partial(_ssd_conv_kernel, nblk=nblk),
        grid=(batch, nblk),
        in_specs=[pl.BlockSpec((SUBLANES, di), prev(1)), pl.BlockSpec((tc, di), cur(1)),
                  pl.BlockSpec((SUBLANES, di), nxt(1)),
                  pl.BlockSpec((SUBLANES, gn2), prev(4)), pl.BlockSpec((tc, gn2), cur(4)),
                  pl.BlockSpec((SUBLANES, gn2), nxt(4)),
                  pl.BlockSpec((SSD_CONV, di), const), pl.BlockSpec((SSD_CONV, gn2), const),
                  pl.BlockSpec((1, di), const), pl.BlockSpec((1, gn2), const)],
        out_specs=[pl.BlockSpec((tc, di), lambda b, i: (b * nblk + i, 0)),
                   pl.BlockSpec((tc, gn2), lambda b, i: (b * nblk + i, 0))],
        out_shape=[jax.ShapeDtypeStruct((n, di), F32), jax.ShapeDtypeStruct((n, gn2), F32)],
        compiler_params=_params("parallel", "parallel"),
        name="ssd_conv",
    )(u, u, u, u, u, u, wx, wb, bx, bb)


def _ssd_scan_kernel(x_ref, bc_ref, dtr_ref, dtb_ref, alog_ref, *rest, reverse, final):
    if final:
        z_ref, yo_ref, dskip_ref, ng_ref, o_ref, st_ref = rest
    else:
        o_ref, st_ref = rest
    c = pl.program_id(1)
    nb = x_ref.shape[0]
    q = CHUNK
    gn = SSD_GROUPS * SSD_STATE
    hpg = SSD_HEADS // SSD_GROUPS
    gw = hpg * SSD_HEADDIM
    hoff = SSD_HEADS if reverse else 0
    far = 0 if reverse else q - 1

    @pl.when(c == 0)
    def _():
        st_ref[...] = jnp.zeros(st_ref.shape, F32)

    row = lax.broadcasted_iota(jnp.int32, (q, q), 0)
    col = lax.broadcasted_iota(jnp.int32, (q, q), 1)
    valid = (col >= row) if reverse else (col <= row)
    tri = valid.astype(BF16)
    lane = lax.broadcasted_iota(jnp.int32, (q, LANES), 1)
    lo = lane < SSD_HEADDIM
    lo_row = lo[0:1, :]
    neg_a = -jnp.exp(alog_ref[...])

    prep = []
    for bb in range(nb):
        dt = _softplus(dtr_ref[bb] + dtb_ref[...])
        cs_col = _prefix_sum(tri, dt * neg_a)
        cs_row = cs_col.T
        dt_row = dt.T
        w_row = jnp.exp(cs_row[:, far:far + 1] - cs_row) * dt_row
        dec = jnp.exp(cs_col[far:far + 1, :])
        prep.append((cs_col, cs_row, dt_row, w_row, dec))

    for g in range(SSD_GROUPS):
        grp = []
        for bb in range(nb):
            b_f = bc_ref[bb, :, g * SSD_STATE:(g + 1) * SSD_STATE]
            c_f = bc_ref[bb, :, gn + g * SSD_STATE:gn + (g + 1) * SSD_STATE]
            c_b = c_f.astype(BF16)
            grp.append((_dot_nt(c_b, b_f.astype(BF16)).astype(BF16), c_b, b_f.T.astype(BF16)))
        y_parts = [[] for _ in range(nb)]
        for pp in range(hpg // 2):
            h0 = g * hpg + 2 * pp
            cols = slice(h0 * SSD_HEADDIM, (h0 + 2) * SSD_HEADDIM)
            scols = slice(2 * pp * SSD_HEADDIM, (2 * pp + 2) * SSD_HEADDIM)
            ops = []
            for bb in range(nb):
                cs_col, cs_row, dt_row, w_row, dec = prep[bb]
                cb, c_f, b_t = grp[bb]
                x2 = x_ref[bb, :, cols]
                x2b = x2.astype(BF16)
                s_old = st_ref[bb, g, :, scols]
                rhs = jnp.concatenate([x2b, s_old.astype(BF16)], axis=0)
                lhs, bws = [], []
                for h in (h0, h0 + 1):
                    hl = hoff + h
                    colb = jnp.broadcast_to(cs_col[:, hl:hl + 1], (q, q))
                    lmat = jnp.exp(jnp.where(valid, colb - cs_row[hl:hl + 1, :], NEG_BIG))
                    m = cb * (lmat.astype(BF16) * dt_row[hl:hl + 1, :].astype(BF16))
                    ce = c_f * jnp.exp(colb).astype(BF16)
                    lhs.append(jnp.concatenate([m, ce], axis=1))
                    bws.append(b_t * w_row[hl:hl + 1, :].astype(BF16))
                dec2 = jnp.where(lo_row, dec[:, hoff + h0:hoff + h0 + 1], dec[:, hoff + h0 + 1:hoff + h0 + 2])
                ops.append((x2, x2b, s_old, rhs, lhs, bws, dec2))
            prods = []
            for bb in range(nb):
                _, x2b, _, rhs, lhs, bws, _ = ops[bb]
                prods.append(([_dot(l, rhs) for l in lhs], [_dot(w, x2b) for w in bws]))
            for bb in range(nb):
                x2, _, s_old, _, _, _, dec2 = ops[bb]
                ys, sts = prods[bb]
                y2 = jnp.where(lo, ys[0], ys[1])
                st_ref[bb, g, :, scols] = s_old * dec2 + jnp.where(lo, sts[0], sts[1])
                if final:
                    y2 = y2 + yo_ref[bb, :, cols] + x2 * dskip_ref[:, cols]
                    y_parts[bb].append(y2 * _silu(z_ref[bb, :, cols]))
                else:
                    o_ref[bb, :, cols] = y2
        if final:
            gcols = slice(g * gw, (g + 1) * gw)
            for bb in range(nb):
                yg = jnp.concatenate(y_parts[bb], axis=1)
                ms = jnp.mean(yg * yg, axis=-1, keepdims=True)
                o_ref[bb, :, gcols] = (yg * lax.rsqrt(ms + EPS) * ng_ref[:, gcols]).astype(o_ref.dtype)


def ssd_scan(xc, bc, dtr, dt_bias, a_log, batch, seq, *, reverse, final_args=None, nb=4):
    n, di = xc.shape
    nc = seq // CHUNK
    gn2 = bc.shape[1]
    final = final_args is not None
    nb = min(nb, batch)
    assert batch % nb == 0
    v3 = lambda a: a.reshape(batch, seq, a.shape[1])

    def chunk(c):
        return nc - 1 - c if reverse else c

    blk = lambda w: pl.BlockSpec((nb, CHUNK, w), lambda b, c: (b, chunk(c), 0))
    const = lambda w: pl.BlockSpec((1, w), lambda b, c: (0, 0))
    in_specs = [blk(di), blk(gn2), blk(LANES), const(LANES), const(LANES)]
    args = [v3(xc), v3(bc), v3(dtr), dt_bias, a_log]
    if final:
        u, y_other, dskip, ng = final_args
        in_specs += [blk(di), blk(di), const(di), const(di)]
        args += [v3(u), v3(y_other), dskip, ng]
    out = pl.pallas_call(
        functools.partial(_ssd_scan_kernel, reverse=reverse, final=final),
        grid=(batch // nb, nc),
        in_specs=in_specs,
        out_specs=blk(di),
        out_shape=jax.ShapeDtypeStruct((batch, seq, di), BF16 if final else F32),
        scratch_shapes=[pltpu.VMEM((nb, SSD_GROUPS, SSD_STATE, (SSD_HEADS // SSD_GROUPS) * SSD_HEADDIM), F32)],
        compiler_params=_params("parallel", "arbitrary"),
        name="ssd_scan_rev" if reverse else "ssd_scan_fwd",
    )(*args)
    return out.reshape(n, di)


def ssd_layer(x, g, w_in, conv_w, conv_b, dt_bias, a_log, d_skip, norm_g, w_out, batch, seq, final_g=None):
    di = SSD_HEADS * SSD_HEADDIM
    main = 2 * di + 2 * SSD_GROUPS * SSD_STATE
    w_main = w_in[:, :main].astype(BF16)
    w_dt = jnp.pad(w_in[:, main:], ((0, 0), (0, LANES - 2 * SSD_HEADS))).astype(BF16)
    u = proj_in(x, g, w_main)
    dtr = proj_in(x, g, w_dt)
    xc, bc = ssd_conv(u, conv_w, conv_b, batch, seq)
    pad = lambda v: jnp.pad(v.reshape(1, 2 * SSD_HEADS), ((0, 0), (0, LANES - 2 * SSD_HEADS)))
    dtb, alog = pad(dt_bias), pad(a_log)
    y_rev = ssd_scan(xc, bc, dtr, dtb, alog, batch, seq, reverse=True)
    dskip = jnp.repeat(d_skip, SSD_HEADDIM).reshape(1, di)
    y = ssd_scan(xc, bc, dtr, dtb, alog, batch, seq, reverse=False,
                 final_args=(u, y_rev, dskip, norm_g.reshape(1, di)))
    return proj_out(y, w_out.astype(BF16), x, final_g)


def _hgrn_scan_kernel(q_ref, f_ref, v_ref, lb_ref, *rest, reverse, final):
    if final:
        gate_ref, oo_ref, ng_ref, o_ref, st_ref = rest
    else:
        o_ref, st_ref = rest
    c = pl.program_id(1)
    n = CHUNK
    nsub = n // HG_SUB
    dk = LANES

    @pl.when(c == 0)
    def _():
        st_ref[...] = jnp.zeros(st_ref.shape, F32)

    nb = q_ref.shape[0]
    lb = lb_ref[...]
    row = lax.broadcasted_iota(jnp.int32, (n, n), 0)
    col = lax.broadcasted_iota(jnp.int32, (n, n), 1)
    valid = (col >= row) if reverse else (col <= row)
    tri = valid.astype(BF16)
    ref_row = HG_SUB // 2 - 1 if reverse else HG_SUB // 2
    far = 0 if reverse else n - 1

    prep = []
    for bb in range(nb):
        f = lb + (1.0 - lb) * _sigmoid(f_ref[bb])
        prep.append((_silu(q_ref[bb]), 1.0 - f, _prefix_sum(tri, jnp.log(f))))

    for h in range(HG_HEADS):
        cols = slice(h * dk, (h + 1) * dk)
        staged = []
        for bb in range(nb):
            qa, ka, gsum = prep[bb]
            gh, qh, kh = gsum[:, cols], qa[:, cols], ka[:, cols]
            tot = gh[far:far + 1, :]
            qp, kn, anchors = [], [], []
            for s in range(nsub):
                rs = slice(s * HG_SUB, (s + 1) * HG_SUB)
                a = gh[s * HG_SUB + ref_row:s * HG_SUB + ref_row + 1, :]
                anchors.append(a)
                qp.append((qh[rs] * jnp.exp(gh[rs] - a)).astype(BF16))
                kn.append((kh[rs] * jnp.exp(a - gh[rs])).astype(BF16))
            lhs_cols = []
            for j in range(nsub):
                lhs = []
                for i in range(nsub):
                    live = (i <= j) if reverse else (i >= j)
                    if not live:
                        lhs.append(jnp.zeros((HG_SUB, dk), BF16))
                    elif i == j:
                        lhs.append(qp[i])
                    else:
                        lhs.append(qp[i] * jnp.exp(anchors[i] - anchors[j]).astype(BF16))
                lhs_cols.append(jnp.concatenate(lhs, axis=0))
            qe = jnp.concatenate([qp[s] * jnp.exp(anchors[s]).astype(BF16) for s in range(nsub)], axis=0)
            kd = jnp.concatenate([kn[s] * jnp.exp(tot - anchors[s]).astype(BF16) for s in range(nsub)], axis=0)
            staged.append((lhs_cols, kn, qe, kd, jnp.exp(tot)))
        atts = []
        for bb in range(nb):
            lhs_cols, kn_b = staged[bb][0], staged[bb][1]
            att = jnp.concatenate([_dot_nt(lhs_cols[j], kn_b[j]) for j in range(nsub)], axis=1)
            atts.append(jnp.where(valid, att, 0.0).astype(BF16))
        for bb in range(nb):
            _, _, qe, kd, decay = staged[bb]
            vh = v_ref[bb, :, cols].astype(BF16)
            s_old = st_ref[bb, h]
            o = _dot(atts[bb], vh) + _dot_nt(qe, s_old.astype(BF16))
            st_ref[bb, h] = s_old * decay + _dot_tn(vh, kd)
            if final:
                o = o + oo_ref[bb, :, cols]
                ms = jnp.mean(o * o, axis=-1, keepdims=True)
                o = o * lax.rsqrt(ms + EPS) * ng_ref[:, cols] * _silu(gate_ref[bb, :, cols])
            o_ref[bb, :, cols] = o.astype(o_ref.dtype)


def hgrn_scan(u, lb, batch, seq, *, reverse, final_args=None, nb=4):
    n = u.shape[0]
    w = HG_HEADS * LANES
    nc = seq // CHUNK
    final = final_args is not None
    nb = min(nb, batch)
    assert batch % nb == 0
    v3 = lambda a: a.reshape(batch, seq, a.shape[1])

    def chunk(c):
        return nc - 1 - c if reverse else c

    ublk = lambda j: pl.BlockSpec((nb, CHUNK, w), lambda b, c: (b, chunk(c), j))
    const = pl.BlockSpec((1, w), lambda b, c: (0, 0))
    u3 = v3(u)
    in_specs = [ublk(0), ublk(2 if reverse else 1), ublk(3), const]
    args = [u3, u3, u3, lb.reshape(1, w)]
    if final:
        o_other, ng = final_args
        in_specs += [ublk(4), ublk(0), const]
        args += [u3, v3(o_other), ng.reshape(1, w)]
    out = pl.pallas_call(
        functools.partial(_hgrn_scan_kernel, reverse=reverse, final=final),
        grid=(batch // nb, nc),
        in_specs=in_specs,
        out_specs=ublk(0),
        out_shape=jax.ShapeDtypeStruct((batch, seq, w), BF16 if final else F32),
        scratch_shapes=[pltpu.VMEM((nb, HG_HEADS, LANES, LANES), F32)],
        compiler_params=_params("parallel", "arbitrary"),
        name="hgrn_scan_rev" if reverse else "hgrn_scan_fwd",
    )(*args)
    return out.reshape(n, w)


def hgrn_layer(x, g, lb, w_in, norm_g, w_out, batch, seq, final_g=None):
    u = proj_in(x, g, w_in.astype(BF16))
    o_rev = hgrn_scan(u, lb, batch, seq, reverse=True)
    o = hgrn_scan(u, lb, batch, seq, reverse=False, final_args=(o_rev, norm_g))
    return proj_out(o, w_out.astype(BF16), x, final_g)


def _rope_tables(seq):
    pos = np.arange(seq)
    rowp = (pos // GRID_W).astype(np.float64)
    colp = (pos % GRID_W).astype(np.float64)
    half = AT_HD // 4
    inv = ROPE_THETA ** (-np.arange(0, 2 * half, 2, dtype=np.float64) / (2 * half))
    ar, ac = rowp[:, None] * inv, colp[:, None] * inv
    cos = np.concatenate([np.cos(ar), np.cos(ac), np.cos(ar), np.cos(ac)], axis=1)
    sin = np.concatenate([-np.sin(ar), -np.sin(ac), np.sin(ar), np.sin(ac)], axis=1)
    return jnp.asarray(cos, F32), jnp.asarray(sin, F32)


def _pair_major(a, nheads):
    lead = a.shape[:-1]
    a = a.reshape(*lead, nheads, 2, 2, AT_HD // 4)
    return jnp.swapaxes(a, -3, -2).reshape(*lead, nheads * AT_HD)


def _gqa_proj_kernel(x_ref, g_ref, w_ref, cos_ref, sin_ref, qg_ref, kg_ref, qk_ref, vo_ref, go_ref, xn_ref, acc_ref,
                     *, tn, ncol):
    j = pl.program_id(1)
    nqk = (AT_HEADS + AT_KV) * AT_HD // tn
    jq = AT_HEADS * AT_HD // tn
    nv = AT_KV * AT_HD // tn

    @pl.when(j == 0)
    def _():
        x = x_ref[...]
        ms = jnp.mean(x * x, axis=-1, keepdims=True)
        xn_ref[...] = (x * lax.rsqrt(ms + EPS) * g_ref[...]).astype(BF16)

    def heads(blk):
        cos, sin = cos_ref[...], sin_ref[...]
        gain = qg_ref[...] * (AT_HD ** -0.5 * math.log2(math.e)) if blk < jq else kg_ref[...]
        for h in range(tn // AT_HD):
            cols = slice(h * AT_HD, (h + 1) * AT_HD)
            xh = acc_ref[blk % 2, :, cols]
            ms = jnp.mean(xh * xh, axis=-1, keepdims=True)
            xn = xh * lax.rsqrt(ms + EPS) * gain
            qk_ref[:, cols] = (xn * cos + pltpu.roll(xn, AT_HD // 2, axis=1) * sin).astype(qk_ref.dtype)

    for jj in range(ncol + 1):
        @pl.when(j == jj)
        def _(jj=jj):
            if jj < ncol:
                acc_ref[jj % 2] = _dot(xn_ref[...], w_ref[...])
            prev = jj - 1
            if 0 <= prev < nqk:
                heads(prev)
            elif nqk <= prev < nqk + nv:
                vo_ref[...] = acc_ref[prev % 2].astype(vo_ref.dtype)
            elif prev >= nqk + nv:
                go_ref[...] = acc_ref[prev % 2].astype(go_ref.dtype)


def gqa_proj(x, g, w, q_g, k_g, seq, *, tm=1024, tn=1024):
    n, d = x.shape
    qw, kw = AT_HEADS * AT_HD, AT_KV * AT_HD
    tm = min(tm, seq)
    assert w.shape[1] == 2 * qw + 2 * kw and kw % tn == 0 and qw % tn == 0 and seq % tm == 0
    nb = seq // tm
    ncol = w.shape[1] // tn
    nqk = (qw + kw) // tn
    nv = kw // tn
    cos, sin = _rope_tables(seq)
    pos = lambda i, j: (i % nb, 0)
    const = lambda i, j: (0, 0)
    blk = lambda f: pl.BlockSpec((tm, tn), f)
    return pl.pallas_call(
        functools.partial(_gqa_proj_kernel, tn=tn, ncol=ncol),
        grid=(n // tm, ncol + 1),
        in_specs=[pl.BlockSpec((tm, d), lambda i, j: (i, 0)), pl.BlockSpec((1, d), const),
                  pl.BlockSpec((d, tn), lambda i, j: (0, jnp.minimum(j, ncol - 1))),
                  pl.BlockSpec((tm, AT_HD), pos), pl.BlockSpec((tm, AT_HD), pos),
                  pl.BlockSpec((1, AT_HD), const), pl.BlockSpec((1, AT_HD), const)],
        out_specs=[blk(lambda i, j: (i, jnp.clip(j - 1, 0, nqk - 1))),
                   blk(lambda i, j: (i, jnp.clip(j - 1 - nqk, 0, nv - 1))),
                   blk(lambda i, j: (i, jnp.clip(j - 1 - nqk - nv, 0, qw // tn - 1)))],
        out_shape=[jax.ShapeDtypeStruct((n, qw + kw), BF16), jax.ShapeDtypeStruct((n, kw), BF16),
                   jax.ShapeDtypeStruct((n, qw), BF16)],
        scratch_shapes=[pltpu.VMEM((tm, d), BF16), pltpu.VMEM((2, tm, tn), F32)],
        compiler_params=_params("parallel", "arbitrary"),
        name="gqa_proj",
    )(x, g.reshape(1, d), w, cos, sin, q_g.reshape(1, AT_HD), k_g.reshape(1, AT_HD))


AT_SAFE_LOG2_RANGE = 100.0


def _gqa_flash_kernel(bound_ref, q_ref, k_ref, v_ref, gate_ref, o_ref, *, tk):
    tq = q_ref.shape[0]
    seq = k_ref.shape[0]
    grp = AT_HEADS // AT_KV
    rows = grp * tq
    nkv = seq // tk
    qs = jnp.concatenate([q_ref[:, j * AT_HD:(j + 1) * AT_HD] for j in range(grp)], axis=0)
    cmax = bound_ref[0, 0]
    c = cmax

    def finish(acc, l):
        o = acc * (1.0 / l)
        for j in range(grp):
            cols = slice(j * AT_HD, (j + 1) * AT_HD)
            o_ref[:, cols] = (o[j * tq:(j + 1) * tq, :] * _silu(gate_ref[:, cols])).astype(o_ref.dtype)

    bounded = 2.0 * cmax <= AT_SAFE_LOG2_RANGE

    @pl.when(bounded)
    def _():
        lvec = jnp.zeros((rows, LANES), F32)
        acc = jnp.zeros((rows, AT_HD), F32)
        for t in range(nkv):
            ks = k_ref[t * tk:(t + 1) * tk, :]
            vs = v_ref[t * tk:(t + 1) * tk, :]
            p = jnp.exp2(_dot_nt(qs, ks) - c)
            for w in range(tk // LANES):
                lvec = lvec + p[:, w * LANES:(w + 1) * LANES]
            acc = acc + _dot(p.astype(BF16), vs)
        finish(acc, jnp.sum(lvec, axis=-1, keepdims=True))

    @pl.when(jnp.logical_not(bounded))
    def _():
        def body(t, carry):
            m, l, acc = carry
            ks = k_ref[pl.ds(t * tk, tk), :]
            vs = v_ref[pl.ds(t * tk, tk), :]
            s = _dot_nt(qs, ks)
            m_new = jnp.maximum(m, jnp.max(s, axis=-1, keepdims=True))
            alpha = jnp.exp2(m - m_new)
            p = jnp.exp2(s - m_new)
            l = alpha * l + jnp.sum(p, axis=-1, keepdims=True)
            acc = alpha * acc + _dot(p.astype(BF16), vs)
            return m_new, l, acc

        init = (jnp.full((rows, 1), -jnp.inf, F32), jnp.zeros((rows, 1), F32), jnp.zeros((rows, AT_HD), F32))
        _, l, acc = lax.fori_loop(0, nkv, body, init)
        finish(acc, l)


def gqa_flash(qk, v, gate, q_g, k_g, batch, seq, *, tq=2048, tk=256):
    n = qk.shape[0]
    grp = AT_HEADS // AT_KV
    gw = grp * AT_HD
    tq = min(tq, seq)
    nq = seq // tq
    bound = (AT_HD * AT_HD ** -0.5 * math.log2(math.e) * (1.0 + 2.0 ** -7)
             * jnp.max(jnp.abs(q_g)) * jnp.max(jnp.abs(k_g))).astype(F32).reshape(1, 1)
    return pl.pallas_call(
        functools.partial(_gqa_flash_kernel, tk=min(tk, seq)),
        grid=(batch, AT_KV, nq),
        in_specs=[pl.BlockSpec(memory_space=pltpu.SMEM),
                  pl.BlockSpec((tq, gw), lambda b, h, i: (b * nq + i, h)),
                  pl.BlockSpec((seq, AT_HD), lambda b, h, i: (b, AT_HEADS + h)),
                  pl.BlockSpec((seq, AT_HD), lambda b, h, i: (b, h)),
                  pl.BlockSpec((tq, gw), lambda b, h, i: (b * nq + i, h))],
        out_specs=pl.BlockSpec((tq, gw), lambda b, h, i: (b * nq + i, h)),
        out_shape=jax.ShapeDtypeStruct((n, AT_HEADS * AT_HD), BF16),
        compiler_params=_params("parallel", "parallel", "parallel"),
        name="gqa_flash",
    )(bound, qk, qk, v, gate)


def gqa_layer(x, g, w_in, q_g, k_g, w_out, batch, seq, final_g=None):
    nqk = AT_HEADS + AT_KV
    w = jnp.concatenate([_pair_major(w_in[:, :nqk * AT_HD], nqk), w_in[:, nqk * AT_HD:]], axis=1).astype(BF16)
    qk, v, gate = gqa_proj(x, g, w, _pair_major(q_g, 1), _pair_major(k_g, 1), seq)
    o = gqa_flash(qk, v, gate, q_g, k_g, batch, seq)
    return proj_out(o, w_out.astype(BF16), x, final_g)


def _t5_bucket_np(rel):
    half = REL_BUCKETS // 2
    exact = half // 2
    nabs = np.abs(rel)
    large = exact + (np.log(np.maximum(nabs, 1).astype(np.float32) / exact)
                     / math.log(REL_MAX_DIST / exact) * (half - exact)).astype(np.int32)
    large = np.minimum(large, half - 1)
    return np.where(rel > 0, half, 0) + np.where(nabs < exact, nabs, large)


DL_TQ = 128
DL_HALO = 64
DL_NK = DL_TQ + 2 * DL_HALO
DL_BATCH = 4


def _dilated_buckets():
    qi = np.arange(DL_TQ)[:, None]
    kj = np.arange(DL_NK)[None, :]
    tabs = [_t5_bucket_np((kj - DL_HALO - qi) * dil).astype(np.int32) for _, dil in DL_PAIRS]
    band = np.abs(kj - DL_HALO - qi) <= DL_HALO
    present = [sorted(set(t[band].tolist())) for t in tabs]
    return np.stack(tabs), present


def _dilated_kernel(relb_ref, bucket_ref, *refs, seq, tok, present):
    ng = len(DL_PAIRS)
    io = refs[:7 * ng]
    gate_ref, out_ref, bias_ref, kbuf, vbuf, obuf, lbuf = refs[7 * ng:]
    i = pl.program_id(1)
    p = pl.program_id(2)
    qi = lax.broadcasted_iota(jnp.int32, (DL_TQ, DL_NK), 0)
    kj = lax.broadcasted_iota(jnp.int32, (DL_TQ, DL_NK), 1)
    band = jnp.abs(kj - DL_HALO - qi) <= DL_HALO
    lane = lax.broadcasted_iota(jnp.int32, (DL_TQ, LANES), 1)
    lo = lane < DL_HD
    lo_k = lax.broadcasted_iota(jnp.int32, (DL_NK, LANES), 1) < DL_HD
    log2e = math.log2(math.e)
    scale = DL_HD ** -0.5 * log2e

    @pl.when((pl.program_id(0) == 0) & (i == 0) & (p == 0))
    def _():
        for g in range(ng):
            bk = bucket_ref[g]

            def fill(h, carry, g=g, bk=bk):
                acc = jnp.zeros((DL_TQ, DL_NK), F32)
                for b in present[g]:
                    acc = jnp.where(bk == b, relb_ref[b, h], acc)
                bias_ref[g, h] = jnp.where(band, acc * log2e, NEG_BIG)
                return carry

            lax.fori_loop(0, DL_HEADS, fill, 0)

    for g, (_, dil) in enumerate(DL_PAIRS):
        q_ref, kp_ref, kc_ref, kn_ref, vp_ref, vc_ref, vn_ref = io[7 * g:7 * g + 7]
        halo = DL_HALO * dil
        ls = seq // dil
        ppb = tok // dil
        nsub = ppb // DL_TQ
        kbuf[0:halo, :] = kp_ref[...]
        kbuf[halo:halo + tok, :] = kc_ref[...]
        kbuf[halo + tok:2 * halo + tok, :] = kn_ref[...]
        vbuf[0:halo, :] = vp_ref[...]
        vbuf[halo:halo + tok, :] = vc_ref[...]
        vbuf[halo + tok:2 * halo + tok, :] = vn_ref[...]

        def blocks(it, carry, g=g, dil=dil, ls=ls, ppb=ppb, nsub=nsub, q_ref=q_ref):
            ld = []
            for t in range(DL_BATCH):
                idx = it * DL_BATCH + t
                r = idx // nsub
                j = idx % nsub
                start = r + j * (DL_TQ * dil)
                if dil == 1:
                    qrows, krows = pl.ds(start, DL_TQ), pl.ds(start, DL_NK)
                else:
                    qrows, krows = pl.ds(start, DL_TQ, stride=dil), pl.ds(start, DL_NK, stride=dil)
                q2 = q_ref[qrows, :] * scale
                qh = [jnp.where(lo if half == 0 else ~lo, q2, 0.0).astype(BF16) for half in range(2)]
                mk = i * ppb + j * DL_TQ - DL_HALO + kj
                v2 = vbuf[krows, :].astype(BF16)
                v1 = [jnp.where(lo_k, v2, 1.0).astype(BF16), jnp.where(lo_k, 1.0, v2).astype(BF16)]
                ld.append((qrows, qh, kbuf[krows, :].astype(BF16), v1, (mk >= 0) & (mk < ls)))
            scores = [[_dot_nt(qh, k2) for qh in qhs] for _, qhs, k2, _, _ in ld]
            soft = []
            for t in range(DL_BATCH):
                inside = ld[t][4]
                per_head = []
                for half in range(2):
                    s = jnp.where(inside, scores[t][half] + bias_ref[g, 2 * p + half], NEG_BIG)
                    m = jnp.max(s, axis=-1, keepdims=True)
                    per_head.append((jnp.exp2(s - m).astype(BF16), m))
                soft.append(per_head)
            pv = [[_dot(soft[t][half][0], ld[t][3][half]) for half in range(2)] for t in range(DL_BATCH)]
            for t in range(DL_BATCH):
                qrows = ld[t][0]
                num = jnp.where(lo, pv[t][0], pv[t][1])
                den = pltpu.roll(jnp.where(lo, pv[t][1], pv[t][0]), DL_HD, axis=1)
                obuf[g, qrows, :] = num * (1.0 / den)
                lbuf[g, qrows, :] = jnp.where(lo, soft[t][0][1], soft[t][1][1]) + jnp.log2(den)
            return carry

        lax.fori_loop(0, dil * nsub // DL_BATCH, blocks, 0)

    la, lb, lc = lbuf[0], lbuf[1], lbuf[2]
    m = jnp.maximum(jnp.maximum(la, lb), lc)
    ea, eb, ec = jnp.exp2(la - m), jnp.exp2(lb - m), jnp.exp2(lc - m)
    o = (ea * obuf[0] + eb * obuf[1] + ec * obuf[2]) / (ea + eb + ec)
    out_ref[...] = (o * _silu(gate_ref[...])).astype(out_ref.dtype)


def dilated_attention(u, rel_bias, batch, seq, *, tok=2048):
    n, win = u.shape
    hw = DL_HEADS * DL_HD
    npair = hw // LANES
    tok = min(tok, seq)
    nblk = seq // tok
    max_halo = DL_HALO * max(d for _, d in DL_PAIRS)
    assert tok % (DL_TQ * max(d for _, d in DL_PAIRS)) == 0 and tok % max_halo == 0
    assert all((window // 2) // d == DL_HALO for window, d in DL_PAIRS)
    buckets, present = _dilated_buckets()

    in_specs = [pl.BlockSpec(memory_space=pltpu.SMEM),
                pl.BlockSpec((len(DL_PAIRS), DL_TQ, DL_NK), lambda b, i, p: (0, 0, 0))]
    args = [rel_bias, jnp.asarray(buckets)]
    for gi, (_, dil) in enumerate(DL_PAIRS):
        halo = DL_HALO * dil
        hpb = tok // halo
        last = n // halo - 1

        def col(c, gi=gi):
            return lambda p: (gi * 3 + c) * npair + p

        def cur(c):
            cf = col(c)
            return pl.BlockSpec((tok, LANES), lambda b, i, p: (b * nblk + i, cf(p)))

        def prev(c, hpb=hpb):
            cf = col(c)
            return pl.BlockSpec((halo, LANES), lambda b, i, p: (jnp.maximum((b * nblk + i) * hpb - 1, 0), cf(p)))

        def nxt(c, hpb=hpb, last=last):
            cf = col(c)
            return pl.BlockSpec((halo, LANES),
                                lambda b, i, p: (jnp.minimum((b * nblk + i + 1) * hpb, last), cf(p)))

        in_specs += [cur(0), prev(1), cur(1), nxt(1), prev(2), cur(2), nxt(2)]
        args += [u] * 7
    gate_col0 = 3 * len(DL_PAIRS) * npair
    in_specs.append(pl.BlockSpec((tok, LANES), lambda b, i, p: (b * nblk + i, gate_col0 + p)))
    args.append(u)
    return pl.pallas_call(
        functools.partial(_dilated_kernel, seq=seq, tok=tok, present=present),
        grid=(batch, nblk, npair),
        in_specs=in_specs,
        out_specs=pl.BlockSpec((tok, LANES), lambda b, i, p: (b * nblk + i, p)),
        out_shape=jax.ShapeDtypeStruct((n, hw), BF16),
        scratch_shapes=[pltpu.VMEM((len(DL_PAIRS), DL_HEADS, DL_TQ, DL_NK), F32),
                        pltpu.VMEM((tok + 2 * max_halo, LANES), F32),
                        pltpu.VMEM((tok + 2 * max_halo, LANES), F32),
                        pltpu.VMEM((len(DL_PAIRS), tok, LANES), F32),
                        pltpu.VMEM((len(DL_PAIRS), tok, LANES), F32)],
        compiler_params=_params("arbitrary", "arbitrary", "arbitrary"),
        name="dilated_attention",
    )(*args)


def dilated_layer(x, g, rel_bias, w_in, w_out, batch, seq, final_g=None):
    u = proj_in(x, g, w_in.astype(BF16))
    o = dilated_attention(u, rel_bias, batch, seq)
    return proj_out(o, w_out.astype(BF16), x, final_g)


def kernel(x, norm_g, final_g, rel_bias, hgrn_lb, ssd_w_in, ssd_conv_w, ssd_conv_b, ssd_dt_bias, ssd_a_log, ssd_d,
           ssd_norm_g, ssd_w_out, hg_w_in, hg_norm_g, hg_w_out, at_w_in, at_q_norm_g, at_k_norm_g, at_w_out,
           dl_w_in, dl_w_out):
    batch, seq, d = x.shape
    depth = norm_g.shape[0]
    n_mixers = 4
    lb_sm = jax.nn.softmax(hgrn_lb.astype(F32), axis=0)
    lb_all = jnp.cumsum(lb_sm, axis=0) - lb_sm[0:1]
    h = x.reshape(batch * seq, d)
    for layer in range(depth):
        kind, slot = layer % n_mixers, layer // n_mixers
        fg = final_g if layer == depth - 1 else None
        if kind == 0:
            h = ssd_layer(h, norm_g[layer], ssd_w_in[slot], ssd_conv_w[slot], ssd_conv_b[slot], ssd_dt_bias[slot],
                          ssd_a_log[slot], ssd_d[slot], ssd_norm_g[slot], ssd_w_out[slot], batch, seq, fg)
        elif kind == 1:
            h = hgrn_layer(h, norm_g[layer], lb_all[layer], hg_w_in[slot], hg_norm_g[slot], hg_w_out[slot],
                           batch, seq, fg)
        elif kind == 2:
            h = gqa_layer(h, norm_g[layer], at_w_in[slot], at_q_norm_g[slot], at_k_norm_g[slot], at_w_out[slot],
                          batch, seq, fg)
        else:
            h = dilated_layer(h, norm_g[layer], rel_bias, dl_w_in[slot], dl_w_out[slot], batch, seq, fg)
    return h.reshape(batch, seq, d)
```
